```python
import jax
import jax.numpy as jnp
from jax import lax
import numpy as np

D_MODEL = 1024
BATCH = 4
SEQ = 8192
DEPTH = 1

D_MIX = D_MODEL
ATTN_HEAD_DIM = 64
ATTN_HEADS = (D_MIX // 2) // ATTN_HEAD_DIM
ATTN_WIDTH = ATTN_HEADS * ATTN_HEAD_DIM
DILATED_PATTERNS = ((128, 1), (512, 4), (2048, 16))
ALIBI_MAX_BIAS = 8.0
HGRN_KEY_DIM = 128
HGRN_VAL_DIM = 128
HGRN_HEADS = (D_MIX - ATTN_WIDTH) // HGRN_VAL_DIM
HGRN_WIDTH = HGRN_HEADS * HGRN_VAL_DIM
HGRN_FDIM = HGRN_HEADS * HGRN_KEY_DIM
HGRN_CHUNK = 64
IN_SPLITS = (ATTN_WIDTH, ATTN_WIDTH, ATTN_WIDTH, HGRN_FDIM, HGRN_FDIM, HGRN_FDIM, HGRN_WIDTH, HGRN_WIDTH)
D_IN = 3 * ATTN_WIDTH + 3 * HGRN_FDIM + 2 * HGRN_WIDTH
N_EXPERTS = 16
CAPACITY_FACTOR = 2
D_FF_EXPERT = D_MODEL
NORM_EPS = 1e-6
NEG_INF = -1e30

kernel_name = "hybrid_dilated_attn_hgrn2_ecmoe"


def rms_norm(x, w):
    xf = x.astype(jnp.float32)
    y = xf * lax.rsqrt(jnp.mean(xf * xf, axis=-1, keepdims=True) + NORM_EPS)
    return (y * w.astype(jnp.float32)).astype(x.dtype)


def alibi_slopes(n_heads):
    return jnp.exp2(-ALIBI_MAX_BIAS * jnp.arange(1, n_heads + 1, dtype=jnp.float32) / n_heads)


def dilated_window_attention(q, k, v, window, dilation, slopes):
    B, S, H, Dh = q.shape
    half = window // (2 * dilation)
    blk = half
    span = dilation * blk
    Sp = -(-S // span) * span
    L = Sp // dilation
    nb = L // blk

    def to_sub(t):
        t = jnp.pad(t, ((0, 0), (0, Sp - S), (0, 0), (0, 0)))
        return t.reshape(B, L, dilation, H, Dh).transpose(0, 2, 1, 3, 4)

    def windows(t):
        t = jnp.pad(t, ((0, 0), (0, 0), (blk, blk), (0, 0), (0, 0))).reshape(B, dilation, nb + 2, blk, H, Dh)
        return jnp.concatenate([t[:, :, :-2], t[:, :, 1:-1], t[:, :, 2:]], axis=3)

    qs = to_sub(q).reshape(B, dilation, nb, blk, H, Dh)
    ks = windows(to_sub(k))
    vs = windows(to_sub(v))

    qi = jnp.arange(blk)
    kj = jnp.arange(3 * blk)
    delta = kj[None, :] - blk - qi[:, None]
    band = jnp.abs(delta) <= half
    l_k = jnp.arange(nb)[:, None] * blk - blk + kj[None, :]
    pos_k = l_k[None] * dilation + jnp.arange(dilation)[:, None, None]
    key_ok = (l_k[None] >= 0) & (pos_k < S)
    mask = band[None, None, None] & key_ok[:, :, None, None, :]
    bias = -slopes[:, None, None] * (dilation * jnp.abs(delta)).astype(jnp.float32)[None]

    s = jnp.einsum('brnqhd,brnkhd->brnhqk', qs, ks)
    s = jnp.where(mask, s + bias, NEG_INF)
    m = jnp.max(s, axis=-1, keepdims=True)
    e = jnp.exp(s - m)
    den = jnp.sum(e, axis=-1)
    num = jnp.einsum('brnhqk,brnkhd->brnqhd', e, vs)

    def back_stat(t):
        t = t.transpose(0, 1, 2, 4, 3).reshape(B, dilation, L, H)
        return t.transpose(0, 2, 1, 3).reshape(B, Sp, H)[:, :S]

    num = num.reshape(B, dilation, L, H, Dh).transpose(0, 2, 1, 3, 4).reshape(B, Sp, H, Dh)[:, :S]
    return back_stat(m[..., 0]), back_stat(den), num


def dilated_attention_mixer(q, k, v, q_norm_w, k_norm_w):
    B, S, _ = q.shape
    shp = (B, S, ATTN_HEADS, ATTN_HEAD_DIM)
    qh = rms_norm(q.reshape(shp), q_norm_w).astype(jnp.float32) * (ATTN_HEAD_DIM ** -0.5)
    kh = rms_norm(k.reshape(shp), k_norm_w).astype(jnp.float32)
    vh = v.reshape(shp).astype(jnp.float32)
    slopes = alibi_slopes(ATTN_HEADS)
    ms, dens, nums = [], [], []
    for window, dilation in DILATED_PATTERNS:
        m, den, num = dilated_window_attention(qh, kh, vh, window, dilation, slopes)
        ms.append(m)
        dens.append(den)
        nums.append(num)
    m_all = jnp.stack(ms)
    scale = jnp.exp(m_all - jnp.max(m_all, axis=0, keepdims=True))
    numer = jnp.sum(scale[..., None] * jnp.stack(nums), axis=0)
    denom = jnp.sum(scale * jnp.stack(dens), axis=0)
    return (numer / denom[..., None]).reshape(B, S, ATTN_WIDTH)


def chunked_gated_recurrence(q, k, v, log_f):
    B, S, H, Dk = q.shape
    Dv = v.shape[-1]
    C = HGRN_CHUNK
    n = S // C

    def to_chunks(t):
        return t.reshape(B, n, C, H, t.shape[-1]).transpose(1, 0, 3, 2, 4)

    incl = jnp.tril(jnp.ones((C, C), dtype=bool))[:, :, None]

    def step(state, inp):
        qb, kb, vb, gb = inp
        b = jnp.cumsum(gb, axis=2)
        o_inter = jnp.einsum('bhck,bhkv->bhcv', qb * jnp.exp(b), state)
        diff = b[:, :, :, None, :] - b[:, :, None, :, :]
        decay = jnp.exp(jnp.where(incl, diff, -jnp.inf))
        a = jnp.einsum('bhtk,bhsk,bhtsk->bhts', qb, kb, decay)
        o_intra = jnp.einsum('bhts,bhsv->bhtv', a, vb)
        b_last = b[:, :, -1:, :]
        k_dec = kb * jnp.exp(b_last - b)
        state = jnp.exp(b_last[:, :, 0, :])[..., None] * state + jnp.einsum('bhsk,bhsv->bhkv', k_dec, vb)
        return state, o_inter + o_intra

    init = jnp.zeros((B, H, Dk, Dv), jnp.float32)
    _, out = lax.scan(step, init, (to_chunks(q), to_chunks(k), to_chunks(v), to_chunks(log_f)))
    return out.transpose(1, 0, 3, 2, 4).reshape(B, S, H, Dv)


def hgrn2_mixer(q, f_fwd, f_bwd, i, g, lb_fwd, lb_bwd, out_norm_w):
    B, S, _ = q.shape

    def heads(t, d):
        return t.reshape(B, S, HGRN_HEADS, d).astype(jnp.float32)

    qh = jax.nn.silu(heads(q, HGRN_KEY_DIM))
    vh = heads(i, HGRN_VAL_DIM)

    def gates(f_raw, lb):
        lb = lb.reshape(HGRN_HEADS, HGRN_KEY_DIM)
        z = heads(f_raw, HGRN_KEY_DIM)
        f = lb + (1.0 - lb) * jax.nn.sigmoid(z)
        return (1.0 - lb) * jax.nn.sigmoid(-z), jnp.log(f)

    k_f, lf_f = gates(f_fwd, lb_fwd)
    k_b, lf_b = gates(f_bwd, lb_bwd)
    o_f = chunked_gated_recurrence(qh, k_f, vh, lf_f)
    flip = lambda t: jnp.flip(t, axis=1)
    o_b = flip(chunked_gated_recurrence(flip(qh), flip(k_b), flip(vh), flip(lf_b)))
    o = rms_norm(o_f + o_b, out_norm_w) * jax.nn.silu(heads(g, HGRN_VAL_DIM))
    return o.reshape(B, S, HGRN_WIDTH)


def expert_choice_moe(h, w_router, w_gate, w_up, w_down):
    B, S, D = h.shape
    cap = max(1, CAPACITY_FACTOR * S // N_EXPERTS)
    logits = jnp.einsum('bsd,de->bse', h, w_router).astype(jnp.float32)
    aff = jax.nn.softmax(logits, axis=-1)
    gate, idx = lax.top_k(aff.transpose(0, 2, 1), cap)
    xin = jax.vmap(lambda hb, ib: hb[ib])(h, idx)
    hid = jax.nn.silu(jnp.einsum('becd,edf->becf', xin, w_gate)) * jnp.einsum('becd,edf->becf', xin, w_up)
    y = jnp.einsum('becf,efd->becd', hid, w_down) * gate[..., None].astype(h.dtype)
    bidx = jnp.arange(B)[:, None, None]
    return jnp.zeros_like(h).at[bidx, idx].add(y)


def setup_inputs(seed: int = 0) -> dict:
    key = jax.random.key(seed)
    ks = jax.random.split(key, 16)
    f32 = jnp.float32
    nrm = lambda k, shape, fan_in: jax.random.normal(k, shape, f32) * (fan_in ** -0.5)
    gain = lambda k, shape: 1.0 + 0.02 * jax.random.normal(k, shape, f32)
    return {
        "x": jax.random.normal(ks[0], (BATCH, SEQ, D_MODEL), f32),
        "norm1_w": gain(ks[1], (DEPTH, D_MODEL)),
        "w_in": nrm(ks[2], (DEPTH, D_MODEL, D_IN), D_MODEL),
        "attn_q_norm_w": gain(ks[3], (DEPTH, ATTN_HEAD_DIM)),
        "attn_k_norm_w": gain(ks[4], (DEPTH, ATTN_HEAD_DIM)),
        "hgrn_lb_fwd": 1.0 + 0.1 * jax.random.normal(ks[5], (DEPTH + 1, HGRN_FDIM), f32),
        "hgrn_lb_bwd": 1.0 + 0.1 * jax.random.normal(ks[6], (DEPTH + 1, HGRN_FDIM), f32),
        "hgrn_out_norm_w": gain(ks[7], (DEPTH, HGRN_VAL_DIM)),
        "w_out": nrm(ks[8], (DEPTH, D_MIX, D_MODEL), D_MIX),
        "norm2_w": gain(ks[9], (DEPTH, D_MODEL)),
        "w_router": nrm(ks[10], (DEPTH, D_MODEL, N_EXPERTS), D_MODEL),
        "w_expert_gate": nrm(ks[11], (DEPTH, N_EXPERTS, D_MODEL, D_FF_EXPERT), D_MODEL),
        "w_expert_up": nrm(ks[12], (DEPTH, N_EXPERTS, D_MODEL, D_FF_EXPERT), D_MODEL),
        "w_expert_down": nrm(ks[13], (DEPTH, N_EXPERTS, D_FF_EXPERT, D_MODEL), D_FF_EXPERT),
    }


def reference(x, norm1_w, w_in, attn_q_norm_w, attn_k_norm_w, hgrn_lb_fwd, hgrn_lb_bwd,
              hgrn_out_norm_w, w_out, norm2_w, w_router, w_expert_gate, w_expert_up, w_expert_down):
    lb_f_all = jnp.cumsum(jax.nn.softmax(hgrn_lb_fwd.astype(jnp.float32), axis=0), axis=0)
    lb_b_all = jnp.cumsum(jax.nn.softmax(hgrn_lb_bwd.astype(jnp.float32), axis=0), axis=0)
    split_at = np.cumsum(IN_SPLITS)[:-1].tolist()
    for l in range(DEPTH):
        h = rms_norm(x, norm1_w[l])
        proj = jnp.einsum('bsd,de->bse', h, w_in[l])
        aq, ak, av, hq, hf_f, hf_b, hi, hg = jnp.split(proj, split_at, axis=-1)
        a_out = dilated_attention_mixer(aq, ak, av, attn_q_norm_w[l], attn_k_norm_w[l])
        b_out = hgrn2_mixer(hq, hf_f, hf_b, hi, hg, lb_f_all[l], lb_b_all[l], hgrn_out_norm_w[l])
        mixed = jnp.concatenate([a_out.astype(x.dtype), b_out.astype(x.dtype)], axis=-1)
        x = x + jnp.einsum('bsm,md->bsd', mixed, w_out[l])
        x = x + expert_choice_moe(rms_norm(x, norm2_w[l]), w_router[l], w_expert_gate[l],
                                  w_expert_up[l], w_expert_down[l])
    return x
```

```python
import functools

import jax
import jax.numpy as jnp
from jax import lax
from jax.experimental import pallas as pl
from jax.experimental.pallas import tpu as pltpu

F32 = jnp.float32
BF16 = jnp.bfloat16

NORM_EPS = 1e-6
NEG_BIG = -1e30
ATTN_HEAD_DIM = 64
ATTN_HEADS = 8
ATTN_WIDTH = ATTN_HEADS * ATTN_HEAD_DIM
DILATED_PATTERNS = ((128, 1), (512, 4), (2048, 16))
ALIBI_MAX_BIAS = 8.0
HGRN_DIM = 128
HGRN_HEADS = 4
HGRN_WIDTH = HGRN_HEADS * HGRN_DIM
N_EXPERTS = 16
CAPACITY_FACTOR = 2
V7X_VMEM_LIMIT_BYTES = 56 * 1024 * 1024


def _dot(a, b, dims=((1,), (0,))):
    return lax.dot_general(a, b, (dims, ((), ())), preferred_element_type=F32)


def _dot_nt(a, b):
    return _dot(a, b, ((1,), (1,)))


def _in_proj_kernel(x_ref, nw_ref, w_ref, o_ref, *, n_chunk):
    x = x_ref[...]
    h = x * lax.rsqrt(jnp.mean(x * x, axis=-1, keepdims=True) + NORM_EPS) * nw_ref[...]
    h = h.astype(BF16)
    n_total = o_ref.shape[1]
    for c in range(n_total // n_chunk):
        cols = slice(c * n_chunk, (c + 1) * n_chunk)
        o_ref[:, cols] = _dot(h, w_ref[:, cols])


def in_proj(x2d, norm_w, w_bf16, *, tm=256, n_chunk=512):
    m, d = x2d.shape
    n = w_bf16.shape[1]
    return pl.pallas_call(
        functools.partial(_in_proj_kernel, n_chunk=n_chunk),
        grid=(m // tm,),
        in_specs=[
            pl.BlockSpec((tm, d), lambda i: (i, 0)),
            pl.BlockSpec((1, d), lambda i: (0, 0)),
            pl.BlockSpec((d, n), lambda i: (0, 0)),
        ],
        out_specs=pl.BlockSpec((tm, n), lambda i: (i, 0)),
        out_shape=jax.ShapeDtypeStruct((m, n), F32),
        compiler_params=pltpu.CompilerParams(
            dimension_semantics=("arbitrary",), vmem_limit_bytes=V7X_VMEM_LIMIT_BYTES),
    )(x2d, norm_w.reshape(1, d), w_bf16)


HGRN_CHUNK = 64
HGRN_SUB = 16


def _split3(g):
    hi = g.astype(BF16)
    r1 = g - hi.astype(F32)
    mid = r1.astype(BF16)
    lo = (r1 - mid.astype(F32)).astype(BF16)
    return hi, mid, lo


def _hgrn_chunk(q_raw, z, v, lb, st_ref, *, reverse):
    c, dk = q_raw.shape
    sub = HGRN_SUB
    q = q_raw * jax.nn.sigmoid(q_raw)
    f = lb + (1.0 - lb) * jax.nn.sigmoid(z)
    k = (1.0 - lb) * jax.nn.sigmoid(-z)
    g = jnp.log(f)

    row = lax.broadcasted_iota(jnp.int32, (c, c), 0)
    col = lax.broadcasted_iota(jnp.int32, (c, c), 1)
    tri = (col >= row) if reverse else (col <= row)
    tri_bf = jnp.where(tri, 1.0, 0.0).astype(BF16)
    g_hi, g_mid, g_lo = _split3(g)
    cum = _dot(tri_bf, g_hi) + _dot(tri_bf, g_mid) + _dot(tri_bf, g_lo)

    def edge(r):
        return cum[r:r + 1, :]

    if reverse:
        ref_half, ref_q_lo, ref_q_hi, ref_end = edge(32), edge(16), edge(48), edge(0)
    else:
        ref_half, ref_q_lo, ref_q_hi, ref_end = edge(31), edge(15), edge(47), edge(c - 1)

    r1 = lax.broadcasted_iota(jnp.int32, (c, 1), 0)
    ref_quarter = jnp.where(r1 < 32, ref_q_lo, ref_q_hi)

    st = st_ref[...]
    qe = (q * jnp.exp(cum)).astype(BF16)
    o = _dot_nt(qe, st.astype(BF16))
    ke = (k * jnp.exp(ref_end - cum)).astype(BF16)
    v_bf = v.astype(BF16)
    st_ref[...] = st * jnp.exp(ref_end) + _dot(v_bf, ke, ((0,), (0,)))

    q1 = (q * jnp.exp(jnp.minimum(cum - ref_half, 0.0))).astype(BF16)
    k1 = (k * jnp.exp(jnp.minimum(ref_half - cum, 0.0))).astype(BF16)
    q2 = (q * jnp.exp(jnp.minimum(cum - ref_quarter, 0.0))).astype(BF16)
    k2 = (k * jnp.exp(jnp.minimum(ref_quarter - cum, 0.0))).astype(BF16)
    tb, sb = row // sub, col // sub
    if reverse:
        m1 = (tb < 2) & (sb >= 2)
        m2 = ((tb == 0) & (sb == 1)) | ((tb == 2) & (sb == 3))
    else:
        m1 = (tb >= 2) & (sb < 2)
        m2 = ((tb == 1) & (sb == 0)) | ((tb == 3) & (sb == 2))
    a = jnp.where(m1, _dot_nt(q1, k1), 0.0) + jnp.where(m2, _dot_nt(q2, k2), 0.0)

    t_loc = lax.broadcasted_iota(jnp.int32, (sub, 1), 0)
    lane = lax.broadcasted_iota(jnp.int32, (sub, c), 1)
    blocks = []
    for blk in range(c // sub):
        rows = slice(blk * sub, (blk + 1) * sub)
        cum_b, q_b = cum[rows], q[rows]
        a_b = jnp.zeros((sub, c), F32)
        for s_loc in range(sub):
            s = blk * sub + s_loc
            keep = (t_loc <= s_loc) if reverse else (t_loc >= s_loc)
            e = jnp.exp(jnp.where(keep, cum_b - cum[s:s + 1, :], NEG_BIG))
            p = (q_b * k[s:s + 1, :]) * e
            a_b = jnp.where(lane == s, jnp.sum(p, axis=-1, keepdims=True), a_b)
        blocks.append(a_b)
    a = a + jnp.concatenate(blocks, axis=0)
    return o + _dot(a.astype(BF16), v_bf)


def _hgrn_kernel(qf_ref, zf_ref, vf_ref, qb_ref, zb_ref, vb_ref, lbf_ref, lbb_ref,
                 of_ref, ob_ref, sf_ref, sb_ref):
    @pl.when(pl.program_id(2) == 0)
    def _():
        sf_ref[...] = jnp.zeros_like(sf_ref)
        sb_ref[...] = jnp.zeros_like(sb_ref)

    n_chunks = qf_ref.shape[0] // HGRN_CHUNK
    lbf, lbb = lbf_ref[...], lbb_ref[...]

    def body(ci, carry):
        rf = pl.ds(pl.multiple_of(ci * HGRN_CHUNK, HGRN_CHUNK), HGRN_CHUNK)
        of_ref[rf, :] = _hgrn_chunk(qf_ref[rf, :], zf_ref[rf, :], vf_ref[rf, :], lbf, sf_ref, reverse=False)
        rb = pl.ds(pl.multiple_of((n_chunks - 1 - ci) * HGRN_CHUNK, HGRN_CHUNK), HGRN_CHUNK)
        ob_ref[rb, :] = _hgrn_chunk(qb_ref[rb, :], zb_ref[rb, :], vb_ref[rb, :], lbb, sb_ref, reverse=True)
        return carry

    lax.fori_loop(0, n_chunks, body, 0)


def hgrn(proj, lb_f, lb_b, *, batch, seq, col0, tile=512):
    nt = seq // tile
    hb = HGRN_HEADS
    cb = col0 // HGRN_DIM

    def fwd(colblock):
        return pl.BlockSpec((tile, HGRN_DIM), lambda b, h, i: (b * nt + i, cb + colblock * hb + h))

    def bwd(colblock):
        return pl.BlockSpec((tile, HGRN_DIM), lambda b, h, i: (b * nt + nt - 1 - i, cb + colblock * hb + h))

    lb_spec = pl.BlockSpec((1, HGRN_DIM), lambda b, h, i: (0, h))
    out_shape = jax.ShapeDtypeStruct((batch * seq, HGRN_WIDTH), F32)
    return pl.pallas_call(
        _hgrn_kernel,
        grid=(batch, hb, nt),
        in_specs=[fwd(0), fwd(1), fwd(3), bwd(0), bwd(2), bwd(3), lb_spec, lb_spec],
        out_specs=[
            pl.BlockSpec((tile, HGRN_DIM), lambda b, h, i: (b * nt + i, h)),
            pl.BlockSpec((tile, HGRN_DIM), lambda b, h, i: (b * nt + nt - 1 - i, h)),
        ],
        out_shape=[out_shape, out_shape],
        scratch_shapes=[pltpu.VMEM((HGRN_DIM, HGRN_DIM), F32), pltpu.VMEM((HGRN_DIM, HGRN_DIM), F32)],
        compiler_params=pltpu.CompilerParams(
            dimension_semantics=("arbitrary", "arbitrary", "arbitrary"),
            vmem_limit_bytes=V7X_VMEM_LIMIT_BYTES),
    )(proj, proj, proj, proj, proj, proj, lb_f.reshape(1, -1), lb_b.reshape(1, -1))


ATTN_HALF = 64
ATTN_QT = 128
ATTN_KT = ATTN_QT + 2 * ATTN_HALF
LANES = 128


def _head_rms(x, w):
    lo = lax.broadcasted_iota(jnp.int32, (1, LANES), 1) < ATTN_HEAD_DIM
    outs = []
    for sl in range(ATTN_WIDTH // LANES):
        xs = x[:, sl * LANES:(sl + 1) * LANES]
        sq = xs * xs
        s_lo = jnp.sum(jnp.where(lo, sq, 0.0), axis=-1, keepdims=True)
        s_hi = jnp.sum(jnp.where(lo, 0.0, sq), axis=-1, keepdims=True)
        ms = jnp.where(lo, s_lo, s_hi) * (1.0 / ATTN_HEAD_DIM)
        outs.append(xs * lax.rsqrt(ms + NORM_EPS))
    return jnp.concatenate(outs, axis=-1) * w


def _attn_kernel(q_ref, k_ref, kp_ref, kn_ref, v_ref, vp_ref, vn_ref, qw_ref, kw_ref, bias_ref,
                 o_ref, lse_ref, kwin, vwin, qn, *, sub_len):
    tile = q_ref.shape[0]
    i = pl.program_id(2)
    qw, kw = qw_ref[...], kw_ref[...]
    kwin[0:ATTN_HALF, :] = _head_rms(kp_ref[...], kw).astype(BF16)
    kwin[ATTN_HALF:ATTN_HALF + tile, :] = _head_rms(k_ref[...], kw).astype(BF16)
    kwin[ATTN_HALF + tile:, :] = _head_rms(kn_ref[...], kw).astype(BF16)
    vwin[0:ATTN_HALF, :] = vp_ref[...].astype(BF16)
    vwin[ATTN_HALF:ATTN_HALF + tile, :] = v_ref[...].astype(BF16)
    vwin[ATTN_HALF + tile:, :] = vn_ref[...].astype(BF16)
    qn[...] = (_head_rms(q_ref[...], qw) * (ATTN_HEAD_DIM ** -0.5)).astype(BF16)

    lane = lax.broadcasted_iota(jnp.int32, (1, LANES), 1)
    lo = lane < ATTN_HEAD_DIM
    kcol = lax.broadcasted_iota(jnp.int32, (1, ATTN_KT), 1)

    def sub(j, carry):
        r0 = pl.multiple_of(j * ATTN_QT, ATTN_QT)
        lk = i * tile + j * ATTN_QT - ATTN_HALF + kcol
        edge = jnp.where((lk >= 0) & (lk < sub_len), 0.0, NEG_BIG)
        lse_tile = jnp.zeros((ATTN_QT, LANES), F32)
        for sl in range(ATTN_WIDTH // LANES):
            cols = slice(sl * LANES, (sl + 1) * LANES)
            qs = qn[pl.ds(r0, ATTN_QT), cols]
            ks = kwin[pl.ds(r0, ATTN_KT), cols]
            vs = vwin[pl.ds(r0, ATTN_KT), cols]
            o_slab = None
            for hh in range(2):
                h = 2 * sl + hh
                qh = jnp.where(lo if hh == 0 else jnp.logical_not(lo), qs, jnp.zeros_like(qs))
                s = _dot_nt(qh, ks) + bias_ref[h] + edge
                m = jnp.max(s, axis=-1, keepdims=True)
                p = jnp.exp(s - m)
                den = jnp.sum(p, axis=-1, keepdims=True)
                oh = _dot(p.astype(BF16), vs) / den
                o_slab = oh if hh == 0 else jnp.where(lo, o_slab, oh)
                lse_tile = jnp.where(lane == h, m + jnp.log(den), lse_tile)
            o_ref[pl.ds(r0, ATTN_QT), cols] = o_slab
        lse_ref[pl.ds(r0, ATTN_QT), :] = lse_tile
        return carry

    lax.fori_loop(0, tile // ATTN_QT, sub, 0)


def _attn_bias(dilation):
    slopes = jnp.exp2(-ALIBI_MAX_BIAS * jnp.arange(1, ATTN_HEADS + 1, dtype=F32) / ATTN_HEADS)
    t = jnp.arange(ATTN_QT)[:, None]
    j = jnp.arange(ATTN_KT)[None, :]
    dist = jnp.abs(j - ATTN_HALF - t)
    alibi = -slopes[:, None, None] * (dilation * dist).astype(F32)[None]
    return jnp.where((dist <= ATTN_HALF)[None], alibi, NEG_BIG)


def attn_pattern(proj, qw, kw, *, batch, seq, d_in, dilation, tile=512):
    sub_len = seq // dilation
    tile = min(tile, sub_len)
    nt = sub_len // tile
    cpp = d_in // ATTN_WIDTH
    view = proj.reshape(batch * sub_len, dilation * d_in)
    hb = tile // ATTN_HALF
    nhb = sub_len // ATTN_HALF

    def main(c):
        return pl.BlockSpec((tile, ATTN_WIDTH), lambda b, r, i: (b * nt + i, r * cpp + c))

    def prev(c):
        return pl.BlockSpec((ATTN_HALF, ATTN_WIDTH),
                            lambda b, r, i: (b * nhb + jnp.maximum(i * hb - 1, 0), r * cpp + c))

    def nxt(c):
        return pl.BlockSpec((ATTN_HALF, ATTN_WIDTH),
                            lambda b, r, i: (b * nhb + jnp.minimum((i + 1) * hb, nhb - 1), r * cpp + c))

    wspec = pl.BlockSpec((1, ATTN_WIDTH), lambda b, r, i: (0, 0))
    o, lse = pl.pallas_call(
        functools.partial(_attn_kernel, sub_len=sub_len),
        grid=(batch, dilation, nt),
        in_specs=[main(0), main(1), prev(1), nxt(1), main(2), prev(2), nxt(2), wspec, wspec,
                  pl.BlockSpec((ATTN_HEADS, ATTN_QT, ATTN_KT), lambda b, r, i: (0, 0, 0))],
        out_specs=[pl.BlockSpec((tile, ATTN_WIDTH), lambda b, r, i: (b * nt + i, r)),
                   pl.BlockSpec((tile, LANES), lambda b, r, i: (b * nt + i, r))],
        out_shape=[jax.ShapeDtypeStruct((batch * sub_len, dilation * ATTN_WIDTH), F32),
                   jax.ShapeDtypeStruct((batch * sub_len, dilation * LANES), F32)],
        scratch_shapes=[pltpu.VMEM((tile + 2 * ATTN_HALF, ATTN_WIDTH), BF16),
                        pltpu.VMEM((tile + 2 * ATTN_HALF, ATTN_WIDTH), BF16),
                        pltpu.VMEM((tile, ATTN_WIDTH), BF16)],
        compiler_params=pltpu.CompilerParams(
            dimension_semantics=("arbitrary", "arbitrary", "arbitrary"),
            vmem_limit_bytes=V7X_VMEM_LIMIT_BYTES),
    )(view, view, view, view, view, view, view,
      jnp.tile(qw, ATTN_HEADS).reshape(1, -1), jnp.tile(kw, ATTN_HEADS).reshape(1, -1), _attn_bias(dilation))
    return o.reshape(batch * seq, ATTN_WIDTH), lse.reshape(batch * seq, LANES)


def _split2(x):
    hi = x.astype(BF16)
    return hi, (x - hi.astype(F32)).astype(BF16)


def _out_proj_kernel(x_ref, o1_ref, o2_ref, o3_ref, l1_ref, l2_ref, l3_ref, of_ref, ob_ref, hg_ref,
                     hw_ref, expand_ref, wo_ref, n2_ref, wr_hi_ref, wr_lo_ref,
                     x1_ref, h2_ref, aff_ref):
    lses = [l1_ref[...], l2_ref[...], l3_ref[...]]
    mx = jnp.maximum(jnp.maximum(lses[0], lses[1]), lses[2])
    ws = [jnp.exp(l - mx) for l in lses]
    inv = 1.0 / (ws[0] + ws[1] + ws[2])
    expand = expand_ref[...]
    a_out = None
    for w, o_ref in zip(ws, (o1_ref, o2_ref, o3_ref)):
        w_hi, w_lo = _split2(w * inv)
        term = (_dot(w_hi, expand) + _dot(w_lo, expand)) * o_ref[...]
        a_out = term if a_out is None else a_out + term

    o = of_ref[...] + ob_ref[...]
    hg = hg_ref[...]
    hw = hw_ref[...]
    b_parts = []
    for sl in range(HGRN_HEADS):
        cols = slice(sl * HGRN_DIM, (sl + 1) * HGRN_DIM)
        os_ = o[:, cols]
        y = os_ * lax.rsqrt(jnp.mean(os_ * os_, axis=-1, keepdims=True) + NORM_EPS) * hw
        g = hg[:, cols]
        b_parts.append(y * (g * jax.nn.sigmoid(g)))
    mixed = jnp.concatenate([a_out] + b_parts, axis=-1).astype(BF16)

    x1 = x_ref[...] + _dot(mixed, wo_ref[...])
    x1_ref[...] = x1
    h2 = x1 * lax.rsqrt(jnp.mean(x1 * x1, axis=-1, keepdims=True) + NORM_EPS) * n2_ref[...]
    h2_ref[...] = h2.astype(BF16)

    h_hi, h_lo = _split2(h2)
    wr_hi = wr_hi_ref[...]
    logits = _dot(h_hi, wr_hi) + _dot(h_lo, wr_hi) + _dot(h_hi, wr_lo_ref[...])
    valid = lax.broadcasted_iota(jnp.int32, (1, LANES), 1) < N_EXPERTS
    logits = jnp.where(valid, logits, NEG_BIG)
    ex = jnp.exp(logits - jnp.max(logits, axis=-1, keepdims=True))
    aff_ref[...] = ex / jnp.sum(ex, axis=-1, keepdims=True)


def out_proj(x2d, attn_outs, o_f, o_b, proj, hgrn_norm_w, w_out_bf16, norm2_w, w_router, *, hg_col, tm=256):
    m, d = x2d.shape
    (o1, l1), (o2, l2), (o3, l3) = attn_outs
    head = jnp.arange(ATTN_WIDTH) // ATTN_HEAD_DIM
    expand = (jnp.arange(LANES)[:, None] == head[None, :]).astype(BF16)
    wr = jnp.pad(w_router, ((0, 0), (0, LANES - N_EXPERTS)))
    wr_hi, wr_lo = _split2(wr)

    def rows(width, colblock=0):
        return pl.BlockSpec((tm, width), lambda i: (i, colblock))

    def const(shape):
        return pl.BlockSpec(shape, lambda i: (0, 0))

    return pl.pallas_call(
        _out_proj_kernel,
        grid=(m // tm,),
        in_specs=[rows(d), rows(ATTN_WIDTH), rows(ATTN_WIDTH), rows(ATTN_WIDTH),
                  rows(LANES), rows(LANES), rows(LANES),
                  rows(HGRN_WIDTH), rows(HGRN_WIDTH), rows(HGRN_WIDTH, hg_col // HGRN_WIDTH),
                  const((1, HGRN_DIM)), const((LANES, ATTN_WIDTH)), const((ATTN_WIDTH + HGRN_WIDTH, d)),
                  const((1, d)), const((d, LANES)), const((d, LANES))],
        out_specs=[rows(d), rows(d), rows(LANES)],
        out_shape=[jax.ShapeDtypeStruct((m, d), F32), jax.ShapeDtypeStruct((m, d), BF16),
                   jax.ShapeDtypeStruct((m, LANES), F32)],
        compiler_params=pltpu.CompilerParams(
            dimension_semantics=("arbitrary",), vmem_limit_bytes=V7X_VMEM_LIMIT_BYTES),
    )(x2d, o1, o2, o3, l1, l2, l3, o_f, o_b, proj, hgrn_norm_w.reshape(1, -1), expand, w_out_bf16,
      norm2_w.reshape(1, -1), wr_hi, wr_lo)


ROUTE_BLOCK = 128
COUNT_ROWS = 512


def _routing_kernel(aff_ref, slot_ref, start_ref, *, cap):
    seq = aff_ref.shape[0]

    def bits_at(r0, n):
        return pltpu.bitcast(aff_ref[pl.ds(r0, n), :], jnp.int32)

    def count(pred):
        def body(c, acc):
            blk = bits_at(pl.multiple_of(c * COUNT_ROWS, COUNT_ROWS), COUNT_ROWS)
            return acc + jnp.sum(jnp.where(pred(blk), 1, 0), axis=0, keepdims=True)
        return lax.fori_loop(0, seq // COUNT_ROWS, body, jnp.zeros((1, LANES), jnp.int32))

    def bit_step(t, thr):
        cand = thr | jnp.left_shift(jnp.int32(1), 30 - t)
        return jnp.where(count(lambda blk: blk >= cand) >= cap, cand, thr)

    thr = lax.fori_loop(0, 31, bit_step, jnp.zeros((1, LANES), jnp.int32))
    need = (cap - count(lambda blk: blk > thr)).astype(F32)

    row = lax.broadcasted_iota(jnp.int32, (ROUTE_BLOCK, ROUTE_BLOCK), 0)
    col = lax.broadcasted_iota(jnp.int32, (ROUTE_BLOCK, ROUTE_BLOCK), 1)
    before = jnp.where(col < row, 1.0, 0.0).astype(BF16)

    def assign(j, carry):
        c_eq, c_sel = carry
        r0 = pl.multiple_of(j * ROUTE_BLOCK, ROUTE_BLOCK)
        blk = bits_at(r0, ROUTE_BLOCK)
        gt, eq = blk > thr, blk == thr
        eq_f = jnp.where(eq, 1.0, 0.0)
        eq_rank = _dot(before, eq_f.astype(BF16)) + c_eq
        sel = gt | (eq & (eq_rank < need))
        sel_f = jnp.where(sel, 1.0, 0.0)
        rank = _dot(before, sel_f.astype(BF16)) + c_sel
        slot_ref[pl.ds(r0, ROUTE_BLOCK), :] = jnp.where(sel, rank, -1.0).astype(jnp.int32)
        start_ref[pl.ds(j, 1), :] = c_sel.astype(jnp.int32)
        return (c_eq + jnp.sum(eq_f, axis=0, keepdims=True), c_sel + jnp.sum(sel_f, axis=0, keepdims=True))

    zero = jnp.zeros((1, LANES), F32)
    lax.fori_loop(0, seq // ROUTE_BLOCK, assign, (zero, zero))


def routing(aff, *, batch, seq, cap):
    nblk = seq // ROUTE_BLOCK
    return pl.pallas_call(
        functools.partial(_routing_kernel, cap=cap),
        grid=(batch,),
        in_specs=[pl.BlockSpec((seq, LANES), lambda b: (b, 0))],
        out_specs=[pl.BlockSpec((seq, LANES), lambda b: (b, 0)), pl.BlockSpec((nblk, LANES), lambda b: (b, 0))],
        out_shape=[jax.ShapeDtypeStruct((batch * seq, LANES), jnp.int32),
                   jax.ShapeDtypeStruct((batch * nblk, LANES), jnp.int32)],
        compiler_params=pltpu.CompilerParams(
            dimension_semantics=("arbitrary",), vmem_limit_bytes=V7X_VMEM_LIMIT_BYTES),
    )(aff)


SLOT_ALIGN = 16
SLOT_WIN = ROUTE_BLOCK + SLOT_ALIGN
FFN_ROWS = 256


def _slot_hits(slot_row, base):
    r = lax.broadcasted_iota(jnp.int32, (SLOT_WIN, ROUTE_BLOCK), 0)
    return (slot_row - base) == r


def _experts_kernel(base_ref, slot_ref, gate_ref, h2_ref, wg_ref, wu_ref, wd_ref, y_ref, xin, gsl,
                    *, n_blocks, cap):
    pair = pl.program_id(0) * pl.num_programs(1) + pl.program_id(1)
    xin[...] = jnp.zeros_like(xin)
    gsl[...] = jnp.zeros_like(gsl)

    def gather(j, carry):
        base = pl.multiple_of(base_ref[pair * n_blocks + j], SLOT_ALIGN)
        tok = pl.ds(pl.multiple_of(j * ROUTE_BLOCK, ROUTE_BLOCK), ROUTE_BLOCK)
        hit = _slot_hits(slot_ref[0, 0, :, tok], base)
        win = pl.ds(base, SLOT_WIN)
        xin[win, :] += _dot(jnp.where(hit, 1.0, 0.0).astype(BF16), h2_ref[tok, :])
        gsl[win, :] += jnp.sum(jnp.where(hit, gate_ref[0, 0, :, tok], 0.0), axis=-1, keepdims=True)
        return carry

    lax.fori_loop(0, n_blocks, gather, 0)

    for rb in range(cap // FFN_ROWS):
        rows = slice(rb * FFN_ROWS, (rb + 1) * FFN_ROWS)
        xb = xin[rows, :].astype(BF16)
        gate_h = _dot(xb, wg_ref[0])
        hid = (gate_h * jax.nn.sigmoid(gate_h)) * _dot(xb, wu_ref[0])
        y = _dot(hid.astype(BF16), wd_ref[0]) * gsl[rows, :]
        y_ref[0, 0, rows, :] = y.astype(BF16)
    y_ref[0, 0, cap:, :] = jnp.zeros((y_ref.shape[2] - cap, y_ref.shape[3]), BF16)


def experts(base, slot_t, gate_t, h2, wg, wu, wd, *, batch, seq, cap):
    n_exp, d, f = wg.shape
    n_blocks = seq // ROUTE_BLOCK
    rows = cap + SLOT_WIN
    row_spec = pl.BlockSpec((1, 1, 1, seq), lambda b, e, base: (b, e, 0, 0))
    grid_spec = pltpu.PrefetchScalarGridSpec(
        num_scalar_prefetch=1,
        grid=(batch, n_exp),
        in_specs=[row_spec, row_spec,
                  pl.BlockSpec((seq, d), lambda b, e, base: (b, 0)),
                  pl.BlockSpec((1, d, f), lambda b, e, base: (e, 0, 0)),
                  pl.BlockSpec((1, d, f), lambda b, e, base: (e, 0, 0)),
                  pl.BlockSpec((1, f, d), lambda b, e, base: (e, 0, 0))],
        out_specs=pl.BlockSpec((1, 1, rows, d), lambda b, e, base: (b, e, 0, 0)),
        scratch_shapes=[pltpu.VMEM((rows, d), F32), pltpu.VMEM((rows, 1), F32)],
    )
    return pl.pallas_call(
        functools.partial(_experts_kernel, n_blocks=n_blocks, cap=cap),
        grid_spec=grid_spec,
        out_shape=jax.ShapeDtypeStruct((batch, n_exp, rows, d), BF16),
        compiler_params=pltpu.CompilerParams(
            dimension_semantics=("arbitrary", "arbitrary"), vmem_limit_bytes=V7X_VMEM_LIMIT_BYTES),
    )(base, slot_t, gate_t, h2, wg, wu, wd)


def _combine_kernel(base_ref, slot_ref, y_ref, x1_ref, out_ref, *, n_blocks):
    n_exp = pl.num_programs(2)
    e = pl.program_id(2)
    pair = pl.program_id(0) * n_exp + e

    @pl.when(e == 0)
    def _():
        out_ref[...] = x1_ref[...]

    def scatter(j, carry):
        base = pl.multiple_of(base_ref[pair * n_blocks + j], SLOT_ALIGN)
        tok = pl.ds(pl.multiple_of(j * ROUTE_BLOCK, ROUTE_BLOCK), ROUTE_BLOCK)
        hit = _slot_hits(slot_ref[0, 0, :, tok], base)
        y_win = y_ref[0, 0, pl.ds(base, SLOT_WIN), :]
        out_ref[tok, :] += _dot(jnp.where(hit, 1.0, 0.0).astype(BF16), y_win, ((0,), (0,)))
        return carry

    lax.fori_loop(0, n_blocks, scatter, 0)


def combine(base, slot_t, y, x1, *, batch, seq, tn=256):
    n_exp, rows, d = y.shape[1], y.shape[2], y.shape[3]
    n_blocks = seq // ROUTE_BLOCK
    grid_spec = pltpu.PrefetchScalarGridSpec(
        num_scalar_prefetch=1,
        grid=(batch, d // tn, n_exp),
        in_specs=[pl.BlockSpec((1, 1, 1, seq), lambda b, n, e, base: (b, e, 0, 0)),
                  pl.BlockSpec((1, 1, rows, tn), lambda b, n, e, base: (b, e, 0, n)),
                  pl.BlockSpec((seq, tn), lambda b, n, e, base: (b, n))],
        out_specs=pl.BlockSpec((seq, tn), lambda b, n, e, base: (b, n)),
    )
    return pl.pallas_call(
        functools.partial(_combine_kernel, n_blocks=n_blocks),
        grid_spec=grid_spec,
        out_shape=jax.ShapeDtypeStruct((batch * seq, d), F32),
        compiler_params=pltpu.CompilerParams(
            dimension_semantics=("arbitrary", "arbitrary", "arbitrary"),
            vmem_limit_bytes=V7X_VMEM_LIMIT_BYTES),
    )(base, slot_t, y, x1)


def kernel(x, norm1_w, w_in, attn_q_norm_w, attn_k_norm_w, hgrn_lb_fwd, hgrn_lb_bwd, hgrn_out_norm_w,
           w_out, norm2_w, w_router, w_expert_gate, w_expert_up, w_expert_down):
    batch, seq, d_model = x.shape
    depth = w_in.shape[0]
    d_in = w_in.shape[2]
    cap = max(1, CAPACITY_FACTOR * seq // N_EXPERTS)
    hgrn_col = 3 * ATTN_WIDTH
    lb_f_all = jnp.cumsum(jax.nn.softmax(hgrn_lb_fwd.astype(F32), axis=0), axis=0)
    lb_b_all = jnp.cumsum(jax.nn.softmax(hgrn_lb_bwd.astype(F32), axis=0), axis=0)

    x2d = x.reshape(batch * seq, d_model)
    for l in range(depth):
        proj = in_proj(x2d, norm1_w[l], w_in[l].astype(BF16))
        o_f, o_b = hgrn(proj, lb_f_all[l], lb_b_all[l], batch=batch, seq=seq, col0=hgrn_col)
        attn_outs = [attn_pattern(proj, attn_q_norm_w[l], attn_k_norm_w[l], batch=batch, seq=seq,
                                  d_in=d_in, dilation=dil) for _, dil in DILATED_PATTERNS]
        x1, h2, aff = out_proj(x2d, attn_outs, o_f, o_b, proj, hgrn_out_norm_w[l], w_out[l].astype(BF16),
                               norm2_w[l], w_router[l], hg_col=hgrn_col + 4 * HGRN_WIDTH)
        slot, start = routing(aff, batch=batch, seq=seq, cap=cap)

        def expert_major(t):
            return t.reshape(batch, -1, LANES)[:, :, :N_EXPERTS].transpose(0, 2, 1)

        slot_t = expert_major(slot).reshape(batch, N_EXPERTS, 1, seq)
        gate_t = expert_major(aff).reshape(batch, N_EXPERTS, 1, seq)
        base = (expert_major(start) // SLOT_ALIGN * SLOT_ALIGN).reshape(-1)
        y = experts(base, slot_t, gate_t, h2, w_expert_gate[l].astype(BF16), w_expert_up[l].astype(BF16),
                    w_expert_down[l].astype(BF16), batch=batch, seq=seq, cap=cap)
        x2d = combine(base, slot_t, y, x1, batch=batch, seq=seq)
    return x2d.reshape(batch, seq, d_model)
```

```python
import functools

import jax
import jax.numpy as jnp
from jax import lax
from jax.experimental import pallas as pl
from jax.experimental.pallas import tpu as pltpu

F32 = jnp.float32
BF16 = jnp.bfloat16

NORM_EPS = 1e-6
NEG_BIG = -1e30
ATTN_HEAD_DIM = 64
ATTN_HEADS = 8
ATTN_WIDTH = ATTN_HEADS * ATTN_HEAD_DIM
DILATED_PATTERNS = ((128, 1), (512, 4), (2048, 16))
ALIBI_MAX_BIAS = 8.0
HGRN_DIM = 128
HGRN_HEADS = 4
HGRN_WIDTH = HGRN_HEADS * HGRN_DIM
N_EXPERTS = 16
CAPACITY_FACTOR = 2
V7X_VMEM_LIMIT_BYTES = 56 * 1024 * 1024


def _dot(a, b, dims=((1,), (0,))):
    return lax.dot_general(a, b, (dims, ((), ())), preferred_element_type=F32)


def _dot_nt(a, b):
    return _dot(a, b, ((1,), (1,)))


def _in_proj_kernel(x_ref, nw_ref, w_ref, o_ref, *, n_chunk):
    x = x_ref[...]
    h = x * lax.rsqrt(jnp.mean(x * x, axis=-1, keepdims=True) + NORM_EPS) * nw_ref[...]
    h = h.astype(BF16)
    n_total = o_ref.shape[1]
    for c in range(n_total // n_chunk):
        cols = slice(c * n_chunk, (c + 1) * n_chunk)
        o_ref[:, cols] = _dot(h, w_ref[:, cols])


def in_proj(x2d, norm_w, w_bf16, *, tm=256, n_chunk=512):
    m, d = x2d.shape
    n = w_bf16.shape[1]
    return pl.pallas_call(
        functools.partial(_in_proj_kernel, n_chunk=n_chunk),
        grid=(m // tm,),
        in_specs=[
            pl.BlockSpec((tm, d), lambda i: (i, 0)),
            pl.BlockSpec((1, d), lambda i: (0, 0)),
            pl.BlockSpec((d, n), lambda i: (0, 0)),
        ],
        out_specs=pl.BlockSpec((tm, n), lambda i: (i, 0)),
        out_shape=jax.ShapeDtypeStruct((m, n), F32),
        compiler_params=pltpu.CompilerParams(
            dimension_semantics=("arbitrary",), vmem_limit_bytes=V7X_VMEM_LIMIT_BYTES),
    )(x2d, norm_w.reshape(1, d), w_bf16)


HGRN_CHUNK = 64
HGRN_SUB = 16


def _split3(g):
    hi = g.astype(BF16)
    r1 = g - hi.astype(F32)
    mid = r1.astype(BF16)
    lo = (r1 - mid.astype(F32)).astype(BF16)
    return hi, mid, lo


def _hgrn_chunk(q_raw, z, v, lb, st_ref, *, reverse):
    c, dk = q_raw.shape
    sub = HGRN_SUB
    q = q_raw * jax.nn.sigmoid(q_raw)
    f = lb + (1.0 - lb) * jax.nn.sigmoid(z)
    k = (1.0 - lb) * jax.nn.sigmoid(-z)
    g = jnp.log(f)

    row = lax.broadcasted_iota(jnp.int32, (c, c), 0)
    col = lax.broadcasted_iota(jnp.int32, (c, c), 1)
    tri = (col >= row) if reverse else (col <= row)
    tri_bf = jnp.where(tri, 1.0, 0.0).astype(BF16)
    g_hi, g_mid, g_lo = _split3(g)
    cum = _dot(tri_bf, g_hi) + _dot(tri_bf, g_mid) + _dot(tri_bf, g_lo)

    def edge(r):
        return cum[r:r + 1, :]

    if reverse:
        ref_half, ref_q_lo, ref_q_hi, ref_end = edge(32), edge(16), edge(48), edge(0)
    else:
        ref_half, ref_q_lo, ref_q_hi, ref_end = edge(31), edge(15), edge(47), edge(c - 1)

    r1 = lax.broadcasted_iota(jnp.int32, (c, 1), 0)
    ref_quarter = jnp.where(r1 < 32, ref_q_lo, ref_q_hi)

    st = st_ref[...]
    qe = (q * jnp.exp(cum)).astype(BF16)
    o = _dot_nt(qe, st.astype(BF16))
    ke = (k * jnp.exp(ref_end - cum)).astype(BF16)
    v_bf = v.astype(BF16)
    st_ref[...] = st * jnp.exp(ref_end) + _dot(v_bf, ke, ((0,), (0,)))

    q1 = (q * jnp.exp(jnp.minimum(cum - ref_half, 0.0))).astype(BF16)
    k1 = (k * jnp.exp(jnp.minimum(ref_half - cum, 0.0))).astype(BF16)
    q2 = (q * jnp.exp(jnp.minimum(cum - ref_quarter, 0.0))).astype(BF16)
    k2 = (k * jnp.exp(jnp.minimum(ref_quarter - cum, 0.0))).astype(BF16)
    tb, sb = row // sub, col // sub
    if reverse:
        m1 = (tb < 2) & (sb >= 2)
        m2 = ((tb == 0) & (sb == 1)) | ((tb == 2) & (sb == 3))
    else:
        m1 = (tb >= 2) & (sb < 2)
        m2 = ((tb == 1) & (sb == 0)) | ((tb == 3) & (sb == 2))
    a = jnp.where(m1, _dot_nt(q1, k1), 0.0) + jnp.where(m2, _dot_nt(q2, k2), 0.0)

    t_loc = lax.broadcasted_iota(jnp.int32, (sub, 1), 0)
    lane = lax.broadcasted_iota(jnp.int32, (sub, c), 1)
    blocks = []
    for blk in range(c // sub):
        rows = slice(blk * sub, (blk + 1) * sub)
        cum_b, q_b = cum[rows], q[rows]
        a_b = jnp.zeros((sub, c), F32)
        for s_loc in range(sub):
            s = blk * sub + s_loc
            keep = (t_loc <= s_loc) if reverse else (t_loc >= s_loc)
            e = jnp.exp(jnp.where(keep, cum_b - cum[s:s + 1, :], NEG_BIG))
            p = (q_b * k[s:s + 1, :]) * e
            a_b = jnp.where(lane == s, jnp.sum(p, axis=-1, keepdims=True), a_b)
        blocks.append(a_b)
    a = a + jnp.concatenate(blocks, axis=0)
    return o + _dot(a.astype(BF16), v_bf)


def _hgrn_kernel(qf_ref, zf_ref, vf_ref, qb_ref, zb_ref, vb_ref, lbf_ref, lbb_ref,
                 of_ref, ob_ref, sf_ref, sb_ref):
    @pl.when(pl.program_id(2) == 0)
    def _():
        sf_ref[...] = jnp.zeros_like(sf_ref)
        sb_ref[...] = jnp.zeros_like(sb_ref)

    n_chunks = qf_ref.shape[0] // HGRN_CHUNK
    lbf, lbb = lbf_ref[...], lbb_ref[...]

    def body(ci, carry):
        rf = pl.ds(pl.multiple_of(ci * HGRN_CHUNK, HGRN_CHUNK), HGRN_CHUNK)
        of_ref[rf, :] = _hgrn_chunk(qf_ref[rf, :], zf_ref[rf, :], vf_ref[rf, :], lbf, sf_ref, reverse=False)
        rb = pl.ds(pl.multiple_of((n_chunks - 1 - ci) * HGRN_CHUNK, HGRN_CHUNK), HGRN_CHUNK)
        ob_ref[rb, :] = _hgrn_chunk(qb_ref[rb, :], zb_ref[rb, :], vb_ref[rb, :], lbb, sb_ref, reverse=True)
        return carry

    lax.fori_loop(0, n_chunks, body, 0)


def hgrn(proj, lb_f, lb_b, *, batch, seq, col0, tile=512):
    nt = seq // tile
    hb = HGRN_HEADS
    cb = col0 // HGRN_DIM

    def fwd(colblock):
        return pl.BlockSpec((tile, HGRN_DIM), lambda b, h, i: (b * nt + i, cb + colblock * hb + h))

    def bwd(colblock):
        return pl.BlockSpec((tile, HGRN_DIM), lambda b, h, i: (b * nt + nt - 1 - i, cb + colblock * hb + h))

    lb_spec = pl.BlockSpec((1, HGRN_DIM), lambda b, h, i: (0, h))
    out_shape = jax.ShapeDtypeStruct((batch * seq, HGRN_WIDTH), F32)
    return pl.pallas_call(
        _hgrn_kernel,
        grid=(batch, hb, nt),
        in_specs=[fwd(0), fwd(1), fwd(3), bwd(0), bwd(2), bwd(3), lb_spec, lb_spec],
        out_specs=[
            pl.BlockSpec((tile, HGRN_DIM), lambda b, h, i: (b * nt + i, h)),
            pl.BlockSpec((tile, HGRN_DIM), lambda b, h, i: (b * nt + nt - 1 - i, h)),
        ],
        out_shape=[out_shape, out_shape],
        scratch_shapes=[pltpu.VMEM((HGRN_DIM, HGRN_DIM), F32), pltpu.VMEM((HGRN_DIM, HGRN_DIM), F32)],
        compiler_params=pltpu.CompilerParams(
            dimension_semantics=("arbitrary", "arbitrary", "arbitrary"),
            vmem_limit_bytes=V7X_VMEM_LIMIT_BYTES),
    )(proj, proj, proj, proj, proj, proj, lb_f.reshape(1, -1), lb_b.reshape(1, -1))


ATTN_HALF = 64
ATTN_QT = 128
ATTN_KT = ATTN_QT + 2 * ATTN_HALF
LANES = 128


def _head_rms(x, w):
    lo = lax.broadcasted_iota(jnp.int32, (1, LANES), 1) < ATTN_HEAD_DIM
    outs = []
    for sl in range(ATTN_WIDTH // LANES):
        xs = x[:, sl * LANES:(sl + 1) * LANES]
        sq = xs * xs
        s_lo = jnp.sum(jnp.where(lo, sq, 0.0), axis=-1, keepdims=True)
        s_hi = jnp.sum(jnp.where(lo, 0.0, sq), axis=-1, keepdims=True)
        ms = jnp.where(lo, s_lo, s_hi) * (1.0 / ATTN_HEAD_DIM)
        outs.append(xs * lax.rsqrt(ms + NORM_EPS))
    return jnp.concatenate(outs, axis=-1) * w


def _attn_kernel(q_ref, k_ref, kp_ref, kn_ref, v_ref, vp_ref, vn_ref, qw_ref, kw_ref, bias_ref,
                 o_ref, lse_ref, kwin, vwin, qn, *, sub_len):
    tile = q_ref.shape[0]
    i = pl.program_id(2)
    qw, kw = qw_ref[...], kw_ref[...]
    kwin[0:ATTN_HALF, :] = _head_rms(kp_ref[...], kw).astype(BF16)
    kwin[ATTN_HALF:ATTN_HALF + tile, :] = _head_rms(k_ref[...], kw).astype(BF16)
    kwin[ATTN_HALF + tile:, :] = _head_rms(kn_ref[...], kw).astype(BF16)
    vwin[0:ATTN_HALF, :] = vp_ref[...].astype(BF16)
    vwin[ATTN_HALF:ATTN_HALF + tile, :] = v_ref[...].astype(BF16)
    vwin[ATTN_HALF + tile:, :] = vn_ref[...].astype(BF16)
    qn[...] = (_head_rms(q_ref[...], qw) * (ATTN_HEAD_DIM ** -0.5)).astype(BF16)

    lane = lax.broadcasted_iota(jnp.int32, (1, LANES), 1)
    lo = lane < ATTN_HEAD_DIM
    kcol = lax.broadcasted_iota(jnp.int32, (1, ATTN_KT), 1)

    def sub(j, carry):
        r0 = pl.multiple_of(j * ATTN_QT, ATTN_QT)
        lk = i * tile + j * ATTN_QT - ATTN_HALF + kcol
        edge = jnp.where((lk >= 0) & (lk < sub_len), 0.0, NEG_BIG)
        lse_tile = jnp.zeros((ATTN_QT, LANES), F32)
        for sl in range(ATTN_WIDTH // LANES):
            cols = slice(sl * LANES, (sl + 1) * LANES)
            qs = qn[pl.ds(r0, ATTN_QT), cols]
            ks = kwin[pl.ds(r0, ATTN_KT), cols]
            vs = vwin[pl.ds(r0, ATTN_KT), cols]
            o_slab = None
            for hh in range(2):
                h = 2 * sl + hh
                qh = jnp.where(lo if hh == 0 else jnp.logical_not(lo), qs, jnp.zeros_like(qs))
                s = _dot_nt(qh, ks) + bias_ref[h] + edge
                m = jnp.max(s, axis=-1, keepdims=True)
                p = jnp.exp(s - m)
                den = jnp.sum(p, axis=-1, keepdims=True)
                oh = _dot(p.astype(BF16), vs) / den
                o_slab = oh if hh == 0 else jnp.where(lo, o_slab, oh)
                lse_tile = jnp.where(lane == h, m + jnp.log(den), lse_tile)
            o_ref[pl.ds(r0, ATTN_QT), cols] = o_slab
        lse_ref[pl.ds(r0, ATTN_QT), :] = lse_tile
        return carry

    lax.fori_loop(0, tile // ATTN_QT, sub, 0)


def _attn_bias(dilation):
    slopes = jnp.exp2(-ALIBI_MAX_BIAS * jnp.arange(1, ATTN_HEADS + 1, dtype=F32) / ATTN_HEADS)
    t = jnp.arange(ATTN_QT)[:, None]
    j = jnp.arange(ATTN_KT)[None, :]
    dist = jnp.abs(j - ATTN_HALF - t)
    alibi = -slopes[:, None, None] * (dilation * dist).astype(F32)[None]
    return jnp.where((dist <= ATTN_HALF)[None], alibi, NEG_BIG)


def attn_pattern(proj, qw, kw, *, batch, seq, d_in, dilation, tile=512):
    sub_len = seq // dilation
    tile = min(tile, sub_len)
    nt = sub_len // tile
    cpp = d_in // ATTN_WIDTH
    view = proj.reshape(batch * sub_len, dilation * d_in)
    hb = tile // ATTN_HALF
    nhb = sub_len // ATTN_HALF

    def main(c):
        return pl.BlockSpec((tile, ATTN_WIDTH), lambda b, r, i: (b * nt + i, r * cpp + c))

    def prev(c):
        return pl.BlockSpec((ATTN_HALF, ATTN_WIDTH),
                            lambda b, r, i: (b * nhb + jnp.maximum(i * hb - 1, 0), r * cpp + c))

    def nxt(c):
        return pl.BlockSpec((ATTN_HALF, ATTN_WIDTH),
                            lambda b, r, i: (b * nhb + jnp.minimum((i + 1) * hb, nhb - 1), r * cpp + c))

    wspec = pl.BlockSpec((1, ATTN_WIDTH), lambda b, r, i: (0, 0))
    o, lse = pl.pallas_call(
        functools.partial(_attn_kernel, sub_len=sub_len),
        grid=(batch, dilation, nt),
        in_specs=[main(0), main(1), prev(1), nxt(1), main(2), prev(2), nxt(2), wspec, wspec,
                  pl.BlockSpec((ATTN_HEADS, ATTN_QT, ATTN_KT), lambda b, r, i: (0, 0, 0))],
        out_specs=[pl.BlockSpec((tile, ATTN_WIDTH), lambda b, r, i: (b * nt + i, r)),
                   pl.BlockSpec((tile, LANES), lambda b, r, i: (b * nt + i, r))],
        out_shape=[jax.ShapeDtypeStruct((batch * sub_len, dilation * ATTN_WIDTH), F32),
                   jax.ShapeDtypeStruct((batch * sub_len, dilation * LANES), F32)],
        scratch_shapes=[pltpu.VMEM((tile + 2 * ATTN_HALF, ATTN_WIDTH), BF16),
                        pltpu.VMEM((tile + 2 * ATTN_HALF, ATTN_WIDTH), BF16),
                        pltpu.VMEM((tile, ATTN_WIDTH), BF16)],
        compiler_params=pltpu.CompilerParams(
            dimension_semantics=("arbitrary", "arbitrary", "arbitrary"),
            vmem_limit_bytes=V7X_VMEM_LIMIT_BYTES),
    )(view, view, view, view, view, view, view,
      jnp.tile(qw, ATTN_HEADS).reshape(1, -1), jnp.tile(kw, ATTN_HEADS).reshape(1, -1), _attn_bias(dilation))
    return o.reshape(batch * seq, ATTN_WIDTH), lse.reshape(batch * seq, LANES)


def _split2(x):
    hi = x.astype(BF16)
    return hi, (x - hi.astype(F32)).astype(BF16)


def _out_proj_kernel(x_ref, o1_ref, o2_ref, o3_ref, l1_ref, l2_ref, l3_ref, of_ref, ob_ref, hg_ref,
                     hw_ref, expand_ref, wo_ref, n2_ref, wr_hi_ref, wr_lo_ref,
                     x1_ref, h2_ref, aff_ref):
    lses = [l1_ref[...], l2_ref[...], l3_ref[...]]
    mx = jnp.maximum(jnp.maximum(lses[0], lses[1]), lses[2])
    ws = [jnp.exp(l - mx) for l in lses]
    inv = 1.0 / (ws[0] + ws[1] + ws[2])
    expand = expand_ref[...]
    a_out = None
    for w, o_ref in zip(ws, (o1_ref, o2_ref, o3_ref)):
        w_hi, w_lo = _split2(w * inv)
        term = (_dot(w_hi, expand) + _dot(w_lo, expand)) * o_ref[...]
        a_out = term if a_out is None else a_out + term

    o = of_ref[...] + ob_ref[...]
    hg = hg_ref[...]
    hw = hw_ref[...]
    b_parts = []
    for sl in range(HGRN_HEADS):
        cols = slice(sl * HGRN_DIM, (sl + 1) * HGRN_DIM)
        os_ = o[:, cols]
        y = os_ * lax.rsqrt(jnp.mean(os_ * os_, axis=-1, keepdims=True) + NORM_EPS) * hw
        g = hg[:, cols]
        b_parts.append(y * (g * jax.nn.sigmoid(g)))
    mixed = jnp.concatenate([a_out] + b_parts, axis=-1).astype(BF16)

    x1 = x_ref[...] + _dot(mixed, wo_ref[...])
    x1_ref[...] = x1
    h2 = x1 * lax.rsqrt(jnp.mean(x1 * x1, axis=-1, keepdims=True) + NORM_EPS) * n2_ref[...]
    d = x1.shape[1]
    h_hi, h_lo = _split2(h2)
    h2_ref[:, 0:d] = h_hi

    wr_hi = wr_hi_ref[...]
    logits = _dot(h_hi, wr_hi) + _dot(h_lo, wr_hi) + _dot(h_hi, wr_lo_ref[...])
    valid = lax.broadcasted_iota(jnp.int32, (1, LANES), 1) < N_EXPERTS
    logits = jnp.where(valid, logits, NEG_BIG)
    ex = jnp.exp(logits - jnp.max(logits, axis=-1, keepdims=True))
    aff = ex / jnp.sum(ex, axis=-1, keepdims=True)
    aff_ref[...] = aff
    a_hi, a_lo = _split2(aff)
    h2_ref[:, d:d + LANES] = a_hi
    h2_ref[:, d + LANES:d + 2 * LANES] = a_lo


def out_proj(x2d, attn_outs, o_f, o_b, proj, hgrn_norm_w, w_out_bf16, norm2_w, w_router, *, hg_col, tm=256):
    m, d = x2d.shape
    (o1, l1), (o2, l2), (o3, l3) = attn_outs
    head = jnp.arange(ATTN_WIDTH) // ATTN_HEAD_DIM
    expand = (jnp.arange(LANES)[:, None] == head[None, :]).astype(BF16)
    wr = jnp.pad(w_router, ((0, 0), (0, LANES - N_EXPERTS)))
    wr_hi, wr_lo = _split2(wr)

    def rows(width, colblock=0):
        return pl.BlockSpec((tm, width), lambda i: (i, colblock))

    def const(shape):
        return pl.BlockSpec(shape, lambda i: (0, 0))

    return pl.pallas_call(
        _out_proj_kernel,
        grid=(m // tm,),
        in_specs=[rows(d), rows(ATTN_WIDTH), rows(ATTN_WIDTH), rows(ATTN_WIDTH),
                  rows(LANES), rows(LANES), rows(LANES),
                  rows(HGRN_WIDTH), rows(HGRN_WIDTH), rows(HGRN_WIDTH, hg_col // HGRN_WIDTH),
                  const((1, HGRN_DIM)), const((LANES, ATTN_WIDTH)), const((ATTN_WIDTH + HGRN_WIDTH, d)),
                  const((1, d)), const((d, LANES)), const((d, LANES))],
        out_specs=[rows(d), rows(d + 2 * LANES), rows(LANES)],
        out_shape=[jax.ShapeDtypeStruct((m, d), F32), jax.ShapeDtypeStruct((m, d + 2 * LANES), BF16),
                   jax.ShapeDtypeStruct((m, LANES), F32)],
        compiler_params=pltpu.CompilerParams(
            dimension_semantics=("arbitrary",), vmem_limit_bytes=V7X_VMEM_LIMIT_BYTES),
    )(x2d, o1, o2, o3, l1, l2, l3, o_f, o_b, proj, hgrn_norm_w.reshape(1, -1), expand, w_out_bf16,
      norm2_w.reshape(1, -1), wr_hi, wr_lo)


ROUTE_BLOCK = 128
COUNT_ROWS = 512


def _routing_kernel(aff_ref, slot_ref, start_ref, *, cap):
    seq = aff_ref.shape[0]

    def bits_at(r0, n):
        return pltpu.bitcast(aff_ref[pl.ds(r0, n), :], jnp.int32)

    def count(pred):
        def body(c, acc):
            blk = bits_at(pl.multiple_of(c * COUNT_ROWS, COUNT_ROWS), COUNT_ROWS)
            return acc + jnp.sum(jnp.where(pred(blk), 1, 0), axis=0, keepdims=True)
        return lax.fori_loop(0, seq // COUNT_ROWS, body, jnp.zeros((1, LANES), jnp.int32))

    def bit_step(t, thr):
        cand = thr | jnp.left_shift(jnp.int32(1), 30 - t)
        return jnp.where(count(lambda blk: blk >= cand) >= cap, cand, thr)

    thr = lax.fori_loop(0, 31, bit_step, jnp.zeros((1, LANES), jnp.int32))
    need = (cap - count(lambda blk: blk > thr)).astype(F32)

    row = lax.broadcasted_iota(jnp.int32, (ROUTE_BLOCK, ROUTE_BLOCK), 0)
    col = lax.broadcasted_iota(jnp.int32, (ROUTE_BLOCK, ROUTE_BLOCK), 1)
    before = jnp.where(col < row, 1.0, 0.0).astype(BF16)

    def assign(j, carry):
        c_eq, c_sel = carry
        r0 = pl.multiple_of(j * ROUTE_BLOCK, ROUTE_BLOCK)
        blk = bits_at(r0, ROUTE_BLOCK)
        gt, eq = blk > thr, blk == thr
        eq_f = jnp.where(eq, 1.0, 0.0)
        eq_rank = _dot(before, eq_f.astype(BF16)) + c_eq
        sel = gt | (eq & (eq_rank < need))
        sel_f = jnp.where(sel, 1.0, 0.0)
        rank = _dot(before, sel_f.astype(BF16)) + c_sel
        slot_ref[pl.ds(r0, ROUTE_BLOCK), :] = jnp.where(sel, rank, -1.0).astype(jnp.int32)
        start_ref[pl.ds(j, 1), :] = c_sel.astype(jnp.int32)
        return (c_eq + jnp.sum(eq_f, axis=0, keepdims=True), c_sel + jnp.sum(sel_f, axis=0, keepdims=True))

    zero = jnp.zeros((1, LANES), F32)
    lax.fori_loop(0, seq // ROUTE_BLOCK, assign, (zero, zero))


def routing(aff, *, batch, seq, cap):
    nblk = seq // ROUTE_BLOCK
    return pl.pallas_call(
        functools.partial(_routing_kernel, cap=cap),
        grid=(batch,),
        in_specs=[pl.BlockSpec((seq, LANES), lambda b: (b, 0))],
        out_specs=[pl.BlockSpec((seq, LANES), lambda b: (b, 0)), pl.BlockSpec((nblk, LANES), lambda b: (b, 0))],
        out_shape=[jax.ShapeDtypeStruct((batch * seq, LANES), jnp.int32),
                   jax.ShapeDtypeStruct((batch * nblk, LANES), jnp.int32)],
        compiler_params=pltpu.CompilerParams(
            dimension_semantics=("arbitrary",), vmem_limit_bytes=V7X_VMEM_LIMIT_BYTES),
    )(aff)


SLOT_ALIGN = 16
SLOT_WIN = ROUTE_BLOCK + SLOT_ALIGN
FFN_ROWS = 256


def _slot_hits(slot_row, base):
    r = lax.broadcasted_iota(jnp.int32, (SLOT_WIN, ROUTE_BLOCK), 0)
    return (slot_row - base) == r


def _one_hot(hit):
    return jnp.where(hit, 1.0, 0.0).astype(BF16)


def _gather_kernel(base_ref, slot_ref, h_ref, xin_ref, *, n_blocks):
    n_exp = slot_ref.shape[1]
    pair0 = pl.program_id(0) * n_exp
    xin_ref[...] = jnp.zeros_like(xin_ref)

    def block(j, carry):
        tok = pl.ds(pl.multiple_of(j * ROUTE_BLOCK, ROUTE_BLOCK), ROUTE_BLOCK)
        h_blk = h_ref[tok, :]
        for e in range(n_exp):
            base = pl.multiple_of(base_ref[(pair0 + e) * n_blocks + j], SLOT_ALIGN)
            hit = _slot_hits(slot_ref[0, e, :, tok], base)
            win = pl.ds(base, SLOT_WIN)
            xin_ref[0, e, win, :] += _dot(_one_hot(hit), h_blk).astype(BF16)
        return carry

    lax.fori_loop(0, n_blocks, block, 0)


def gather(base, slot_t, h_ext, *, batch, seq, cap, tn=256):
    n_exp = slot_t.shape[1]
    width = h_ext.shape[1]
    rows = cap + SLOT_WIN
    grid_spec = pltpu.PrefetchScalarGridSpec(
        num_scalar_prefetch=1,
        grid=(batch, width // tn),
        in_specs=[pl.BlockSpec((1, n_exp, 1, seq), lambda b, n, base: (b, 0, 0, 0)),
                  pl.BlockSpec((seq, tn), lambda b, n, base: (b, n))],
        out_specs=pl.BlockSpec((1, n_exp, rows, tn), lambda b, n, base: (b, 0, 0, n)),
    )
    return pl.pallas_call(
        functools.partial(_gather_kernel, n_blocks=seq // ROUTE_BLOCK),
        grid_spec=grid_spec,
        out_shape=jax.ShapeDtypeStruct((batch, n_exp, rows, width), BF16),
        compiler_params=pltpu.CompilerParams(
            dimension_semantics=("arbitrary", "arbitrary"), vmem_limit_bytes=V7X_VMEM_LIMIT_BYTES),
    )(base, slot_t, h_ext)


def _ffn_kernel(xin_ref, g_ref, wg_ref, wu_ref, wd_ref, y_ref, *, cap):
    e = pl.program_id(0)
    lane = lax.broadcasted_iota(jnp.int32, (1, 2 * LANES), 1)
    mine = (lane == e) | (lane == LANES + e)
    for rb in range(cap // FFN_ROWS):
        rows = slice(rb * FFN_ROWS, (rb + 1) * FFN_ROWS)
        xb = xin_ref[0, 0, rows, :]
        gate = jnp.sum(jnp.where(mine, g_ref[0, 0, rows, :].astype(F32), 0.0), axis=-1, keepdims=True)
        gate_h = _dot(xb, wg_ref[0])
        hid = (gate_h * jax.nn.sigmoid(gate_h)) * _dot(xb, wu_ref[0])
        y_ref[0, 0, rows, :] = (_dot(hid.astype(BF16), wd_ref[0]) * gate).astype(BF16)
    y_ref[0, 0, cap:, :] = jnp.zeros((y_ref.shape[2] - cap, y_ref.shape[3]), BF16)


def expert_ffn(xin, wg, wu, wd, *, cap):
    batch, n_exp, rows, width = xin.shape
    _, d, f = wg.shape
    return pl.pallas_call(
        functools.partial(_ffn_kernel, cap=cap),
        grid=(n_exp, batch),
        in_specs=[pl.BlockSpec((1, 1, cap, d), lambda e, b: (b, e, 0, 0)),
                  pl.BlockSpec((1, 1, cap, 2 * LANES), lambda e, b: (b, e, 0, d // (2 * LANES))),
                  pl.BlockSpec((1, d, f), lambda e, b: (e, 0, 0)),
                  pl.BlockSpec((1, d, f), lambda e, b: (e, 0, 0)),
                  pl.BlockSpec((1, f, d), lambda e, b: (e, 0, 0))],
        out_specs=pl.BlockSpec((1, 1, rows, d), lambda e, b: (b, e, 0, 0)),
        out_shape=jax.ShapeDtypeStruct((batch, n_exp, rows, d), BF16),
        compiler_params=pltpu.CompilerParams(
            dimension_semantics=("arbitrary", "arbitrary"), vmem_limit_bytes=V7X_VMEM_LIMIT_BYTES),
    )(xin, xin, wg, wu, wd)


def _combine_kernel(base_ref, slot_ref, y_ref, x1_ref, out_ref, *, n_blocks):
    n_exp = slot_ref.shape[1]
    pair0 = pl.program_id(0) * n_exp
    blocks_here = out_ref.shape[0] // ROUTE_BLOCK
    first = pl.program_id(2) * blocks_here

    def block(jj, carry):
        j = first + jj
        rows = pl.ds(pl.multiple_of(jj * ROUTE_BLOCK, ROUTE_BLOCK), ROUTE_BLOCK)
        tok = pl.ds(pl.multiple_of(j * ROUTE_BLOCK, ROUTE_BLOCK), ROUTE_BLOCK)
        acc = x1_ref[rows, :]
        for e in range(n_exp):
            base = pl.multiple_of(base_ref[(pair0 + e) * n_blocks + j], SLOT_ALIGN)
            hit = _slot_hits(slot_ref[0, e, :, tok], base)
            y_win = y_ref[0, e, pl.ds(base, SLOT_WIN), :]
            acc = acc + _dot(_one_hot(hit), y_win, ((0,), (0,)))
        out_ref[rows, :] = acc
        return carry

    lax.fori_loop(0, blocks_here, block, 0)


def combine(base, slot_t, y, x1, *, batch, seq, tn=512, tt=2048):
    n_exp, rows, d = y.shape[1], y.shape[2], y.shape[3]
    grid_spec = pltpu.PrefetchScalarGridSpec(
        num_scalar_prefetch=1,
        grid=(batch, d // tn, seq // tt),
        in_specs=[pl.BlockSpec((1, n_exp, 1, seq), lambda b, n, t, base: (b, 0, 0, 0)),
                  pl.BlockSpec((1, n_exp, rows, tn), lambda b, n, t, base: (b, 0, 0, n),
                               pipeline_mode=pl.Buffered(1)),
                  pl.BlockSpec((tt, tn), lambda b, n, t, base: (b * (seq // tt) + t, n))],
        out_specs=pl.BlockSpec((tt, tn), lambda b, n, t, base: (b * (seq // tt) + t, n)),
    )
    return pl.pallas_call(
        functools.partial(_combine_kernel, n_blocks=seq // ROUTE_BLOCK),
        grid_spec=grid_spec,
        out_shape=jax.ShapeDtypeStruct((batch * seq, d), F32),
        compiler_params=pltpu.CompilerParams(
            dimension_semantics=("arbitrary", "arbitrary", "arbitrary"),
            vmem_limit_bytes=V7X_VMEM_LIMIT_BYTES),
    )(base, slot_t, y, x1)


def kernel(x, norm1_w, w_in, attn_q_norm_w, attn_k_norm_w, hgrn_lb_fwd, hgrn_lb_bwd, hgrn_out_norm_w,
           w_out, norm2_w, w_router, w_expert_gate, w_expert_up, w_expert_down):
    batch, seq, d_model = x.shape
    depth = w_in.shape[0]
    d_in = w_in.shape[2]
    cap = max(1, CAPACITY_FACTOR * seq // N_EXPERTS)
    hgrn_col = 3 * ATTN_WIDTH
    lb_f_all = jnp.cumsum(jax.nn.softmax(hgrn_lb_fwd.astype(F32), axis=0), axis=0)
    lb_b_all = jnp.cumsum(jax.nn.softmax(hgrn_lb_bwd.astype(F32), axis=0), axis=0)

    x2d = x.reshape(batch * seq, d_model)
    for l in range(depth):
        proj = in_proj(x2d, norm1_w[l], w_in[l].astype(BF16))
        o_f, o_b = hgrn(proj, lb_f_all[l], lb_b_all[l], batch=batch, seq=seq, col0=hgrn_col)
        attn_outs = [attn_pattern(proj, attn_q_norm_w[l], attn_k_norm_w[l], batch=batch, seq=seq,
                                  d_in=d_in, dilation=dil) for _, dil in DILATED_PATTERNS]
        x1, h_ext, aff = out_proj(x2d, attn_outs, o_f, o_b, proj, hgrn_out_norm_w[l], w_out[l].astype(BF16),
                                  norm2_w[l], w_router[l], hg_col=hgrn_col + 4 * HGRN_WIDTH)
        slot, start = routing(aff, batch=batch, seq=seq, cap=cap)

        def expert_major(t):
            return t.reshape(batch, -1, LANES)[:, :, :N_EXPERTS].transpose(0, 2, 1)

        slot_t = expert_major(slot).reshape(batch, N_EXPERTS, 1, seq)
        base = (expert_major(start) // SLOT_ALIGN * SLOT_ALIGN).reshape(-1)
        xin = gather(base, slot_t, h_ext, batch=batch, seq=seq, cap=cap)
        y = expert_ffn(xin, w_expert_gate[l].astype(BF16), w_expert_up[l].astype(BF16),
                       w_expert_down[l].astype(BF16), cap=cap)
        x2d = combine(base, slot_t, y, x1, batch=batch, seq=seq)
    return x2d.reshape(batch, seq, d_model)
```

```python
import functools

import jax
import jax.numpy as jnp
from jax import lax
from jax.experimental import pallas as pl
from jax.experimental.pallas import tpu as pltpu

F32 = jnp.float32
BF16 = jnp.bfloat16

NORM_EPS = 1e-6
NEG_BIG = -1e30
ATTN_HEAD_DIM = 64
ATTN_HEADS = 8
ATTN_WIDTH = ATTN_HEADS * ATTN_HEAD_DIM
DILATED_PATTERNS = ((128, 1), (512, 4), (2048, 16))
ALIBI_MAX_BIAS = 8.0
HGRN_DIM = 128
HGRN_HEADS = 4
HGRN_WIDTH = HGRN_HEADS * HGRN_DIM
N_EXPERTS = 16
CAPACITY_FACTOR = 2
V7X_VMEM_LIMIT_BYTES = 56 * 1024 * 1024


def _dot(a, b, dims=((1,), (0,))):
    return lax.dot_general(a, b, (dims, ((), ())), preferred_element_type=F32)


def _dot_nt(a, b):
    return _dot(a, b, ((1,), (1,)))


def _in_proj_kernel(x_ref, nw_ref, w_ref, qw_ref, kw_ref, o_ref, *, n_chunk):
    x = x_ref[...]
    h = x * lax.rsqrt(jnp.mean(x * x, axis=-1, keepdims=True) + NORM_EPS) * nw_ref[...]
    h = h.astype(BF16)
    n_total = o_ref.shape[1]
    for c in range(n_total // n_chunk):
        cols = slice(c * n_chunk, (c + 1) * n_chunk)
        o_ref[:, cols] = _dot(h, w_ref[:, cols])
    for sl in range(ATTN_WIDTH // LANES):
        cols = slice(sl * LANES, (sl + 1) * LANES)
        o_ref[:, cols] = _head_rms(o_ref[:, cols], qw_ref[:, cols]) * (ATTN_HEAD_DIM ** -0.5)
        kcols = slice(ATTN_WIDTH + sl * LANES, ATTN_WIDTH + (sl + 1) * LANES)
        o_ref[:, kcols] = _head_rms(o_ref[:, kcols], kw_ref[:, cols])


def in_proj(x2d, norm_w, w_bf16, qw, kw, *, tm=256, n_chunk=512):
    m, d = x2d.shape
    n = w_bf16.shape[1]
    wspec = pl.BlockSpec((1, ATTN_WIDTH), lambda i: (0, 0))
    return pl.pallas_call(
        functools.partial(_in_proj_kernel, n_chunk=n_chunk),
        grid=(m // tm,),
        in_specs=[
            pl.BlockSpec((tm, d), lambda i: (i, 0)),
            pl.BlockSpec((1, d), lambda i: (0, 0)),
            pl.BlockSpec((d, n), lambda i: (0, 0)),
            wspec, wspec,
        ],
        out_specs=pl.BlockSpec((tm, n), lambda i: (i, 0)),
        out_shape=jax.ShapeDtypeStruct((m, n), F32),
        compiler_params=pltpu.CompilerParams(
            dimension_semantics=("arbitrary",), vmem_limit_bytes=V7X_VMEM_LIMIT_BYTES),
    )(x2d, norm_w.reshape(1, d), w_bf16,
      jnp.tile(qw, ATTN_HEADS).reshape(1, -1), jnp.tile(kw, ATTN_HEADS).reshape(1, -1))


HGRN_CHUNK = 64
HGRN_SUB = 16
HGRN_MAX_LOG_DECAY = 80.0


def _split3(g):
    hi = g.astype(BF16)
    r1 = g - hi.astype(F32)
    mid = r1.astype(BF16)
    lo = (r1 - mid.astype(F32)).astype(BF16)
    return hi, mid, lo


def _hgrn_chunk(load, store, lb, st_ref, *, reverse, bounded_decay):
    q_raw, z, v = load()
    c, dk = q_raw.shape
    sub = HGRN_SUB
    q = q_raw * jax.nn.sigmoid(q_raw)
    ez = jnp.exp(-jnp.abs(z))
    inv = 1.0 / (1.0 + ez)
    pos = z >= 0.0
    f = lb + (1.0 - lb) * jnp.where(pos, inv, ez * inv)
    k = (1.0 - lb) * jnp.where(pos, ez * inv, inv)
    g = jnp.log(f)

    row = lax.broadcasted_iota(jnp.int32, (c, c), 0)
    col = lax.broadcasted_iota(jnp.int32, (c, c), 1)
    tri = (col >= row) if reverse else (col <= row)
    tri_bf = jnp.where(tri, 1.0, 0.0).astype(BF16)
    g_hi, g_mid, g_lo = _split3(g)
    yield
    cum = _dot(tri_bf, g_hi) + _dot(tri_bf, g_mid) + _dot(tri_bf, g_lo)
    yield

    def edge(r):
        return cum[r:r + 1, :]

    if reverse:
        ref_half, ref_q_lo, ref_q_hi, ref_end = edge(32), edge(16), edge(48), edge(0)
    else:
        ref_half, ref_q_lo, ref_q_hi, ref_end = edge(31), edge(15), edge(47), edge(c - 1)

    r1 = lax.broadcasted_iota(jnp.int32, (c, 1), 0)
    ref_quarter = jnp.where(r1 < 32, ref_q_lo, ref_q_hi)

    st = st_ref[...]
    qe = (q * jnp.exp(cum)).astype(BF16)
    ke = (k * jnp.exp(ref_end - cum)).astype(BF16)
    v_bf = v.astype(BF16)
    q1 = (q * jnp.exp(jnp.minimum(cum - ref_half, 0.0))).astype(BF16)
    k1 = (k * jnp.exp(jnp.minimum(ref_half - cum, 0.0))).astype(BF16)
    q2 = (q * jnp.exp(jnp.minimum(cum - ref_quarter, 0.0))).astype(BF16)
    k2 = (k * jnp.exp(jnp.minimum(ref_quarter - cum, 0.0))).astype(BF16)
    tb, sb = row // sub, col // sub
    if reverse:
        m1 = (tb < 2) & (sb >= 2)
        m2 = ((tb == 0) & (sb == 1)) | ((tb == 2) & (sb == 3))
    else:
        m1 = (tb >= 2) & (sb < 2)
        m2 = ((tb == 1) & (sb == 0)) | ((tb == 3) & (sb == 2))
    if bounded_decay:
        first = (sub - 1) if reverse else 0
        ref_diag = edge(3 * sub + first)
        for blk in (2, 1, 0):
            ref_diag = jnp.where(r1 < (blk + 1) * sub, edge(blk * sub + first), ref_diag)
        q3 = (q * jnp.exp(cum - ref_diag)).astype(BF16)
        k3 = (k * jnp.exp(ref_diag - cum)).astype(BF16)
        m3 = (tb == sb) & ((col >= row) if reverse else (col <= row))
    yield

    o = _dot_nt(qe, st.astype(BF16))
    st_ref[...] = st * jnp.exp(ref_end) + _dot(v_bf, ke, ((0,), (0,)))
    a = jnp.where(m1, _dot_nt(q1, k1), 0.0) + jnp.where(m2, _dot_nt(q2, k2), 0.0)
    if bounded_decay:
        a = a + jnp.where(m3, _dot_nt(q3, k3), 0.0)
        a_bf = a.astype(BF16)
        yield
        store(o + _dot(a_bf, v_bf))
        return
    yield

    t_loc = lax.broadcasted_iota(jnp.int32, (sub, 1), 0)
    lane = lax.broadcasted_iota(jnp.int32, (sub, c), 1)
    blocks = []
    for blk in range(c // sub):
        rows = slice(blk * sub, (blk + 1) * sub)
        cum_b, q_b = cum[rows], q[rows]
        a_b = jnp.zeros((sub, c), F32)
        for s_loc in range(sub):
            s = blk * sub + s_loc
            keep = (t_loc <= s_loc) if reverse else (t_loc >= s_loc)
            e = jnp.exp(jnp.where(keep, cum_b - cum[s:s + 1, :], NEG_BIG))
            p = (q_b * k[s:s + 1, :]) * e
            a_b = jnp.where(lane == s, jnp.sum(p, axis=-1, keepdims=True), a_b)
        blocks.append(a_b)
    a = a + jnp.concatenate(blocks, axis=0)
    store(o + _dot(a.astype(BF16), v_bf))


def _hgrn_kernel(bounded_ref, qf_ref, zf_ref, vf_ref, qb_ref, zb_ref, vb_ref, lbf_ref, lbb_ref,
                 of_ref, ob_ref, sf_ref, sb_ref):
    @pl.when(pl.program_id(1) == 0)
    def _():
        sf_ref[...] = jnp.zeros_like(sf_ref)
        sb_ref[...] = jnp.zeros_like(sb_ref)

    n_chunks = qf_ref.shape[0] // HGRN_CHUNK

    def run(bounded_decay):
        def body(ci, carry):
            rf = pl.ds(pl.multiple_of(ci * HGRN_CHUNK, HGRN_CHUNK), HGRN_CHUNK)
            rb = pl.ds(pl.multiple_of((n_chunks - 1 - ci) * HGRN_CHUNK, HGRN_CHUNK), HGRN_CHUNK)
            chains = []
            for h in range(HGRN_HEADS):
                cols = slice(h * HGRN_DIM, (h + 1) * HGRN_DIM)
                for rows, q_ref, z_ref, v_ref, lb_ref, o_ref, st_ref, reverse in (
                        (rf, qf_ref, zf_ref, vf_ref, lbf_ref, of_ref, sf_ref, False),
                        (rb, qb_ref, zb_ref, vb_ref, lbb_ref, ob_ref, sb_ref, True)):
                    def load(rows=rows, cols=cols, q_ref=q_ref, z_ref=z_ref, v_ref=v_ref):
                        return q_ref[rows, cols], z_ref[rows, cols], v_ref[rows, cols]

                    def store(o, rows=rows, cols=cols, o_ref=o_ref):
                        o_ref[rows, cols] = o

                    chains.append(_hgrn_chunk(load, store, lb_ref[:, cols], st_ref.at[h],
                                              reverse=reverse, bounded_decay=bounded_decay))
            while chains:
                chains = [ch for ch in chains if next(ch, True) is None]
            return carry

        lax.fori_loop(0, n_chunks, body, 0)

    pl.when(bounded_ref[0] == 1)(functools.partial(run, True))
    pl.when(bounded_ref[0] != 1)(functools.partial(run, False))


def hgrn(proj, lb_f, lb_b, *, batch, seq, col0, tile=512):
    nt = seq // tile
    cb = col0 // HGRN_WIDTH
    worst = -(HGRN_SUB - 1) * jnp.log(jnp.minimum(jnp.min(lb_f), jnp.min(lb_b)))
    bounded = (worst <= HGRN_MAX_LOG_DECAY).astype(jnp.int32).reshape(1)

    def fwd(colblock):
        return pl.BlockSpec((tile, HGRN_WIDTH), lambda b, i, flag: (b * nt + i, cb + colblock))

    def bwd(colblock):
        return pl.BlockSpec((tile, HGRN_WIDTH), lambda b, i, flag: (b * nt + nt - 1 - i, cb + colblock))

    lb_spec = pl.BlockSpec((1, HGRN_WIDTH), lambda b, i, flag: (0, 0))
    out_shape = jax.ShapeDtypeStruct((batch * seq, HGRN_WIDTH), F32)
    state = pltpu.VMEM((HGRN_HEADS, HGRN_DIM, HGRN_DIM), F32)
    grid_spec = pltpu.PrefetchScalarGridSpec(
        num_scalar_prefetch=1,
        grid=(batch, nt),
        in_specs=[fwd(0), fwd(1), fwd(3), bwd(0), bwd(2), bwd(3), lb_spec, lb_spec],
        out_specs=[
            pl.BlockSpec((tile, HGRN_WIDTH), lambda b, i, flag: (b * nt + i, 0)),
            pl.BlockSpec((tile, HGRN_WIDTH), lambda b, i, flag: (b * nt + nt - 1 - i, 0)),
        ],
        scratch_shapes=[state, state],
    )
    return pl.pallas_call(
        _hgrn_kernel,
        grid_spec=grid_spec,
        out_shape=[out_shape, out_shape],
        compiler_params=pltpu.CompilerParams(
            dimension_semantics=("arbitrary", "arbitrary"),
            vmem_limit_bytes=V7X_VMEM_LIMIT_BYTES),
    )(bounded, proj, proj, proj, proj, proj, proj, lb_f.reshape(1, -1), lb_b.reshape(1, -1))


ATTN_HALF = 64
ATTN_QT = 128
LANES = 128
ATTN_SLABS = ATTN_WIDTH // LANES


def _head_rms(xs, w):
    lo = lax.broadcasted_iota(jnp.int32, (1, LANES), 1) < ATTN_HEAD_DIM
    sq = xs * xs
    s_lo = jnp.sum(jnp.where(lo, sq, 0.0), axis=-1, keepdims=True)
    s_hi = jnp.sum(jnp.where(lo, 0.0, sq), axis=-1, keepdims=True)
    ms = jnp.where(lo, s_lo, s_hi) * (1.0 / ATTN_HEAD_DIM)
    return xs * lax.rsqrt(ms + NORM_EPS) * w


def _attn_kernel(*refs, seq):
    n_pat = len(DILATED_PATTERNS)
    ns = ATTN_SLABS
    q_refs, k_refs, kp_refs, kn_refs, v_refs, vp_refs, vn_refs = [refs[g * ns:(g + 1) * ns] for g in range(7)]
    bias_refs = refs[7 * ns:7 * ns + n_pat]
    o_ref, kwin, vwin, qn, s_scr, o_scr, l_scr = refs[7 * ns + n_pat:]
    tile = o_ref.shape[0]
    i = pl.program_id(1)
    lo = lax.broadcasted_iota(jnp.int32, (1, LANES), 1) < ATTN_HEAD_DIM

    for pi, ((_, dil), bias_ref) in enumerate(zip(DILATED_PATTERNS, bias_refs)):
        sub_rows = tile // dil
        qt = min(ATTN_QT, sub_rows)
        kt = qt + 2 * ATTN_HALF
        halo = ATTN_HALF * dil
        kcol = lax.broadcasted_iota(jnp.int32, (1, kt), 1)

        def subsequence(r, carry, pi=pi, dil=dil, bias_ref=bias_ref, sub_rows=sub_rows, qt=qt, kt=kt,
                        halo=halo, kcol=kcol):
            def rows_of(ref, start, n):
                return ref[pl.ds(start + r, n, stride=dil), :]

            main = slice(ATTN_HALF, ATTN_HALF + sub_rows)
            after = slice(ATTN_HALF + sub_rows, 2 * ATTN_HALF + sub_rows)
            for sl in range(ns):
                kwin[sl, 0:ATTN_HALF, :] = rows_of(kp_refs[sl], tile - halo, ATTN_HALF).astype(BF16)
                kwin[sl, main, :] = rows_of(k_refs[sl], 0, sub_rows).astype(BF16)
                kwin[sl, after, :] = rows_of(kn_refs[sl], 0, ATTN_HALF).astype(BF16)
                vwin[sl, 0:ATTN_HALF, :] = rows_of(vp_refs[sl], tile - halo, ATTN_HALF).astype(BF16)
                vwin[sl, main, :] = rows_of(v_refs[sl], 0, sub_rows).astype(BF16)
                vwin[sl, after, :] = rows_of(vn_refs[sl], 0, ATTN_HALF).astype(BF16)
                qn[sl, 0:sub_rows, :] = rows_of(q_refs[sl], 0, sub_rows).astype(BF16)

            def sub(j, carry2):
                r0 = pl.multiple_of(j * qt, qt)
                lk = (i * tile) // dil + j * qt - ATTN_HALF + kcol
                edge = jnp.where((lk >= 0) & (lk < seq // dil), 0.0, NEG_BIG)
                out_rows = pl.ds(r + j * (qt * dil), qt, stride=dil)
                for sl in range(ns):
                    qs = qn[sl, pl.ds(r0, qt), :]
                    ks = kwin[sl, pl.ds(r0, kt), :]
                    for hh in range(2):
                        qh = jnp.where(lo if hh == 0 else jnp.logical_not(lo), qs, jnp.zeros_like(qs))
                        s_scr[2 * sl + hh, 0:qt, 0:kt] = _dot_nt(qh, ks) + bias_ref[2 * sl + hh] + edge
                results = {}

                def head(h):
                    s = s_scr[h, 0:qt, 0:kt]
                    m = jnp.max(s, axis=-1, keepdims=True)
                    yield
                    p = jnp.exp(s - m)
                    den = jnp.sum(p, axis=-1, keepdims=True)
                    p_bf = p.astype(BF16)
                    yield
                    pv = _dot(p_bf, vwin[h // 2, pl.ds(r0, kt), :])
                    yield
                    results[h] = (pv * (1.0 / den), m + jnp.log(den))

                chains = [head(h) for h in range(ATTN_HEADS)]
                while chains:
                    chains = [ch for ch in chains if next(ch, True) is None]
                for sl in range(ns):
                    (o_lo, l_lo), (o_hi, l_hi) = results[2 * sl], results[2 * sl + 1]
                    o_scr[pi * ns + sl, out_rows, :] = jnp.where(lo, o_lo, o_hi)
                    l_scr[pi * ns + sl, out_rows, :] = jnp.where(lo, l_lo, l_hi)
                return carry2

            lax.fori_loop(0, sub_rows // qt, sub, 0)
            return carry

        lax.fori_loop(0, dil, subsequence, 0)

    def merge(c, carry):
        rows = pl.ds(pl.multiple_of(c * ATTN_QT, ATTN_QT), ATTN_QT)
        for sl in range(ns):
            ls = [l_scr[p * ns + sl, rows, :] for p in range(n_pat)]
            mx = functools.reduce(jnp.maximum, ls)
            ws = [jnp.exp(l - mx) for l in ls]
            num = sum(w * o_scr[p * ns + sl, rows, :] for p, w in enumerate(ws))
            o_ref[rows, sl * LANES:(sl + 1) * LANES] = num / sum(ws)
        return carry

    lax.fori_loop(0, tile // ATTN_QT, merge, 0)


def _attn_bias(dilation, qt):
    slopes = jnp.exp2(-ALIBI_MAX_BIAS * jnp.arange(1, ATTN_HEADS + 1, dtype=F32) / ATTN_HEADS)
    t = jnp.arange(qt)[:, None]
    j = jnp.arange(qt + 2 * ATTN_HALF)[None, :]
    dist = jnp.abs(j - ATTN_HALF - t)
    alibi = -slopes[:, None, None] * (dilation * dist).astype(F32)[None]
    return jnp.where((dist <= ATTN_HALF)[None], alibi, NEG_BIG)


def attention(proj, *, batch, seq, tile=1024):
    nt = seq // tile
    n_pat = len(DILATED_PATTERNS)
    assert all(tile % (ATTN_HALF * dil) == 0 for _, dil in DILATED_PATTERNS)

    def slabs(c, shift):
        def spec(sl):
            def index(b, i):
                return (b * nt + jnp.clip(i + shift, 0, nt - 1), c * ATTN_SLABS + sl)
            return pl.BlockSpec((tile, LANES), index)
        return [spec(sl) for sl in range(ATTN_SLABS)]

    groups = [(0, 0), (1, 0), (1, -1), (1, 1), (2, 0), (2, -1), (2, 1)]
    biases = [_attn_bias(dil, min(ATTN_QT, tile // dil)) for _, dil in DILATED_PATTERNS]
    bias_specs = [pl.BlockSpec(bias.shape, lambda b, i: (0, 0, 0)) for bias in biases]
    return pl.pallas_call(
        functools.partial(_attn_kernel, seq=seq),
        grid=(batch, nt),
        in_specs=[s for c, shift in groups for s in slabs(c, shift)] + bias_specs,
        out_specs=pl.BlockSpec((tile, ATTN_WIDTH), lambda b, i: (b * nt + i, 0)),
        out_shape=jax.ShapeDtypeStruct((batch * seq, ATTN_WIDTH), F32),
        scratch_shapes=[pltpu.VMEM((ATTN_SLABS, tile + 2 * ATTN_HALF, LANES), BF16),
                        pltpu.VMEM((ATTN_SLABS, tile + 2 * ATTN_HALF, LANES), BF16),
                        pltpu.VMEM((ATTN_SLABS, tile, LANES), BF16),
                        pltpu.VMEM((ATTN_HEADS, ATTN_QT, ATTN_QT + 2 * ATTN_HALF), F32),
                        pltpu.VMEM((n_pat * ATTN_SLABS, tile, LANES), F32),
                        pltpu.VMEM((n_pat * ATTN_SLABS, tile, LANES), F32)],
        compiler_params=pltpu.CompilerParams(
            dimension_semantics=("arbitrary", "arbitrary"), vmem_limit_bytes=V7X_VMEM_LIMIT_BYTES),
    )(*([proj] * (len(groups) * ATTN_SLABS)), *biases)


def _split2(x):
    hi = x.astype(BF16)
    return hi, (x - hi.astype(F32)).astype(BF16)


def _out_proj_kernel(x_ref, a_ref, of_ref, ob_ref, hg_ref, hw_ref, wo_ref, n2_ref, wr_hi_ref, wr_lo_ref,
                     x1_ref, h2_ref, aff_ref):
    a_out = a_ref[...]
    o = of_ref[...] + ob_ref[...]
    hg = hg_ref[...]
    hw = hw_ref[...]
    b_parts = []
    for sl in range(HGRN_HEADS):
        cols = slice(sl * HGRN_DIM, (sl + 1) * HGRN_DIM)
        os_ = o[:, cols]
        y = os_ * lax.rsqrt(jnp.mean(os_ * os_, axis=-1, keepdims=True) + NORM_EPS) * hw
        g = hg[:, cols]
        b_parts.append(y * (g * jax.nn.sigmoid(g)))
    mixed = jnp.concatenate([a_out] + b_parts, axis=-1).astype(BF16)

    x1 = x_ref[...] + _dot(mixed, wo_ref[...])
    x1_ref[...] = x1
    h2 = x1 * lax.rsqrt(jnp.mean(x1 * x1, axis=-1, keepdims=True) + NORM_EPS) * n2_ref[...]
    d = x1.shape[1]
    h_hi, h_lo = _split2(h2)
    h2_ref[:, 0:d] = h_hi

    wr_hi = wr_hi_ref[...]
    logits = _dot(h_hi, wr_hi) + _dot(h_lo, wr_hi) + _dot(h_hi, wr_lo_ref[...])
    valid = lax.broadcasted_iota(jnp.int32, (1, LANES), 1) < N_EXPERTS
    logits = jnp.where(valid, logits, NEG_BIG)
    ex = jnp.exp(logits - jnp.max(logits, axis=-1, keepdims=True))
    aff = ex / jnp.sum(ex, axis=-1, keepdims=True)
    aff_ref[...] = aff
    a_hi, a_lo = _split2(aff)
    h2_ref[:, d:d + LANES] = a_hi
    h2_ref[:, d + LANES:d + 2 * LANES] = a_lo


def out_proj(x2d, a_out, o_f, o_b, proj, hgrn_norm_w, w_out_bf16, norm2_w, w_router, *, hg_col, tm=256):
    m, d = x2d.shape
    wr = jnp.pad(w_router, ((0, 0), (0, LANES - N_EXPERTS)))
    wr_hi, wr_lo = _split2(wr)

    def rows(width, colblock=0):
        return pl.BlockSpec((tm, width), lambda i: (i, colblock))

    def const(shape):
        return pl.BlockSpec(shape, lambda i: (0, 0))

    return pl.pallas_call(
        _out_proj_kernel,
        grid=(m // tm,),
        in_specs=[rows(d), rows(ATTN_WIDTH),
                  rows(HGRN_WIDTH), rows(HGRN_WIDTH), rows(HGRN_WIDTH, hg_col // HGRN_WIDTH),
                  const((1, HGRN_DIM)), const((ATTN_WIDTH + HGRN_WIDTH, d)),
                  const((1, d)), const((d, LANES)), const((d, LANES))],
        out_specs=[rows(d), rows(d + 2 * LANES), rows(LANES)],
        out_shape=[jax.ShapeDtypeStruct((m, d), F32), jax.ShapeDtypeStruct((m, d + 2 * LANES), BF16),
                   jax.ShapeDtypeStruct((m, LANES), F32)],
        compiler_params=pltpu.CompilerParams(
            dimension_semantics=("arbitrary",), vmem_limit_bytes=V7X_VMEM_LIMIT_BYTES),
    )(x2d, a_out, o_f, o_b, proj, hgrn_norm_w.reshape(1, -1), w_out_bf16,
      norm2_w.reshape(1, -1), wr_hi, wr_lo)


ROUTE_BLOCK = 128
COUNT_ROWS = 512


def _routing_kernel(aff_ref, slot_ref, start_ref, *, cap):
    seq = aff_ref.shape[0]

    def bits_at(r0, n):
        return pltpu.bitcast(aff_ref[pl.ds(r0, n), :], jnp.int32)

    def count(pred):
        def body(c, acc):
            blk = bits_at(pl.multiple_of(c * COUNT_ROWS, COUNT_ROWS), COUNT_ROWS)
            return acc + jnp.sum(jnp.where(pred(blk), 1, 0), axis=0, keepdims=True)
        return lax.fori_loop(0, seq // COUNT_ROWS, body, jnp.zeros((1, LANES), jnp.int32))

    def bit_step(t, thr):
        cand = thr | jnp.left_shift(jnp.int32(1), 30 - t)
        return jnp.where(count(lambda blk: blk >= cand) >= cap, cand, thr)

    thr = lax.fori_loop(0, 31, bit_step, jnp.zeros((1, LANES), jnp.int32))
    need = (cap - count(lambda blk: blk > thr)).astype(F32)

    row = lax.broadcasted_iota(jnp.int32, (ROUTE_BLOCK, ROUTE_BLOCK), 0)
    col = lax.broadcasted_iota(jnp.int32, (ROUTE_BLOCK, ROUTE_BLOCK), 1)
    before = jnp.where(col < row, 1.0, 0.0).astype(BF16)

    def assign(j, carry):
        c_eq, c_sel = carry
        r0 = pl.multiple_of(j * ROUTE_BLOCK, ROUTE_BLOCK)
        blk = bits_at(r0, ROUTE_BLOCK)
        gt, eq = blk > thr, blk == thr
        eq_f = jnp.where(eq, 1.0, 0.0)
        eq_rank = _dot(before, eq_f.astype(BF16)) + c_eq
        sel = gt | (eq & (eq_rank < need))
        sel_f = jnp.where(sel, 1.0, 0.0)
        rank = _dot(before, sel_f.astype(BF16)) + c_sel
        slot_ref[pl.ds(r0, ROUTE_BLOCK), :] = jnp.where(sel, rank, -1.0).astype(jnp.int32)
        start_ref[pl.ds(j, 1), :] = c_sel.astype(jnp.int32)
        return (c_eq + jnp.sum(eq_f, axis=0, keepdims=True), c_sel + jnp.sum(sel_f, axis=0, keepdims=True))

    zero = jnp.zeros((1, LANES), F32)
    lax.fori_loop(0, seq // ROUTE_BLOCK, assign, (zero, zero))


def routing(aff, *, batch, seq, cap):
    nblk = seq // ROUTE_BLOCK
    return pl.pallas_call(
        functools.partial(_routing_kernel, cap=cap),
        grid=(batch,),
        in_specs=[pl.BlockSpec((seq, LANES), lambda b: (b, 0))],
        out_specs=[pl.BlockSpec((seq, LANES), lambda b: (b, 0)), pl.BlockSpec((nblk, LANES), lambda b: (b, 0))],
        out_shape=[jax.ShapeDtypeStruct((batch * seq, LANES), jnp.int32),
                   jax.ShapeDtypeStruct((batch * nblk, LANES), jnp.int32)],
        compiler_params=pltpu.CompilerParams(
            dimension_semantics=("arbitrary",), vmem_limit_bytes=V7X_VMEM_LIMIT_BYTES),
    )(aff)


SLOT_ALIGN = 16
SLOT_WIN = ROUTE_BLOCK + SLOT_ALIGN
FFN_ROWS = 256


def _slot_hits(slot_row, base):
    r = lax.broadcasted_iota(jnp.int32, (SLOT_WIN, ROUTE_BLOCK), 0)
    return (slot_row - base) == r


def _one_hot(hit):
    return jnp.where(hit, 1.0, 0.0).astype(BF16)


def _gather_kernel(base_ref, slot_ref, h_ref, xin_ref, *, n_blocks):
    n_exp = slot_ref.shape[1]
    pair0 = pl.program_id(0) * n_exp
    xin_ref[...] = jnp.zeros_like(xin_ref)

    def block(j, carry):
        tok = pl.ds(pl.multiple_of(j * ROUTE_BLOCK, ROUTE_BLOCK), ROUTE_BLOCK)
        h_blk = h_ref[tok, :]
        for e in range(n_exp):
            base = pl.multiple_of(base_ref[(pair0 + e) * n_blocks + j], SLOT_ALIGN)
            hit = _slot_hits(slot_ref[0, e, :, tok], base)
            win = pl.ds(base, SLOT_WIN)
            xin_ref[0, e, win, :] += _dot(_one_hot(hit), h_blk).astype(BF16)
        return carry

    lax.fori_loop(0, n_blocks, block, 0)


def gather(base, slot_t, h_ext, *, batch, seq, cap, tn=256):
    n_exp = slot_t.shape[1]
    width = h_ext.shape[1]
    rows = cap + SLOT_WIN
    grid_spec = pltpu.PrefetchScalarGridSpec(
        num_scalar_prefetch=1,
        grid=(batch, width // tn),
        in_specs=[pl.BlockSpec((1, n_exp, 1, seq), lambda b, n, base: (b, 0, 0, 0)),
                  pl.BlockSpec((seq, tn), lambda b, n, base: (b, n))],
        out_specs=pl.BlockSpec((1, n_exp, rows, tn), lambda b, n, base: (b, 0, 0, n)),
    )
    return pl.pallas_call(
        functools.partial(_gather_kernel, n_blocks=seq // ROUTE_BLOCK),
        grid_spec=grid_spec,
        out_shape=jax.ShapeDtypeStruct((batch, n_exp, rows, width), BF16),
        compiler_params=pltpu.CompilerParams(
            dimension_semantics=("arbitrary", "arbitrary"), vmem_limit_bytes=V7X_VMEM_LIMIT_BYTES),
    )(base, slot_t, h_ext)


def _ffn_kernel(xin_ref, g_ref, wg_ref, wu_ref, wd_ref, y_ref, *, cap):
    e = pl.program_id(0)
    lane = lax.broadcasted_iota(jnp.int32, (1, 2 * LANES), 1)
    mine = (lane == e) | (lane == LANES + e)
    for rb in range(cap // FFN_ROWS):
        rows = slice(rb * FFN_ROWS, (rb + 1) * FFN_ROWS)
        xb = xin_ref[0, 0, rows, :]
        gate = jnp.sum(jnp.where(mine, g_ref[0, 0, rows, :].astype(F32), 0.0), axis=-1, keepdims=True)
        gate_h = _dot(xb, wg_ref[0])
        hid = (gate_h * jax.nn.sigmoid(gate_h)) * _dot(xb, wu_ref[0])
        y_ref[0, 0, rows, :] = (_dot(hid.astype(BF16), wd_ref[0]) * gate).astype(BF16)
    y_ref[0, 0, cap:, :] = jnp.zeros((y_ref.shape[2] - cap, y_ref.shape[3]), BF16)


def expert_ffn(xin, wg, wu, wd, *, cap):
    batch, n_exp, rows, width = xin.shape
    _, d, f = wg.shape
    return pl.pallas_call(
        functools.partial(_ffn_kernel, cap=cap),
        grid=(n_exp, batch),
        in_specs=[pl.BlockSpec((1, 1, cap, d), lambda e, b: (b, e, 0, 0)),
                  pl.BlockSpec((1, 1, cap, 2 * LANES), lambda e, b: (b, e, 0, d // (2 * LANES))),
                  pl.BlockSpec((1, d, f), lambda e, b: (e, 0, 0)),
                  pl.BlockSpec((1, d, f), lambda e, b: (e, 0, 0)),
                  pl.BlockSpec((1, f, d), lambda e, b: (e, 0, 0))],
        out_specs=pl.BlockSpec((1, 1, rows, d), lambda e, b: (b, e, 0, 0)),
        out_shape=jax.ShapeDtypeStruct((batch, n_exp, rows, d), BF16),
        compiler_params=pltpu.CompilerParams(
            dimension_semantics=("arbitrary", "arbitrary"), vmem_limit_bytes=V7X_VMEM_LIMIT_BYTES),
    )(xin, xin, wg, wu, wd)


def _combine_kernel(base_ref, slot_ref, y_ref, x1_ref, out_ref, *, n_blocks):
    n_exp = slot_ref.shape[1]
    pair0 = pl.program_id(0) * n_exp
    blocks_here = out_ref.shape[0] // ROUTE_BLOCK
    first = pl.program_id(2) * blocks_here

    def block(jj, carry):
        j = first + jj
        rows = pl.ds(pl.multiple_of(jj * ROUTE_BLOCK, ROUTE_BLOCK), ROUTE_BLOCK)
        tok = pl.ds(pl.multiple_of(j * ROUTE_BLOCK, ROUTE_BLOCK), ROUTE_BLOCK)
        acc = x1_ref[rows, :]
        for e in range(n_exp):
            base = pl.multiple_of(base_ref[(pair0 + e) * n_blocks + j], SLOT_ALIGN)
            hit = _slot_hits(slot_ref[0, e, :, tok], base)
            y_win = y_ref[0, e, pl.ds(base, SLOT_WIN), :]
            acc = acc + _dot(_one_hot(hit), y_win, ((0,), (0,)))
        out_ref[rows, :] = acc
        return carry

    lax.fori_loop(0, blocks_here, block, 0)


def combine(base, slot_t, y, x1, *, batch, seq, tn=512, tt=2048):
    n_exp, rows, d = y.shape[1], y.shape[2], y.shape[3]
    grid_spec = pltpu.PrefetchScalarGridSpec(
        num_scalar_prefetch=1,
        grid=(batch, d // tn, seq // tt),
        in_specs=[pl.BlockSpec((1, n_exp, 1, seq), lambda b, n, t, base: (b, 0, 0, 0)),
                  pl.BlockSpec((1, n_exp, rows, tn), lambda b, n, t, base: (b, 0, 0, n),
                               pipeline_mode=pl.Buffered(1)),
                  pl.BlockSpec((tt, tn), lambda b, n, t, base: (b * (seq // tt) + t, n))],
        out_specs=pl.BlockSpec((tt, tn), lambda b, n, t, base: (b * (seq // tt) + t, n)),
    )
    return pl.pallas_call(
        functools.partial(_combine_kernel, n_blocks=seq // ROUTE_BLOCK),
        grid_spec=grid_spec,
        out_shape=jax.ShapeDtypeStruct((batch * seq, d), F32),
        compiler_params=pltpu.CompilerParams(
            dimension_semantics=("arbitrary", "arbitrary", "arbitrary"),
            vmem_limit_bytes=V7X_VMEM_LIMIT_BYTES),
    )(base, slot_t, y, x1)


def kernel(x, norm1_w, w_in, attn_q_norm_w, attn_k_norm_w, hgrn_lb_fwd, hgrn_lb_bwd, hgrn_out_norm_w,
           w_out, norm2_w, w_router, w_expert_gate, w_expert_up, w_expert_down):
    batch, seq, d_model = x.shape
    depth = w_in.shape[0]
    cap = max(1, CAPACITY_FACTOR * seq // N_EXPERTS)
    hgrn_col = 3 * ATTN_WIDTH
    lb_f_all = jnp.cumsum(jax.nn.softmax(hgrn_lb_fwd.astype(F32), axis=0), axis=0)
    lb_b_all = jnp.cumsum(jax.nn.softmax(hgrn_lb_bwd.astype(F32), axis=0), axis=0)

    x2d = x.reshape(batch * seq, d_model)
    for l in range(depth):
        proj = in_proj(x2d, norm1_w[l], w_in[l].astype(BF16), attn_q_norm_w[l], attn_k_norm_w[l])
        o_f, o_b = hgrn(proj, lb_f_all[l], lb_b_all[l], batch=batch, seq=seq, col0=hgrn_col)
        a_out = attention(proj, batch=batch, seq=seq)
        x1, h_ext, aff = out_proj(x2d, a_out, o_f, o_b, proj, hgrn_out_norm_w[l], w_out[l].astype(BF16),
                                  norm2_w[l], w_router[l], hg_col=hgrn_col + 4 * HGRN_WIDTH)
        slot, start = routing(aff, batch=batch, seq=seq, cap=cap)

        def expert_major(t):
            return t.reshape(batch, -1, LANES)[:, :, :N_EXPERTS].transpose(0, 2, 1)

        slot_t = expert_major(slot).reshape(batch, N_EXPERTS, 1, seq)
        base = (expert_major(start) // SLOT_ALIGN * SLOT_ALIGN).reshape(-1)
        xin = gather(base, slot_t, h_ext, batch=batch, seq=seq, cap=cap)
        y = expert_ffn(xin, w_expert_gate[l].astype(BF16), w_expert_up[l].astype(BF16),
                       w_expert_down[l].astype(BF16), cap=cap)
        x2d = combine(base, slot_t, y, x1, batch=batch, seq=seq)
    return x2d.reshape(batch, seq, d_model)
```

```python
import functools

import jax
import jax.numpy as jnp
from jax import lax
from jax.experimental import pallas as pl
from jax.experimental.pallas import tpu as pltpu

F32 = jnp.float32
BF16 = jnp.bfloat16

NORM_EPS = 1e-6
NEG_BIG = -1e30
ATTN_HEAD_DIM = 64
ATTN_HEADS = 8
ATTN_WIDTH = ATTN_HEADS * ATTN_HEAD_DIM
DILATED_PATTERNS = ((128, 1), (512, 4), (2048, 16))
ALIBI_MAX_BIAS = 8.0
HGRN_DIM = 128
HGRN_HEADS = 4
HGRN_WIDTH = HGRN_HEADS * HGRN_DIM
N_EXPERTS = 16
CAPACITY_FACTOR = 2
V7X_VMEM_LIMIT_BYTES = 56 * 1024 * 1024


def _dot(a, b, dims=((1,), (0,))):
    return lax.dot_general(a, b, (dims, ((), ())), preferred_element_type=F32)


def _dot_nt(a, b):
    return _dot(a, b, ((1,), (1,)))


def _cast_chunks(src_ref, dst_ref, rows=256):
    for r0 in range(0, src_ref.shape[0], rows):
        dst_ref[r0:r0 + rows, :] = src_ref[r0:r0 + rows, :].astype(dst_ref.dtype)


def _in_proj_kernel(x_ref, nw_ref, w32_ref, qw_ref, kw_ref, o_ref, w_ref, *, n_chunk):
    @pl.when(pl.program_id(0) == 0)
    def _():
        _cast_chunks(w32_ref, w_ref)

    x = x_ref[...]
    h = x * lax.rsqrt(jnp.mean(x * x, axis=-1, keepdims=True) + NORM_EPS) * nw_ref[...]
    h = h.astype(BF16)
    n_total = o_ref.shape[1]
    for c in range(n_total // n_chunk):
        cols = slice(c * n_chunk, (c + 1) * n_chunk)
        o_ref[:, cols] = _dot(h, w_ref[:, cols])
    for sl in range(ATTN_WIDTH // LANES):
        cols = slice(sl * LANES, (sl + 1) * LANES)
        o_ref[:, cols] = _head_rms(o_ref[:, cols], qw_ref[:, cols]) * (ATTN_HEAD_DIM ** -0.5)
        kcols = slice(ATTN_WIDTH + sl * LANES, ATTN_WIDTH + (sl + 1) * LANES)
        o_ref[:, kcols] = _head_rms(o_ref[:, kcols], kw_ref[:, cols])


def in_proj(x2d, norm_w, w, qw, kw, *, tm=256, n_chunk=512):
    m, d = x2d.shape
    n = w.shape[1]
    wspec = pl.BlockSpec((1, ATTN_WIDTH), lambda i: (0, 0))
    return pl.pallas_call(
        functools.partial(_in_proj_kernel, n_chunk=n_chunk),
        grid=(m // tm,),
        in_specs=[
            pl.BlockSpec((tm, d), lambda i: (i, 0)),
            pl.BlockSpec((1, d), lambda i: (0, 0)),
            pl.BlockSpec((d, n), lambda i: (0, 0), pipeline_mode=pl.Buffered(1)),
            wspec, wspec,
        ],
        out_specs=pl.BlockSpec((tm, n), lambda i: (i, 0)),
        out_shape=jax.ShapeDtypeStruct((m, n), F32),
        scratch_shapes=[pltpu.VMEM((d, n), BF16)],
        compiler_params=pltpu.CompilerParams(
            dimension_semantics=("arbitrary",), vmem_limit_bytes=V7X_VMEM_LIMIT_BYTES),
    )(x2d, norm_w.reshape(1, d), w,
      jnp.tile(qw, ATTN_HEADS).reshape(1, -1), jnp.tile(kw, ATTN_HEADS).reshape(1, -1))


HGRN_CHUNK = 64
HGRN_SUB = 16
HGRN_MAX_LOG_DECAY = 80.0


def _split3(g):
    hi = g.astype(BF16)
    r1 = g - hi.astype(F32)
    mid = r1.astype(BF16)
    lo = (r1 - mid.astype(F32)).astype(BF16)
    return hi, mid, lo


def _hgrn_chunk(load, store, lb, st_ref, *, reverse, bounded_decay):
    q_raw, z, v = load()
    c, dk = q_raw.shape
    sub = HGRN_SUB
    q = q_raw * jax.nn.sigmoid(q_raw)
    ez = jnp.exp(-jnp.abs(z))
    inv = 1.0 / (1.0 + ez)
    pos = z >= 0.0
    f = lb + (1.0 - lb) * jnp.where(pos, inv, ez * inv)
    k = (1.0 - lb) * jnp.where(pos, ez * inv, inv)
    g = jnp.log(f)

    row = lax.broadcasted_iota(jnp.int32, (c, c), 0)
    col = lax.broadcasted_iota(jnp.int32, (c, c), 1)
    tri = (col >= row) if reverse else (col <= row)
    tri_bf = jnp.where(tri, 1.0, 0.0).astype(BF16)
    g_hi, g_mid, g_lo = _split3(g)
    yield
    cum = _dot(tri_bf, g_hi) + _dot(tri_bf, g_mid) + _dot(tri_bf, g_lo)
    yield

    def edge(r):
        return cum[r:r + 1, :]

    if reverse:
        ref_half, ref_q_lo, ref_q_hi, ref_end = edge(32), edge(16), edge(48), edge(0)
    else:
        ref_half, ref_q_lo, ref_q_hi, ref_end = edge(31), edge(15), edge(47), edge(c - 1)

    r1 = lax.broadcasted_iota(jnp.int32, (c, 1), 0)
    ref_quarter = jnp.where(r1 < 32, ref_q_lo, ref_q_hi)

    st = st_ref[...]
    qe = (q * jnp.exp(cum)).astype(BF16)
    ke = (k * jnp.exp(ref_end - cum)).astype(BF16)
    v_bf = v.astype(BF16)
    q1 = (q * jnp.exp(jnp.minimum(cum - ref_half, 0.0))).astype(BF16)
    k1 = (k * jnp.exp(jnp.minimum(ref_half - cum, 0.0))).astype(BF16)
    q2 = (q * jnp.exp(jnp.minimum(cum - ref_quarter, 0.0))).astype(BF16)
    k2 = (k * jnp.exp(jnp.minimum(ref_quarter - cum, 0.0))).astype(BF16)
    tb, sb = row // sub, col // sub
    if reverse:
        m1 = (tb < 2) & (sb >= 2)
        m2 = ((tb == 0) & (sb == 1)) | ((tb == 2) & (sb == 3))
    else:
        m1 = (tb >= 2) & (sb < 2)
        m2 = ((tb == 1) & (sb == 0)) | ((tb == 3) & (sb == 2))
    if bounded_decay:
        first = (sub - 1) if reverse else 0
        ref_diag = edge(3 * sub + first)
        for blk in (2, 1, 0):
            ref_diag = jnp.where(r1 < (blk + 1) * sub, edge(blk * sub + first), ref_diag)
        q3 = (q * jnp.exp(cum - ref_diag)).astype(BF16)
        k3 = (k * jnp.exp(ref_diag - cum)).astype(BF16)
        m3 = (tb == sb) & ((col >= row) if reverse else (col <= row))
    yield

    o = _dot_nt(qe, st.astype(BF16))
    st_ref[...] = st * jnp.exp(ref_end) + _dot(v_bf, ke, ((0,), (0,)))
    a = jnp.where(m1, _dot_nt(q1, k1), 0.0) + jnp.where(m2, _dot_nt(q2, k2), 0.0)
    if bounded_decay:
        a = a + jnp.where(m3, _dot_nt(q3, k3), 0.0)
        a_bf = a.astype(BF16)
        yield
        store(o + _dot(a_bf, v_bf))
        return
    yield

    t_loc = lax.broadcasted_iota(jnp.int32, (sub, 1), 0)
    lane = lax.broadcasted_iota(jnp.int32, (sub, c), 1)
    blocks = []
    for blk in range(c // sub):
        rows = slice(blk * sub, (blk + 1) * sub)
        cum_b, q_b = cum[rows], q[rows]
        a_b = jnp.zeros((sub, c), F32)
        for s_loc in range(sub):
            s = blk * sub + s_loc
            keep = (t_loc <= s_loc) if reverse else (t_loc >= s_loc)
            e = jnp.exp(jnp.where(keep, cum_b - cum[s:s + 1, :], NEG_BIG))
            p = (q_b * k[s:s + 1, :]) * e
            a_b = jnp.where(lane == s, jnp.sum(p, axis=-1, keepdims=True), a_b)
        blocks.append(a_b)
    a = a + jnp.concatenate(blocks, axis=0)
    store(o + _dot(a.astype(BF16), v_bf))


def _hgrn_kernel(bounded_ref, qf_ref, zf_ref, vf_ref, qb_ref, zb_ref, vb_ref, lbf_ref, lbb_ref,
                 of_ref, ob_ref, sf_ref, sb_ref):
    @pl.when(pl.program_id(1) == 0)
    def _():
        sf_ref[...] = jnp.zeros_like(sf_ref)
        sb_ref[...] = jnp.zeros_like(sb_ref)

    n_chunks = qf_ref.shape[0] // HGRN_CHUNK

    def run(bounded_decay):
        def body(ci, carry):
            rf = pl.ds(pl.multiple_of(ci * HGRN_CHUNK, HGRN_CHUNK), HGRN_CHUNK)
            rb = pl.ds(pl.multiple_of((n_chunks - 1 - ci) * HGRN_CHUNK, HGRN_CHUNK), HGRN_CHUNK)
            chains = []
            for h in range(HGRN_HEADS):
                cols = slice(h * HGRN_DIM, (h + 1) * HGRN_DIM)
                for rows, q_ref, z_ref, v_ref, lb_ref, o_ref, st_ref, reverse in (
                        (rf, qf_ref, zf_ref, vf_ref, lbf_ref, of_ref, sf_ref, False),
                        (rb, qb_ref, zb_ref, vb_ref, lbb_ref, ob_ref, sb_ref, True)):
                    def load(rows=rows, cols=cols, q_ref=q_ref, z_ref=z_ref, v_ref=v_ref):
                        return q_ref[rows, cols], z_ref[rows, cols], v_ref[rows, cols]

                    def store(o, rows=rows, cols=cols, o_ref=o_ref):
                        o_ref[rows, cols] = o

                    chains.append(_hgrn_chunk(load, store, lb_ref[:, cols], st_ref.at[h],
                                              reverse=reverse, bounded_decay=bounded_decay))
            while chains:
                chains = [ch for ch in chains if next(ch, True) is None]
            return carry

        lax.fori_loop(0, n_chunks, body, 0)

    pl.when(bounded_ref[0] == 1)(functools.partial(run, True))
    pl.when(bounded_ref[0] != 1)(functools.partial(run, False))


def hgrn(proj, lb_f, lb_b, *, batch, seq, col0, tile=512):
    nt = seq // tile
    cb = col0 // HGRN_WIDTH
    worst = -(HGRN_SUB - 1) * jnp.log(jnp.minimum(jnp.min(lb_f), jnp.min(lb_b)))
    bounded = (worst <= HGRN_MAX_LOG_DECAY).astype(jnp.int32).reshape(1)

    def fwd(colblock):
        return pl.BlockSpec((tile, HGRN_WIDTH), lambda b, i, flag: (b * nt + i, cb + colblock))

    def bwd(colblock):
        return pl.BlockSpec((tile, HGRN_WIDTH), lambda b, i, flag: (b * nt + nt - 1 - i, cb + colblock))

    lb_spec = pl.BlockSpec((1, HGRN_WIDTH), lambda b, i, flag: (0, 0))
    out_shape = jax.ShapeDtypeStruct((batch * seq, HGRN_WIDTH), F32)
    state = pltpu.VMEM((HGRN_HEADS, HGRN_DIM, HGRN_DIM), F32)
    grid_spec = pltpu.PrefetchScalarGridSpec(
        num_scalar_prefetch=1,
        grid=(batch, nt),
        in_specs=[fwd(0), fwd(1), fwd(3), bwd(0), bwd(2), bwd(3), lb_spec, lb_spec],
        out_specs=[
            pl.BlockSpec((tile, HGRN_WIDTH), lambda b, i, flag: (b * nt + i, 0)),
            pl.BlockSpec((tile, HGRN_WIDTH), lambda b, i, flag: (b * nt + nt - 1 - i, 0)),
        ],
        scratch_shapes=[state, state],
    )
    return pl.pallas_call(
        _hgrn_kernel,
        grid_spec=grid_spec,
        out_shape=[out_shape, out_shape],
        compiler_params=pltpu.CompilerParams(
            dimension_semantics=("arbitrary", "arbitrary"),
            vmem_limit_bytes=V7X_VMEM_LIMIT_BYTES),
    )(bounded, proj, proj, proj, proj, proj, proj, lb_f.reshape(1, -1), lb_b.reshape(1, -1))


ATTN_HALF = 64
ATTN_QT = 128
LANES = 128
ATTN_SLABS = ATTN_WIDTH // LANES


def _head_rms(xs, w):
    lo = lax.broadcasted_iota(jnp.int32, (1, LANES), 1) < ATTN_HEAD_DIM
    sq = xs * xs
    s_lo = jnp.sum(jnp.where(lo, sq, 0.0), axis=-1, keepdims=True)
    s_hi = jnp.sum(jnp.where(lo, 0.0, sq), axis=-1, keepdims=True)
    ms = jnp.where(lo, s_lo, s_hi) * (1.0 / ATTN_HEAD_DIM)
    return xs * lax.rsqrt(ms + NORM_EPS) * w


def _attn_kernel(*refs, seq):
    n_pat = len(DILATED_PATTERNS)
    ns = ATTN_SLABS
    q_refs, k_refs, kp_refs, kn_refs, v_refs, vp_refs, vn_refs = [refs[g * ns:(g + 1) * ns] for g in range(7)]
    bias_refs = refs[7 * ns:7 * ns + n_pat]
    o_ref, kwin, vwin, qn, s_scr, o_scr, l_scr = refs[7 * ns + n_pat:]
    tile = o_ref.shape[0]
    i = pl.program_id(1)
    lo = lax.broadcasted_iota(jnp.int32, (1, LANES), 1) < ATTN_HEAD_DIM

    for pi, ((_, dil), bias_ref) in enumerate(zip(DILATED_PATTERNS, bias_refs)):
        sub_rows = tile // dil
        qt = min(ATTN_QT, sub_rows)
        kt = qt + 2 * ATTN_HALF
        halo = ATTN_HALF * dil
        kcol = lax.broadcasted_iota(jnp.int32, (1, kt), 1)

        n_sub = sub_rows // qt

        def fill(r, ws, dil=dil, sub_rows=sub_rows, halo=halo):
            def rows_of(ref, start, n):
                return ref[pl.ds(start + r, n, stride=dil), :]

            main = slice(ATTN_HALF, ATTN_HALF + sub_rows)
            after = slice(ATTN_HALF + sub_rows, 2 * ATTN_HALF + sub_rows)
            for sl in range(ns):
                w = ws * ns + sl
                kwin[w, 0:ATTN_HALF, :] = rows_of(kp_refs[sl], tile - halo, ATTN_HALF).astype(BF16)
                kwin[w, main, :] = rows_of(k_refs[sl], 0, sub_rows).astype(BF16)
                kwin[w, after, :] = rows_of(kn_refs[sl], 0, ATTN_HALF).astype(BF16)
                vwin[w, 0:ATTN_HALF, :] = rows_of(vp_refs[sl], tile - halo, ATTN_HALF).astype(BF16)
                vwin[w, main, :] = rows_of(v_refs[sl], 0, sub_rows).astype(BF16)
                vwin[w, after, :] = rows_of(vn_refs[sl], 0, ATTN_HALF).astype(BF16)
                qn[w, 0:sub_rows, :] = rows_of(q_refs[sl], 0, sub_rows).astype(BF16)

        def sub(j, r, ws, ss, pi=pi, dil=dil, bias_ref=bias_ref, qt=qt, kt=kt, kcol=kcol):
            r0 = j * qt if isinstance(j, int) else pl.multiple_of(j * qt, qt)
            lk = (i * tile) // dil + j * qt - ATTN_HALF + kcol
            edge = jnp.where((lk >= 0) & (lk < seq // dil), 0.0, NEG_BIG)
            out_rows = pl.ds(r + j * (qt * dil), qt, stride=dil)
            for sl in range(ns):
                qs = qn[ws * ns + sl, pl.ds(r0, qt), :]
                ks = kwin[ws * ns + sl, pl.ds(r0, kt), :]
                for hh in range(2):
                    qh = jnp.where(lo if hh == 0 else jnp.logical_not(lo), qs, jnp.zeros_like(qs))
                    s_scr[ss * ATTN_HEADS + 2 * sl + hh, 0:qt, 0:kt] = (
                        _dot_nt(qh, ks) + bias_ref[2 * sl + hh] + edge)
            for sl in range(ns):
                vs = vwin[ws * ns + sl, pl.ds(r0, kt), :]
                o_slab = l_slab = None
                for hh in range(2):
                    s = s_scr[ss * ATTN_HEADS + 2 * sl + hh, 0:qt, 0:kt]
                    m = jnp.max(s, axis=-1, keepdims=True)
                    p = jnp.exp(s - m)
                    den = jnp.sum(p, axis=-1, keepdims=True)
                    oh = _dot(p.astype(BF16), vs) * (1.0 / den)
                    lse = m + jnp.log(den)
                    o_slab = oh if hh == 0 else jnp.where(lo, o_slab, oh)
                    l_slab = lse if hh == 0 else jnp.where(lo, l_slab, lse)
                o_scr[pi * ns + sl, out_rows, :] = o_slab
                l_scr[pi * ns + sl, out_rows, :] = l_slab

        if n_sub >= 2:
            assert n_sub % 2 == 0

            def subsequence(r, carry, fill=fill, sub=sub, n_sub=n_sub):
                fill(r, 0)
                if n_sub == 2:
                    sub(0, r, 0, 0)
                    sub(1, r, 0, 1)
                else:
                    def pair(jj, carry2):
                        sub(2 * jj, r, 0, 0)
                        sub(2 * jj + 1, r, 0, 1)
                        return carry2
                    lax.fori_loop(0, n_sub // 2, pair, 0)
                return carry

            lax.fori_loop(0, dil, subsequence, 0)
        else:
            assert dil % 2 == 0

            def subsequence_pair(rp, carry, fill=fill, sub=sub):
                fill(2 * rp, 0)
                fill(2 * rp + 1, 1)
                sub(0, 2 * rp, 0, 0)
                sub(0, 2 * rp + 1, 1, 1)
                return carry

            lax.fori_loop(0, dil // 2, subsequence_pair, 0)

    def merge(c, carry):
        rows = pl.ds(pl.multiple_of(c * ATTN_QT, ATTN_QT), ATTN_QT)
        for sl in range(ns):
            ls = [l_scr[p * ns + sl, rows, :] for p in range(n_pat)]
            mx = functools.reduce(jnp.maximum, ls)
            ws = [jnp.exp(l - mx) for l in ls]
            num = sum(w * o_scr[p * ns + sl, rows, :] for p, w in enumerate(ws))
            o_ref[rows, sl * LANES:(sl + 1) * LANES] = num / sum(ws)
        return carry

    lax.fori_loop(0, tile // ATTN_QT, merge, 0)


def _attn_bias(dilation, qt):
    slopes = jnp.exp2(-ALIBI_MAX_BIAS * jnp.arange(1, ATTN_HEADS + 1, dtype=F32) / ATTN_HEADS)
    t = jnp.arange(qt)[:, None]
    j = jnp.arange(qt + 2 * ATTN_HALF)[None, :]
    dist = jnp.abs(j - ATTN_HALF - t)
    alibi = -slopes[:, None, None] * (dilation * dist).astype(F32)[None]
    return jnp.where((dist <= ATTN_HALF)[None], alibi, NEG_BIG)


def attention(proj, *, batch, seq, tile=1024):
    nt = seq // tile
    n_pat = len(DILATED_PATTERNS)
    assert all(tile % (ATTN_HALF * dil) == 0 for _, dil in DILATED_PATTERNS)

    def slabs(c, shift):
        def spec(sl):
            def index(b, i):
                return (b * nt + jnp.clip(i + shift, 0, nt - 1), c * ATTN_SLABS + sl)
            return pl.BlockSpec((tile, LANES), index)
        return [spec(sl) for sl in range(ATTN_SLABS)]

    groups = [(0, 0), (1, 0), (1, -1), (1, 1), (2, 0), (2, -1), (2, 1)]
    biases = [_attn_bias(dil, min(ATTN_QT, tile // dil)) for _, dil in DILATED_PATTERNS]
    bias_specs = [pl.BlockSpec(bias.shape, lambda b, i: (0, 0, 0)) for bias in biases]
    return pl.pallas_call(
        functools.partial(_attn_kernel, seq=seq),
        grid=(batch, nt),
        in_specs=[s for c, shift in groups for s in slabs(c, shift)] + bias_specs,
        out_specs=pl.BlockSpec((tile, ATTN_WIDTH), lambda b, i: (b * nt + i, 0)),
        out_shape=jax.ShapeDtypeStruct((batch * seq, ATTN_WIDTH), F32),
        scratch_shapes=[pltpu.VMEM((2 * ATTN_SLABS, tile + 2 * ATTN_HALF, LANES), BF16),
                        pltpu.VMEM((2 * ATTN_SLABS, tile + 2 * ATTN_HALF, LANES), BF16),
                        pltpu.VMEM((2 * ATTN_SLABS, tile, LANES), BF16),
                        pltpu.VMEM((2 * ATTN_HEADS, ATTN_QT, ATTN_QT + 2 * ATTN_HALF), F32),
                        pltpu.VMEM((n_pat * ATTN_SLABS, tile, LANES), F32),
                        pltpu.VMEM((n_pat * ATTN_SLABS, tile, LANES), F32)],
        compiler_params=pltpu.CompilerParams(
            dimension_semantics=("arbitrary", "arbitrary"), vmem_limit_bytes=V7X_VMEM_LIMIT_BYTES),
    )(*([proj] * (len(groups) * ATTN_SLABS)), *biases)


def _split2(x):
    hi = x.astype(BF16)
    return hi, (x - hi.astype(F32)).astype(BF16)


def _out_proj_kernel(x_ref, a_ref, of_ref, ob_ref, hg_ref, hw_ref, wo32_ref, n2_ref, wr_hi_ref, wr_lo_ref,
                     x1_ref, h2_ref, aff_ref, wo_ref):
    @pl.when(pl.program_id(0) == 0)
    def _():
        _cast_chunks(wo32_ref, wo_ref)

    a_out = a_ref[...]
    o = of_ref[...] + ob_ref[...]
    hg = hg_ref[...]
    hw = hw_ref[...]
    b_parts = []
    for sl in range(HGRN_HEADS):
        cols = slice(sl * HGRN_DIM, (sl + 1) * HGRN_DIM)
        os_ = o[:, cols]
        y = os_ * lax.rsqrt(jnp.mean(os_ * os_, axis=-1, keepdims=True) + NORM_EPS) * hw
        g = hg[:, cols]
        b_parts.append(y * (g * jax.nn.sigmoid(g)))
    mixed = jnp.concatenate([a_out] + b_parts, axis=-1).astype(BF16)

    x1 = x_ref[...] + _dot(mixed, wo_ref[...])
    x1_ref[...] = x1
    h2 = x1 * lax.rsqrt(jnp.mean(x1 * x1, axis=-1, keepdims=True) + NORM_EPS) * n2_ref[...]
    d = x1.shape[1]
    h_hi, h_lo = _split2(h2)
    h2_ref[:, 0:d] = h_hi

    wr_hi = wr_hi_ref[...]
    logits = _dot(h_hi, wr_hi) + _dot(h_lo, wr_hi) + _dot(h_hi, wr_lo_ref[...])
    valid = lax.broadcasted_iota(jnp.int32, (1, LANES), 1) < N_EXPERTS
    logits = jnp.where(valid, logits, NEG_BIG)
    ex = jnp.exp(logits - jnp.max(logits, axis=-1, keepdims=True))
    aff = ex / jnp.sum(ex, axis=-1, keepdims=True)
    aff_ref[...] = aff
    a_hi, a_lo = _split2(aff)
    h2_ref[:, d:d + LANES] = a_hi
    h2_ref[:, d + LANES:d + 2 * LANES] = a_lo


def out_proj(x2d, a_out, o_f, o_b, proj, hgrn_norm_w, w_out, norm2_w, w_router, *, hg_col, tm=256):
    m, d = x2d.shape
    wr = jnp.pad(w_router, ((0, 0), (0, LANES - N_EXPERTS)))
    wr_hi, wr_lo = _split2(wr)

    def rows(width, colblock=0):
        return pl.BlockSpec((tm, width), lambda i: (i, colblock))

    def const(shape):
        return pl.BlockSpec(shape, lambda i: (0, 0))

    return pl.pallas_call(
        _out_proj_kernel,
        grid=(m // tm,),
        in_specs=[rows(d), rows(ATTN_WIDTH),
                  rows(HGRN_WIDTH), rows(HGRN_WIDTH), rows(HGRN_WIDTH, hg_col // HGRN_WIDTH),
                  const((1, HGRN_DIM)), const((ATTN_WIDTH + HGRN_WIDTH, d)),
                  const((1, d)), const((d, LANES)), const((d, LANES))],
        out_specs=[rows(d), rows(d + 2 * LANES), rows(LANES)],
        out_shape=[jax.ShapeDtypeStruct((m, d), F32), jax.ShapeDtypeStruct((m, d + 2 * LANES), BF16),
                   jax.ShapeDtypeStruct((m, LANES), F32)],
        scratch_shapes=[pltpu.VMEM((ATTN_WIDTH + HGRN_WIDTH, d), BF16)],
        compiler_params=pltpu.CompilerParams(
            dimension_semantics=("arbitrary",), vmem_limit_bytes=V7X_VMEM_LIMIT_BYTES),
    )(x2d, a_out, o_f, o_b, proj, hgrn_norm_w.reshape(1, -1), w_out,
      norm2_w.reshape(1, -1), wr_hi, wr_lo)


ROUTE_BLOCK = 128
COUNT_ROWS = 512


def _routing_kernel(aff_ref, slot_ref, start_ref, *, cap):
    seq = aff_ref.shape[0]

    def count(pred):
        def body(c, acc):
            blk = aff_ref[pl.ds(pl.multiple_of(c * COUNT_ROWS, COUNT_ROWS), COUNT_ROWS), :]
            hits = jnp.where(pred(blk), 1, 0).reshape(COUNT_ROWS // 8, 8, LANES)
            return acc + jnp.sum(hits, axis=0)
        acc = lax.fori_loop(0, seq // COUNT_ROWS, body, jnp.zeros((8, LANES), jnp.int32), unroll=4)
        return jnp.sum(acc, axis=0, keepdims=True)

    def bit_step(t, thr_bits):
        cand = thr_bits | jnp.left_shift(jnp.int32(1), 30 - t)
        cand_f = pltpu.bitcast(cand, F32)
        return jnp.where(count(lambda blk: blk >= cand_f) >= cap, cand, thr_bits)

    thr = pltpu.bitcast(lax.fori_loop(0, 31, bit_step, jnp.zeros((1, LANES), jnp.int32)), F32)
    need = (cap - count(lambda blk: blk > thr)).astype(F32)

    row = lax.broadcasted_iota(jnp.int32, (ROUTE_BLOCK, ROUTE_BLOCK), 0)
    col = lax.broadcasted_iota(jnp.int32, (ROUTE_BLOCK, ROUTE_BLOCK), 1)
    before = jnp.where(col < row, 1.0, 0.0).astype(BF16)

    def assign(j, carry):
        c_eq, c_sel = carry
        r0 = pl.multiple_of(j * ROUTE_BLOCK, ROUTE_BLOCK)
        blk = aff_ref[pl.ds(r0, ROUTE_BLOCK), :]
        gt, eq = blk > thr, blk == thr
        eq_f = jnp.where(eq, 1.0, 0.0)
        eq_rank = _dot(before, eq_f.astype(BF16)) + c_eq
        sel = gt | (eq & (eq_rank < need))
        sel_f = jnp.where(sel, 1.0, 0.0)
        rank = _dot(before, sel_f.astype(BF16)) + c_sel
        slot_ref[pl.ds(r0, ROUTE_BLOCK), :] = jnp.where(sel, rank, -1.0).astype(jnp.int32)
        start_ref[pl.ds(j, 1), :] = c_sel.astype(jnp.int32)
        return (c_eq + jnp.sum(eq_f, axis=0, keepdims=True), c_sel + jnp.sum(sel_f, axis=0, keepdims=True))

    zero = jnp.zeros((1, LANES), F32)
    lax.fori_loop(0, seq // ROUTE_BLOCK, assign, (zero, zero))


def routing(aff, *, batch, seq, cap):
    nblk = seq // ROUTE_BLOCK
    return pl.pallas_call(
        functools.partial(_routing_kernel, cap=cap),
        grid=(batch,),
        in_specs=[pl.BlockSpec((seq, LANES), lambda b: (b, 0))],
        out_specs=[pl.BlockSpec((seq, LANES), lambda b: (b, 0)), pl.BlockSpec((nblk, LANES), lambda b: (b, 0))],
        out_shape=[jax.ShapeDtypeStruct((batch * seq, LANES), jnp.int32),
                   jax.ShapeDtypeStruct((batch * nblk, LANES), jnp.int32)],
        compiler_params=pltpu.CompilerParams(
            dimension_semantics=("arbitrary",), vmem_limit_bytes=V7X_VMEM_LIMIT_BYTES),
    )(aff)


SLOT_ALIGN = 16
SLOT_WIN = ROUTE_BLOCK + SLOT_ALIGN
SLOT_WIN_SMALL = 64
FFN_ROWS = 256


def _one_hots(slot_ref, tok, bases, win):
    r = lax.broadcasted_iota(jnp.int32, (win, ROUTE_BLOCK), 0)
    return jnp.concatenate(
        [jnp.where((slot_ref[0, e, :, tok] - base) == r, 1.0, 0.0).astype(BF16) for e, base in enumerate(bases)],
        axis=0)


def _block_windows(base_ref, fits_ref, j, n_blocks):
    b = pl.program_id(0)
    n_exp = N_EXPERTS
    bases = [pl.multiple_of(base_ref[(b * n_exp + e) * n_blocks + j], SLOT_ALIGN) for e in range(n_exp)]
    return bases, fits_ref[b * n_blocks + j] == 1


def _gather_kernel(base_ref, fits_ref, slot_ref, h_ref, xin_ref, *, n_blocks):
    xin_ref[...] = jnp.zeros_like(xin_ref)

    def block(j, carry):
        tok = pl.ds(pl.multiple_of(j * ROUTE_BLOCK, ROUTE_BLOCK), ROUTE_BLOCK)
        bases, fits = _block_windows(base_ref, fits_ref, j, n_blocks)

        def run(win):
            rows = _dot(_one_hots(slot_ref, tok, bases, win), h_ref[tok, :]).astype(BF16)
            for e, base in enumerate(bases):
                xin_ref[0, e, pl.ds(base, win), :] += rows[e * win:(e + 1) * win]

        pl.when(fits)(functools.partial(run, SLOT_WIN_SMALL))
        pl.when(jnp.logical_not(fits))(functools.partial(run, SLOT_WIN))
        return carry

    lax.fori_loop(0, n_blocks, block, 0)


def gather(base, fits, slot_t, h_ext, *, batch, seq, cap, tn=256):
    n_exp = slot_t.shape[1]
    width = h_ext.shape[1]
    rows = cap + SLOT_WIN
    grid_spec = pltpu.PrefetchScalarGridSpec(
        num_scalar_prefetch=2,
        grid=(batch, width // tn),
        in_specs=[pl.BlockSpec((1, n_exp, 1, seq), lambda b, n, base, fits: (b, 0, 0, 0)),
                  pl.BlockSpec((seq, tn), lambda b, n, base, fits: (b, n))],
        out_specs=pl.BlockSpec((1, n_exp, rows, tn), lambda b, n, base, fits: (b, 0, 0, n)),
    )
    return pl.pallas_call(
        functools.partial(_gather_kernel, n_blocks=seq // ROUTE_BLOCK),
        grid_spec=grid_spec,
        out_shape=jax.ShapeDtypeStruct((batch, n_exp, rows, width), BF16),
        compiler_params=pltpu.CompilerParams(
            dimension_semantics=("arbitrary", "arbitrary"), vmem_limit_bytes=V7X_VMEM_LIMIT_BYTES),
    )(base, fits, slot_t, h_ext)


def _ffn_kernel(xin_ref, g_ref, wg32_ref, wu32_ref, wd32_ref, y_ref, wg_ref, wu_ref, wd_ref, *, cap):
    e = pl.program_id(0)

    @pl.when(pl.program_id(1) == 0)
    def _():
        _cast_chunks(wg32_ref.at[0], wg_ref)
        _cast_chunks(wu32_ref.at[0], wu_ref)
        _cast_chunks(wd32_ref.at[0], wd_ref)

    lane = lax.broadcasted_iota(jnp.int32, (1, 2 * LANES), 1)
    mine = (lane == e) | (lane == LANES + e)
    for rb in range(cap // FFN_ROWS):
        rows = slice(rb * FFN_ROWS, (rb + 1) * FFN_ROWS)
        xb = xin_ref[0, 0, rows, :]
        gate = jnp.sum(jnp.where(mine, g_ref[0, 0, rows, :].astype(F32), 0.0), axis=-1, keepdims=True)
        gate_h = _dot(xb, wg_ref[...])
        hid = (gate_h * jax.nn.sigmoid(gate_h)) * _dot(xb, wu_ref[...])
        y_ref[0, 0, rows, :] = (_dot(hid.astype(BF16), wd_ref[...]) * gate).astype(BF16)
    y_ref[0, 0, cap:, :] = jnp.zeros((y_ref.shape[2] - cap, y_ref.shape[3]), BF16)


def expert_ffn(xin, wg, wu, wd, *, cap):
    batch, n_exp, rows, width = xin.shape
    _, d, f = wg.shape
    return pl.pallas_call(
        functools.partial(_ffn_kernel, cap=cap),
        grid=(n_exp, batch),
        in_specs=[pl.BlockSpec((1, 1, cap, d), lambda e, b: (b, e, 0, 0)),
                  pl.BlockSpec((1, 1, cap, 2 * LANES), lambda e, b: (b, e, 0, d // (2 * LANES))),
                  pl.BlockSpec((1, d, f), lambda e, b: (e, 0, 0)),
                  pl.BlockSpec((1, d, f), lambda e, b: (e, 0, 0)),
                  pl.BlockSpec((1, f, d), lambda e, b: (e, 0, 0))],
        out_specs=pl.BlockSpec((1, 1, rows, d), lambda e, b: (b, e, 0, 0)),
        out_shape=jax.ShapeDtypeStruct((batch, n_exp, rows, d), BF16),
        scratch_shapes=[pltpu.VMEM((d, f), BF16), pltpu.VMEM((d, f), BF16), pltpu.VMEM((f, d), BF16)],
        compiler_params=pltpu.CompilerParams(
            dimension_semantics=("arbitrary", "arbitrary"), vmem_limit_bytes=V7X_VMEM_LIMIT_BYTES),
    )(xin, xin, wg, wu, wd)


def _combine_kernel(base_ref, fits_ref, slot_ref, y_ref, x1_ref, out_ref, ywin, *, n_blocks):
    blocks_here = out_ref.shape[0] // ROUTE_BLOCK
    first = pl.program_id(2) * blocks_here

    def block(jj, carry):
        j = first + jj
        rows = pl.ds(pl.multiple_of(jj * ROUTE_BLOCK, ROUTE_BLOCK), ROUTE_BLOCK)
        tok = pl.ds(pl.multiple_of(j * ROUTE_BLOCK, ROUTE_BLOCK), ROUTE_BLOCK)
        bases, fits = _block_windows(base_ref, fits_ref, j, n_blocks)

        def run(win):
            for e, base in enumerate(bases):
                ywin[e * win:(e + 1) * win, :] = y_ref[0, e, pl.ds(base, win), :]
            hits = _one_hots(slot_ref, tok, bases, win)
            out_ref[rows, :] = x1_ref[rows, :] + _dot(hits, ywin[0:len(bases) * win, :], ((0,), (0,)))

        pl.when(fits)(functools.partial(run, SLOT_WIN_SMALL))
        pl.when(jnp.logical_not(fits))(functools.partial(run, SLOT_WIN))
        return carry

    lax.fori_loop(0, blocks_here, block, 0)


def combine(base, fits, slot_t, y, x1, *, batch, seq, tn=512, tt=2048):
    n_exp, rows, d = y.shape[1], y.shape[2], y.shape[3]
    grid_spec = pltpu.PrefetchScalarGridSpec(
        num_scalar_prefetch=2,
        grid=(batch, d // tn, seq // tt),
        in_specs=[pl.BlockSpec((1, n_exp, 1, seq), lambda b, n, t, base, fits: (b, 0, 0, 0)),
                  pl.BlockSpec((1, n_exp, rows, tn), lambda b, n, t, base, fits: (b, 0, 0, n),
                               pipeline_mode=pl.Buffered(1)),
                  pl.BlockSpec((tt, tn), lambda b, n, t, base, fits: (b * (seq // tt) + t, n))],
        out_specs=pl.BlockSpec((tt, tn), lambda b, n, t, base, fits: (b * (seq // tt) + t, n)),
        scratch_shapes=[pltpu.VMEM((n_exp * SLOT_WIN, tn), BF16)],
    )
    return pl.pallas_call(
        functools.partial(_combine_kernel, n_blocks=seq // ROUTE_BLOCK),
        grid_spec=grid_spec,
        out_shape=jax.ShapeDtypeStruct((batch * seq, d), F32),
        compiler_params=pltpu.CompilerParams(
            dimension_semantics=("arbitrary", "arbitrary", "arbitrary"),
            vmem_limit_bytes=V7X_VMEM_LIMIT_BYTES),
    )(base, fits, slot_t, y, x1)


def kernel(x, norm1_w, w_in, attn_q_norm_w, attn_k_norm_w, hgrn_lb_fwd, hgrn_lb_bwd, hgrn_out_norm_w,
           w_out, norm2_w, w_router, w_expert_gate, w_expert_up, w_expert_down):
    batch, seq, d_model = x.shape
    depth = w_in.shape[0]
    cap = max(1, CAPACITY_FACTOR * seq // N_EXPERTS)
    hgrn_col = 3 * ATTN_WIDTH
    lb_f_all = jnp.cumsum(jax.nn.softmax(hgrn_lb_fwd.astype(F32), axis=0), axis=0)
    lb_b_all = jnp.cumsum(jax.nn.softmax(hgrn_lb_bwd.astype(F32), axis=0), axis=0)

    x2d = x.reshape(batch * seq, d_model)
    for l in range(depth):
        proj = in_proj(x2d, norm1_w[l], w_in[l], attn_q_norm_w[l], attn_k_norm_w[l])
        o_f, o_b = hgrn(proj, lb_f_all[l], lb_b_all[l], batch=batch, seq=seq, col0=hgrn_col)
        a_out = attention(proj, batch=batch, seq=seq)
        x1, h_ext, aff = out_proj(x2d, a_out, o_f, o_b, proj, hgrn_out_norm_w[l], w_out[l],
                                  norm2_w[l], w_router[l], hg_col=hgrn_col + 4 * HGRN_WIDTH)
        slot, start = routing(aff, batch=batch, seq=seq, cap=cap)

        def expert_major(t):
            return t.reshape(batch, -1, LANES)[:, :, :N_EXPERTS].transpose(0, 2, 1)

        slot_t = expert_major(slot).reshape(batch, N_EXPERTS, 1, seq)
        start_t = expert_major(start)
        base_t = start_t // SLOT_ALIGN * SLOT_ALIGN
        end_t = jnp.concatenate([start_t[:, :, 1:], jnp.full((batch, N_EXPERTS, 1), cap, jnp.int32)], axis=2)
        fits = (jnp.max(end_t - base_t, axis=1) <= SLOT_WIN_SMALL).astype(jnp.int32).reshape(-1)
        base = base_t.reshape(-1)
        xin = gather(base, fits, slot_t, h_ext, batch=batch, seq=seq, cap=cap)
        y = expert_ffn(xin, w_expert_gate[l], w_expert_up[l], w_expert_down[l], cap=cap)
        x2d = combine(base, fits, slot_t, y, x1, batch=batch, seq=seq)
    return x2d.reshape(batch, seq, d_model)
```

```python
import functools

import jax
import jax.numpy as jnp
from jax import lax
from jax.experimental import pallas as pl
from jax.experimental.pallas import tpu as pltpu

F32 = jnp.float32
BF16 = jnp.bfloat16

NORM_EPS = 1e-6
NEG_BIG = -1e30
ATTN_HEAD_DIM = 64
ATTN_HEADS = 8
ATTN_WIDTH = ATTN_HEADS * ATTN_HEAD_DIM
DILATED_PATTERNS = ((128, 1), (512, 4), (2048, 16))
ALIBI_MAX_BIAS = 8.0
HGRN_DIM = 128
HGRN_HEADS = 4
HGRN_WIDTH = HGRN_HEADS * HGRN_DIM
N_EXPERTS = 16
CAPACITY_FACTOR = 2
V7X_VMEM_LIMIT_BYTES = 56 * 1024 * 1024


def _dot(a, b, dims=((1,), (0,))):
    return lax.dot_general(a, b, (dims, ((), ())), preferred_element_type=F32)


def _dot_nt(a, b):
    return _dot(a, b, ((1,), (1,)))


def _cast_chunks(src_ref, dst_ref, rows=256):
    for r0 in range(0, src_ref.shape[0], rows):
        dst_ref[r0:r0 + rows, :] = src_ref[r0:r0 + rows, :].astype(dst_ref.dtype)


def _in_proj_kernel(x_ref, nw_ref, w32_ref, qw_ref, kw_ref, o_ref, w_ref, *, n_chunk):
    @pl.when(pl.program_id(0) == 0)
    def _():
        _cast_chunks(w32_ref, w_ref)

    x = x_ref[...]
    h = x * lax.rsqrt(jnp.mean(x * x, axis=-1, keepdims=True) + NORM_EPS) * nw_ref[...]
    h = h.astype(BF16)
    n_total = o_ref.shape[1]
    for c in range(n_total // n_chunk):
        cols = slice(c * n_chunk, (c + 1) * n_chunk)
        o_ref[:, cols] = _dot(h, w_ref[:, cols])
    for sl in range(ATTN_WIDTH // LANES):
        cols = slice(sl * LANES, (sl + 1) * LANES)
        o_ref[:, cols] = _head_rms(o_ref[:, cols], qw_ref[:, cols]) * (ATTN_HEAD_DIM ** -0.5)
        kcols = slice(ATTN_WIDTH + sl * LANES, ATTN_WIDTH + (sl + 1) * LANES)
        o_ref[:, kcols] = _head_rms(o_ref[:, kcols], kw_ref[:, cols])


def in_proj(x2d, norm_w, w, qw, kw, *, tm=256, n_chunk=512):
    m, d = x2d.shape
    n = w.shape[1]
    wspec = pl.BlockSpec((1, ATTN_WIDTH), lambda i: (0, 0))
    return pl.pallas_call(
        functools.partial(_in_proj_kernel, n_chunk=n_chunk),
        grid=(m // tm,),
        in_specs=[
            pl.BlockSpec((tm, d), lambda i: (i, 0)),
            pl.BlockSpec((1, d), lambda i: (0, 0)),
            pl.BlockSpec((d, n), lambda i: (0, 0), pipeline_mode=pl.Buffered(1)),
            wspec, wspec,
        ],
        out_specs=pl.BlockSpec((tm, n), lambda i: (i, 0)),
        out_shape=jax.ShapeDtypeStruct((m, n), F32),
        scratch_shapes=[pltpu.VMEM((d, n), BF16)],
        compiler_params=pltpu.CompilerParams(
            dimension_semantics=("arbitrary",), vmem_limit_bytes=V7X_VMEM_LIMIT_BYTES),
    )(x2d, norm_w.reshape(1, d), w,
      jnp.tile(qw, ATTN_HEADS).reshape(1, -1), jnp.tile(kw, ATTN_HEADS).reshape(1, -1))


HGRN_CHUNK = 64
HGRN_SUB = 16
HGRN_MAX_LOG_DECAY = 80.0


def _split3(g):
    hi = g.astype(BF16)
    r1 = g - hi.astype(F32)
    mid = r1.astype(BF16)
    lo = (r1 - mid.astype(F32)).astype(BF16)
    return hi, mid, lo


def _hgrn_chunk(load, store, lb, st_ref, *, reverse, bounded_decay):
    q_raw, z, v = load()
    c, dk = q_raw.shape
    sub = HGRN_SUB
    q = q_raw * jax.nn.sigmoid(q_raw)
    ez = jnp.exp(-jnp.abs(z))
    inv = 1.0 / (1.0 + ez)
    pos = z >= 0.0
    f = lb + (1.0 - lb) * jnp.where(pos, inv, ez * inv)
    k = (1.0 - lb) * jnp.where(pos, ez * inv, inv)
    g = jnp.log(f)

    row = lax.broadcasted_iota(jnp.int32, (c, c), 0)
    col = lax.broadcasted_iota(jnp.int32, (c, c), 1)
    tri = (col >= row) if reverse else (col <= row)
    tri_bf = jnp.where(tri, 1.0, 0.0).astype(BF16)
    g_hi, g_mid, g_lo = _split3(g)
    yield
    cum = _dot(tri_bf, g_hi) + _dot(tri_bf, g_mid) + _dot(tri_bf, g_lo)
    yield

    def edge(r):
        return cum[r:r + 1, :]

    if reverse:
        ref_half, ref_q_lo, ref_q_hi, ref_end = edge(32), edge(16), edge(48), edge(0)
    else:
        ref_half, ref_q_lo, ref_q_hi, ref_end = edge(31), edge(15), edge(47), edge(c - 1)

    r1 = lax.broadcasted_iota(jnp.int32, (c, 1), 0)
    ref_quarter = jnp.where(r1 < 32, ref_q_lo, ref_q_hi)

    st = st_ref[...]
    v_bf = v.astype(BF16)
    tb, sb = row // sub, col // sub
    if reverse:
        m1 = (tb < 2) & (sb >= 2)
        m2 = ((tb == 0) & (sb == 1)) | ((tb == 2) & (sb == 3))
    else:
        m1 = (tb >= 2) & (sb < 2)
        m2 = ((tb == 1) & (sb == 0)) | ((tb == 3) & (sb == 2))
    if bounded_decay:
        first = (sub - 1) if reverse else 0
        blocks = [slice(b * sub, (b + 1) * sub) for b in range(c // sub)]
        refs_d = [edge(b * sub + first) for b in range(c // sub)]
        q3 = jnp.concatenate([q[rows] * jnp.exp(cum[rows] - d) for rows, d in zip(blocks, refs_d)], axis=0)
        k3 = jnp.concatenate([k[rows] * jnp.exp(d - cum[rows]) for rows, d in zip(blocks, refs_d)], axis=0)

        def scaled(x, factors):
            return jnp.concatenate([x[rows] * f for rows, f in zip(blocks, factors)], axis=0).astype(BF16)

        refs_q = [ref_q_lo, ref_q_lo, ref_q_hi, ref_q_hi]
        qe = scaled(q3, [jnp.exp(d) for d in refs_d])
        ke = scaled(k3, [jnp.exp(ref_end - d) for d in refs_d])
        q1 = scaled(q3, [jnp.exp(jnp.minimum(d - ref_half, 0.0)) for d in refs_d])
        k1 = scaled(k3, [jnp.exp(jnp.minimum(ref_half - d, 0.0)) for d in refs_d])
        q2 = scaled(q3, [jnp.exp(jnp.minimum(d - rq, 0.0)) for d, rq in zip(refs_d, refs_q)])
        k2 = scaled(k3, [jnp.exp(jnp.minimum(rq - d, 0.0)) for d, rq in zip(refs_d, refs_q)])
        q3, k3 = q3.astype(BF16), k3.astype(BF16)
        m3 = (tb == sb) & ((col >= row) if reverse else (col <= row))
    else:
        qe = (q * jnp.exp(cum)).astype(BF16)
        ke = (k * jnp.exp(ref_end - cum)).astype(BF16)
        q1 = (q * jnp.exp(jnp.minimum(cum - ref_half, 0.0))).astype(BF16)
        k1 = (k * jnp.exp(jnp.minimum(ref_half - cum, 0.0))).astype(BF16)
        q2 = (q * jnp.exp(jnp.minimum(cum - ref_quarter, 0.0))).astype(BF16)
        k2 = (k * jnp.exp(jnp.minimum(ref_quarter - cum, 0.0))).astype(BF16)
    yield

    o = _dot_nt(qe, st.astype(BF16))
    st_ref[...] = st * jnp.exp(ref_end) + _dot(v_bf, ke, ((0,), (0,)))
    a = jnp.where(m1, _dot_nt(q1, k1), 0.0) + jnp.where(m2, _dot_nt(q2, k2), 0.0)
    if bounded_decay:
        a = a + jnp.where(m3, _dot_nt(q3, k3), 0.0)
        a_bf = a.astype(BF16)
        yield
        store(o + _dot(a_bf, v_bf))
        return
    yield

    t_loc = lax.broadcasted_iota(jnp.int32, (sub, 1), 0)
    lane = lax.broadcasted_iota(jnp.int32, (sub, c), 1)
    blocks = []
    for blk in range(c // sub):
        rows = slice(blk * sub, (blk + 1) * sub)
        cum_b, q_b = cum[rows], q[rows]
        a_b = jnp.zeros((sub, c), F32)
        for s_loc in range(sub):
            s = blk * sub + s_loc
            keep = (t_loc <= s_loc) if reverse else (t_loc >= s_loc)
            e = jnp.exp(jnp.where(keep, cum_b - cum[s:s + 1, :], NEG_BIG))
            p = (q_b * k[s:s + 1, :]) * e
            a_b = jnp.where(lane == s, jnp.sum(p, axis=-1, keepdims=True), a_b)
        blocks.append(a_b)
    a = a + jnp.concatenate(blocks, axis=0)
    store(o + _dot(a.astype(BF16), v_bf))


def _hgrn_kernel(bounded_ref, qf_ref, zf_ref, vf_ref, qb_ref, zb_ref, vb_ref, lbf_ref, lbb_ref,
                 of_ref, ob_ref, sf_ref, sb_ref):
    @pl.when(pl.program_id(1) == 0)
    def _():
        sf_ref[...] = jnp.zeros_like(sf_ref)
        sb_ref[...] = jnp.zeros_like(sb_ref)

    n_chunks = qf_ref.shape[0] // HGRN_CHUNK

    def run(bounded_decay):
        def body(ci, carry):
            rf = pl.ds(pl.multiple_of(ci * HGRN_CHUNK, HGRN_CHUNK), HGRN_CHUNK)
            rb = pl.ds(pl.multiple_of((n_chunks - 1 - ci) * HGRN_CHUNK, HGRN_CHUNK), HGRN_CHUNK)
            chains = []
            for h in range(HGRN_HEADS):
                cols = slice(h * HGRN_DIM, (h + 1) * HGRN_DIM)
                for rows, q_ref, z_ref, v_ref, lb_ref, o_ref, st_ref, reverse in (
                        (rf, qf_ref, zf_ref, vf_ref, lbf_ref, of_ref, sf_ref, False),
                        (rb, qb_ref, zb_ref, vb_ref, lbb_ref, ob_ref, sb_ref, True)):
                    def load(rows=rows, cols=cols, q_ref=q_ref, z_ref=z_ref, v_ref=v_ref):
                        return q_ref[rows, cols], z_ref[rows, cols], v_ref[rows, cols]

                    def store(o, rows=rows, cols=cols, o_ref=o_ref):
                        o_ref[rows, cols] = o

                    chains.append(_hgrn_chunk(load, store, lb_ref[:, cols], st_ref.at[h],
                                              reverse=reverse, bounded_decay=bounded_decay))
            while chains:
                chains = [ch for ch in chains if next(ch, True) is None]
            return carry

        lax.fori_loop(0, n_chunks, body, 0)

    pl.when(bounded_ref[0] == 1)(functools.partial(run, True))
    pl.when(bounded_ref[0] != 1)(functools.partial(run, False))


def hgrn(proj, lb_f, lb_b, *, batch, seq, col0, tile=512):
    nt = seq // tile
    cb = col0 // HGRN_WIDTH
    worst = -(HGRN_SUB - 1) * jnp.log(jnp.minimum(jnp.min(lb_f), jnp.min(lb_b)))
    bounded = (worst <= HGRN_MAX_LOG_DECAY).astype(jnp.int32).reshape(1)

    def fwd(colblock):
        return pl.BlockSpec((tile, HGRN_WIDTH), lambda b, i, flag: (b * nt + i, cb + colblock))

    def bwd(colblock):
        return pl.BlockSpec((tile, HGRN_WIDTH), lambda b, i, flag: (b * nt + nt - 1 - i, cb + colblock))

    lb_spec = pl.BlockSpec((1, HGRN_WIDTH), lambda b, i, flag: (0, 0))
    out_shape = jax.ShapeDtypeStruct((batch * seq, HGRN_WIDTH), F32)
    state = pltpu.VMEM((HGRN_HEADS, HGRN_DIM, HGRN_DIM), F32)
    grid_spec = pltpu.PrefetchScalarGridSpec(
        num_scalar_prefetch=1,
        grid=(batch, nt),
        in_specs=[fwd(0), fwd(1), fwd(3), bwd(0), bwd(2), bwd(3), lb_spec, lb_spec],
        out_specs=[
            pl.BlockSpec((tile, HGRN_WIDTH), lambda b, i, flag: (b * nt + i, 0)),
            pl.BlockSpec((tile, HGRN_WIDTH), lambda b, i, flag: (b * nt + nt - 1 - i, 0)),
        ],
        scratch_shapes=[state, state],
    )
    return pl.pallas_call(
        _hgrn_kernel,
        grid_spec=grid_spec,
        out_shape=[out_shape, out_shape],
        compiler_params=pltpu.CompilerParams(
            dimension_semantics=("arbitrary", "arbitrary"),
            vmem_limit_bytes=V7X_VMEM_LIMIT_BYTES),
    )(bounded, proj, proj, proj, proj, proj, proj, lb_f.reshape(1, -1), lb_b.reshape(1, -1))


ATTN_HALF = 64
ATTN_QT = 128
LANES = 128
ATTN_SLABS = ATTN_WIDTH // LANES


def _head_rms(xs, w):
    lo = lax.broadcasted_iota(jnp.int32, (1, LANES), 1) < ATTN_HEAD_DIM
    sq = xs * xs
    s_lo = jnp.sum(jnp.where(lo, sq, 0.0), axis=-1, keepdims=True)
    s_hi = jnp.sum(jnp.where(lo, 0.0, sq), axis=-1, keepdims=True)
    ms = jnp.where(lo, s_lo, s_hi) * (1.0 / ATTN_HEAD_DIM)
    return xs * lax.rsqrt(ms + NORM_EPS) * w


def _attn_kernel(*refs, seq):
    n_pat = len(DILATED_PATTERNS)
    ns = ATTN_SLABS
    q_refs, k_refs, kp_refs, kn_refs, v_refs, vp_refs, vn_refs = [refs[g * ns:(g + 1) * ns] for g in range(7)]
    bias_refs = refs[7 * ns:7 * ns + n_pat]
    o_ref, kwin, vwin, qn, s_scr, o_scr, l_scr = refs[7 * ns + n_pat:]
    tile = o_ref.shape[0]
    i = pl.program_id(1)
    lo = lax.broadcasted_iota(jnp.int32, (1, LANES), 1) < ATTN_HEAD_DIM

    for pi, ((_, dil), bias_ref) in enumerate(zip(DILATED_PATTERNS, bias_refs)):
        sub_rows = tile // dil
        qt = min(ATTN_QT, sub_rows)
        kt = qt + 2 * ATTN_HALF
        halo = ATTN_HALF * dil
        kcol = lax.broadcasted_iota(jnp.int32, (1, kt), 1)

        n_sub = sub_rows // qt

        def fill(r, ws, dil=dil, sub_rows=sub_rows, halo=halo):
            def rows_of(ref, start, n):
                return ref[pl.ds(start + r, n, stride=dil), :]

            main = slice(ATTN_HALF, ATTN_HALF + sub_rows)
            after = slice(ATTN_HALF + sub_rows, 2 * ATTN_HALF + sub_rows)
            for sl in range(ns):
                w = ws * ns + sl
                kwin[w, 0:ATTN_HALF, :] = rows_of(kp_refs[sl], tile - halo, ATTN_HALF).astype(BF16)
                kwin[w, main, :] = rows_of(k_refs[sl], 0, sub_rows).astype(BF16)
                kwin[w, after, :] = rows_of(kn_refs[sl], 0, ATTN_HALF).astype(BF16)
                vwin[w, 0:ATTN_HALF, :] = rows_of(vp_refs[sl], tile - halo, ATTN_HALF).astype(BF16)
                vwin[w, main, :] = rows_of(v_refs[sl], 0, sub_rows).astype(BF16)
                vwin[w, after, :] = rows_of(vn_refs[sl], 0, ATTN_HALF).astype(BF16)
                qn[w, 0:sub_rows, :] = rows_of(q_refs[sl], 0, sub_rows).astype(BF16)

        def sub(j, r, ws, ss, pi=pi, dil=dil, bias_ref=bias_ref, qt=qt, kt=kt, kcol=kcol):
            r0 = j * qt if isinstance(j, int) else pl.multiple_of(j * qt, qt)
            lk = (i * tile) // dil + j * qt - ATTN_HALF + kcol
            edge = jnp.where((lk >= 0) & (lk < seq // dil), 0.0, NEG_BIG)
            out_rows = pl.ds(r + j * (qt * dil), qt, stride=dil)
            for sl in range(ns):
                qs = qn[ws * ns + sl, pl.ds(r0, qt), :]
                ks = kwin[ws * ns + sl, pl.ds(r0, kt), :]
                for hh in range(2):
                    qh = jnp.where(lo if hh == 0 else jnp.logical_not(lo), qs, jnp.zeros_like(qs))
                    s_scr[ss * ATTN_HEADS + 2 * sl + hh, 0:qt, 0:kt] = (
                        _dot_nt(qh, ks) + bias_ref[2 * sl + hh] + edge)
            for sl in range(ns):
                vs = vwin[ws * ns + sl, pl.ds(r0, kt), :]
                o_slab = l_slab = None
                for hh in range(2):
                    s = s_scr[ss * ATTN_HEADS + 2 * sl + hh, 0:qt, 0:kt]
                    m = jnp.max(s, axis=-1, keepdims=True)
                    p = jnp.exp(s - m)
                    den = jnp.sum(p, axis=-1, keepdims=True)
                    oh = _dot(p.astype(BF16), vs) * (1.0 / den)
                    lse = m + jnp.log(den)
                    o_slab = oh if hh == 0 else jnp.where(lo, o_slab, oh)
                    l_slab = lse if hh == 0 else jnp.where(lo, l_slab, lse)
                o_scr[pi * ns + sl, out_rows, :] = o_slab
                l_scr[pi * ns + sl, out_rows, :] = l_slab

        if n_sub >= 2:
            assert n_sub % 2 == 0

            def subsequence(r, carry, fill=fill, sub=sub, n_sub=n_sub):
                fill(r, 0)
                if n_sub == 2:
                    sub(0, r, 0, 0)
                    sub(1, r, 0, 1)
                else:
                    def pair(jj, carry2):
                        sub(2 * jj, r, 0, 0)
                        sub(2 * jj + 1, r, 0, 1)
                        return carry2
                    lax.fori_loop(0, n_sub // 2, pair, 0)
                return carry

            lax.fori_loop(0, dil, subsequence, 0)
        else:
            assert dil % 2 == 0

            def subsequence_pair(rp, carry, fill=fill, sub=sub):
                fill(2 * rp, 0)
                fill(2 * rp + 1, 1)
                sub(0, 2 * rp, 0, 0)
                sub(0, 2 * rp + 1, 1, 1)
                return carry

            lax.fori_loop(0, dil // 2, subsequence_pair, 0)

    def merge(c, carry):
        rows = pl.ds(pl.multiple_of(c * ATTN_QT, ATTN_QT), ATTN_QT)
        for sl in range(ns):
            ls = [l_scr[p * ns + sl, rows, :] for p in range(n_pat)]
            mx = functools.reduce(jnp.maximum, ls)
            ws = [jnp.exp(l - mx) for l in ls]
            num = sum(w * o_scr[p * ns + sl, rows, :] for p, w in enumerate(ws))
            o_ref[rows, sl * LANES:(sl + 1) * LANES] = num / sum(ws)
        return carry

    lax.fori_loop(0, tile // ATTN_QT, merge, 0)


def _attn_bias(dilation, qt):
    slopes = jnp.exp2(-ALIBI_MAX_BIAS * jnp.arange(1, ATTN_HEADS + 1, dtype=F32) / ATTN_HEADS)
    t = jnp.arange(qt)[:, None]
    j = jnp.arange(qt + 2 * ATTN_HALF)[None, :]
    dist = jnp.abs(j - ATTN_HALF - t)
    alibi = -slopes[:, None, None] * (dilation * dist).astype(F32)[None]
    return jnp.where((dist <= ATTN_HALF)[None], alibi, NEG_BIG)


def attention(proj, *, batch, seq, tile=1024):
    nt = seq // tile
    n_pat = len(DILATED_PATTERNS)
    assert all(tile % (ATTN_HALF * dil) == 0 for _, dil in DILATED_PATTERNS)

    def slabs(c, shift):
        def spec(sl):
            def index(b, i):
                return (b * nt + jnp.clip(i + shift, 0, nt - 1), c * ATTN_SLABS + sl)
            return pl.BlockSpec((tile, LANES), index)
        return [spec(sl) for sl in range(ATTN_SLABS)]

    groups = [(0, 0), (1, 0), (1, -1), (1, 1), (2, 0), (2, -1), (2, 1)]
    biases = [_attn_bias(dil, min(ATTN_QT, tile // dil)) for _, dil in DILATED_PATTERNS]
    bias_specs = [pl.BlockSpec(bias.shape, lambda b, i: (0, 0, 0)) for bias in biases]
    return pl.pallas_call(
        functools.partial(_attn_kernel, seq=seq),
        grid=(batch, nt),
        in_specs=[s for c, shift in groups for s in slabs(c, shift)] + bias_specs,
        out_specs=pl.BlockSpec((tile, ATTN_WIDTH), lambda b, i: (b * nt + i, 0)),
        out_shape=jax.ShapeDtypeStruct((batch * seq, ATTN_WIDTH), F32),
        scratch_shapes=[pltpu.VMEM((2 * ATTN_SLABS, tile + 2 * ATTN_HALF, LANES), BF16),
                        pltpu.VMEM((2 * ATTN_SLABS, tile + 2 * ATTN_HALF, LANES), BF16),
                        pltpu.VMEM((2 * ATTN_SLABS, tile, LANES), BF16),
                        pltpu.VMEM((2 * ATTN_HEADS, ATTN_QT, ATTN_QT + 2 * ATTN_HALF), F32),
                        pltpu.VMEM((n_pat * ATTN_SLABS, tile, LANES), F32),
                        pltpu.VMEM((n_pat * ATTN_SLABS, tile, LANES), F32)],
        compiler_params=pltpu.CompilerParams(
            dimension_semantics=("arbitrary", "arbitrary"), vmem_limit_bytes=V7X_VMEM_LIMIT_BYTES),
    )(*([proj] * (len(groups) * ATTN_SLABS)), *biases)


def _split2(x):
    hi = x.astype(BF16)
    return hi, (x - hi.astype(F32)).astype(BF16)


def _out_proj_kernel(x_ref, a_ref, of_ref, ob_ref, hg_ref, hw_ref, wo32_ref, n2_ref, wr_hi_ref, wr_lo_ref,
                     x1_ref, h2_ref, aff_ref, wo_ref):
    @pl.when(pl.program_id(0) == 0)
    def _():
        _cast_chunks(wo32_ref, wo_ref)

    a_out = a_ref[...]
    o = of_ref[...] + ob_ref[...]
    hg = hg_ref[...]
    hw = hw_ref[...]
    b_parts = []
    for sl in range(HGRN_HEADS):
        cols = slice(sl * HGRN_DIM, (sl + 1) * HGRN_DIM)
        os_ = o[:, cols]
        y = os_ * lax.rsqrt(jnp.mean(os_ * os_, axis=-1, keepdims=True) + NORM_EPS) * hw
        g = hg[:, cols]
        b_parts.append(y * (g * jax.nn.sigmoid(g)))
    mixed = jnp.concatenate([a_out] + b_parts, axis=-1).astype(BF16)

    x1 = x_ref[...] + _dot(mixed, wo_ref[...])
    x1_ref[...] = x1
    h2 = x1 * lax.rsqrt(jnp.mean(x1 * x1, axis=-1, keepdims=True) + NORM_EPS) * n2_ref[...]
    d = x1.shape[1]
    h_hi, h_lo = _split2(h2)
    h2_ref[:, 0:d] = h_hi

    wr_hi = wr_hi_ref[...]
    logits = _dot(h_hi, wr_hi) + _dot(h_lo, wr_hi) + _dot(h_hi, wr_lo_ref[...])
    valid = lax.broadcasted_iota(jnp.int32, (1, LANES), 1) < N_EXPERTS
    logits = jnp.where(valid, logits, NEG_BIG)
    ex = jnp.exp(logits - jnp.max(logits, axis=-1, keepdims=True))
    aff = ex / jnp.sum(ex, axis=-1, keepdims=True)
    aff_ref[...] = aff
    a_hi, a_lo = _split2(aff)
    h2_ref[:, d:d + LANES] = a_hi
    h2_ref[:, d + LANES:d + 2 * LANES] = a_lo


def out_proj(x2d, a_out, o_f, o_b, proj, hgrn_norm_w, w_out, norm2_w, w_router, *, hg_col, tm=256):
    m, d = x2d.shape
    wr = jnp.pad(w_router, ((0, 0), (0, LANES - N_EXPERTS)))
    wr_hi, wr_lo = _split2(wr)

    def rows(width, colblock=0):
        return pl.BlockSpec((tm, width), lambda i: (i, colblock))

    def const(shape):
        return pl.BlockSpec(shape, lambda i: (0, 0))

    return pl.pallas_call(
        _out_proj_kernel,
        grid=(m // tm,),
        in_specs=[rows(d), rows(ATTN_WIDTH),
                  rows(HGRN_WIDTH), rows(HGRN_WIDTH), rows(HGRN_WIDTH, hg_col // HGRN_WIDTH),
                  const((1, HGRN_DIM)), const((ATTN_WIDTH + HGRN_WIDTH, d)),
                  const((1, d)), const((d, LANES)), const((d, LANES))],
        out_specs=[rows(d), rows(d + 2 * LANES), rows(LANES)],
        out_shape=[jax.ShapeDtypeStruct((m, d), F32), jax.ShapeDtypeStruct((m, d + 2 * LANES), BF16),
                   jax.ShapeDtypeStruct((m, LANES), F32)],
        scratch_shapes=[pltpu.VMEM((ATTN_WIDTH + HGRN_WIDTH, d), BF16)],
        compiler_params=pltpu.CompilerParams(
            dimension_semantics=("arbitrary",), vmem_limit_bytes=V7X_VMEM_LIMIT_BYTES),
    )(x2d, a_out, o_f, o_b, proj, hgrn_norm_w.reshape(1, -1), w_out,
      norm2_w.reshape(1, -1), wr_hi, wr_lo)


ROUTE_BLOCK = 256
COUNT_ROWS = 512


def _routing_kernel(aff_ref, slot_ref, start_ref, *, cap):
    seq = aff_ref.shape[0]

    def count(pred):
        def body(c, acc):
            blk = aff_ref[pl.ds(pl.multiple_of(c * COUNT_ROWS, COUNT_ROWS), COUNT_ROWS), :]
            hits = jnp.where(pred(blk), 1, 0).reshape(COUNT_ROWS // 8, 8, LANES)
            return acc + jnp.sum(hits, axis=0)
        acc = lax.fori_loop(0, seq // COUNT_ROWS, body, jnp.zeros((8, LANES), jnp.int32), unroll=4)
        return jnp.sum(acc, axis=0, keepdims=True)

    def bit_step(t, thr_bits):
        cand = thr_bits | jnp.left_shift(jnp.int32(1), 30 - t)
        cand_f = pltpu.bitcast(cand, F32)
        return jnp.where(count(lambda blk: blk >= cand_f) >= cap, cand, thr_bits)

    thr = pltpu.bitcast(lax.fori_loop(0, 31, bit_step, jnp.zeros((1, LANES), jnp.int32)), F32)
    need = (cap - count(lambda blk: blk > thr)).astype(F32)

    row = lax.broadcasted_iota(jnp.int32, (ROUTE_BLOCK, ROUTE_BLOCK), 0)
    col = lax.broadcasted_iota(jnp.int32, (ROUTE_BLOCK, ROUTE_BLOCK), 1)
    before = jnp.where(col < row, 1.0, 0.0).astype(BF16)

    def assign(j, carry):
        c_eq, c_sel = carry
        r0 = pl.multiple_of(j * ROUTE_BLOCK, ROUTE_BLOCK)
        blk = aff_ref[pl.ds(r0, ROUTE_BLOCK), :]
        gt, eq = blk > thr, blk == thr
        eq_f = jnp.where(eq, 1.0, 0.0)
        eq_rank = _dot(before, eq_f.astype(BF16)) + c_eq
        sel = gt | (eq & (eq_rank < need))
        sel_f = jnp.where(sel, 1.0, 0.0)
        rank = _dot(before, sel_f.astype(BF16)) + c_sel
        slot_ref[pl.ds(r0, ROUTE_BLOCK), :] = jnp.where(sel, rank, -1.0).astype(jnp.int32)
        start_ref[pl.ds(j, 1), :] = c_sel.astype(jnp.int32)
        return (c_eq + jnp.sum(eq_f, axis=0, keepdims=True), c_sel + jnp.sum(sel_f, axis=0, keepdims=True))

    zero = jnp.zeros((1, LANES), F32)
    lax.fori_loop(0, seq // ROUTE_BLOCK, assign, (zero, zero))


def routing(aff, *, batch, seq, cap):
    nblk = seq // ROUTE_BLOCK
    return pl.pallas_call(
        functools.partial(_routing_kernel, cap=cap),
        grid=(batch,),
        in_specs=[pl.BlockSpec((seq, LANES), lambda b: (b, 0))],
        out_specs=[pl.BlockSpec((seq, LANES), lambda b: (b, 0)), pl.BlockSpec((nblk, LANES), lambda b: (b, 0))],
        out_shape=[jax.ShapeDtypeStruct((batch * seq, LANES), jnp.int32),
                   jax.ShapeDtypeStruct((batch * nblk, LANES), jnp.int32)],
        compiler_params=pltpu.CompilerParams(
            dimension_semantics=("arbitrary",), vmem_limit_bytes=V7X_VMEM_LIMIT_BYTES),
    )(aff)


SLOT_ALIGN = 16
SLOT_WIN = ROUTE_BLOCK + SLOT_ALIGN
SLOT_WIN_SMALL = 64
FFN_ROWS = 256


def _one_hots(slot_ref, tok, bases, win):
    r = lax.broadcasted_iota(jnp.int32, (win, ROUTE_BLOCK), 0)
    return jnp.concatenate(
        [jnp.where((slot_ref[0, e, :, tok] - base) == r, 1.0, 0.0).astype(BF16) for e, base in enumerate(bases)],
        axis=0)


def _block_windows(base_ref, fits_ref, j, n_blocks):
    b = pl.program_id(0)
    n_exp = N_EXPERTS
    bases = [pl.multiple_of(base_ref[(b * n_exp + e) * n_blocks + j], SLOT_ALIGN) for e in range(n_exp)]
    return bases, fits_ref[b * n_blocks + j] == 1


def _gather_kernel(base_ref, fits_ref, slot_ref, h_ref, xin_ref, *, n_blocks):
    xin_ref[...] = jnp.zeros_like(xin_ref)

    def block(j, carry):
        tok = pl.ds(pl.multiple_of(j * ROUTE_BLOCK, ROUTE_BLOCK), ROUTE_BLOCK)
        bases, fits = _block_windows(base_ref, fits_ref, j, n_blocks)

        def run(win):
            rows = _dot(_one_hots(slot_ref, tok, bases, win), h_ref[tok, :]).astype(BF16)
            for e, base in enumerate(bases):
                xin_ref[0, e, pl.ds(base, win), :] += rows[e * win:(e + 1) * win]

        pl.when(fits)(functools.partial(run, SLOT_WIN_SMALL))
        pl.when(jnp.logical_not(fits))(functools.partial(run, SLOT_WIN))
        return carry

    lax.fori_loop(0, n_blocks, block, 0)


def gather(base, fits, slot_t, h_ext, *, batch, seq, cap, tn=256):
    n_exp = slot_t.shape[1]
    width = h_ext.shape[1]
    rows = cap + SLOT_WIN
    grid_spec = pltpu.PrefetchScalarGridSpec(
        num_scalar_prefetch=2,
        grid=(batch, width // tn),
        in_specs=[pl.BlockSpec((1, n_exp, 1, seq), lambda b, n, base, fits: (b, 0, 0, 0)),
                  pl.BlockSpec((seq, tn), lambda b, n, base, fits: (b, n))],
        out_specs=pl.BlockSpec((1, n_exp, rows, tn), lambda b, n, base, fits: (b, 0, 0, n)),
    )
    return pl.pallas_call(
        functools.partial(_gather_kernel, n_blocks=seq // ROUTE_BLOCK),
        grid_spec=grid_spec,
        out_shape=jax.ShapeDtypeStruct((batch, n_exp, rows, width), BF16),
        compiler_params=pltpu.CompilerParams(
            dimension_semantics=("arbitrary", "arbitrary"), vmem_limit_bytes=V7X_VMEM_LIMIT_BYTES),
    )(base, fits, slot_t, h_ext)


def _ffn_kernel(xin_ref, g_ref, wg32_ref, wu32_ref, wd32_ref, y_ref, wg_ref, wu_ref, wd_ref, *, cap):
    e = pl.program_id(0)

    @pl.when(pl.program_id(1) == 0)
    def _():
        _cast_chunks(wg32_ref.at[0], wg_ref)
        _cast_chunks(wu32_ref.at[0], wu_ref)
        _cast_chunks(wd32_ref.at[0], wd_ref)

    lane = lax.broadcasted_iota(jnp.int32, (1, 2 * LANES), 1)
    mine = (lane == e) | (lane == LANES + e)
    for rb in range(cap // FFN_ROWS):
        rows = slice(rb * FFN_ROWS, (rb + 1) * FFN_ROWS)
        xb = xin_ref[0, 0, rows, :]
        gate = jnp.sum(jnp.where(mine, g_ref[0, 0, rows, :].astype(F32), 0.0), axis=-1, keepdims=True)
        gate_h = _dot(xb, wg_ref[...])
        hid = (gate_h * jax.nn.sigmoid(gate_h)) * _dot(xb, wu_ref[...])
        y_ref[0, 0, rows, :] = (_dot(hid.astype(BF16), wd_ref[...]) * gate).astype(BF16)
    y_ref[0, 0, cap:, :] = jnp.zeros((y_ref.shape[2] - cap, y_ref.shape[3]), BF16)


def expert_ffn(xin, wg, wu, wd, *, cap):
    batch, n_exp, rows, width = xin.shape
    _, d, f = wg.shape
    return pl.pallas_call(
        functools.partial(_ffn_kernel, cap=cap),
        grid=(n_exp, batch),
        in_specs=[pl.BlockSpec((1, 1, cap, d), lambda e, b: (b, e, 0, 0)),
                  pl.BlockSpec((1, 1, cap, 2 * LANES), lambda e, b: (b, e, 0, d // (2 * LANES))),
                  pl.BlockSpec((1, d, f), lambda e, b: (e, 0, 0)),
                  pl.BlockSpec((1, d, f), lambda e, b: (e, 0, 0)),
                  pl.BlockSpec((1, f, d), lambda e, b: (e, 0, 0))],
        out_specs=pl.BlockSpec((1, 1, rows, d), lambda e, b: (b, e, 0, 0)),
        out_shape=jax.ShapeDtypeStruct((batch, n_exp, rows, d), BF16),
        scratch_shapes=[pltpu.VMEM((d, f), BF16), pltpu.VMEM((d, f), BF16), pltpu.VMEM((f, d), BF16)],
        compiler_params=pltpu.CompilerParams(
            dimension_semantics=("arbitrary", "arbitrary"), vmem_limit_bytes=V7X_VMEM_LIMIT_BYTES),
    )(xin, xin, wg, wu, wd)


def _combine_kernel(base_ref, fits_ref, slot_ref, y_ref, x1_ref, out_ref, ywin, *, n_blocks):
    blocks_here = out_ref.shape[0] // ROUTE_BLOCK
    first = pl.program_id(2) * blocks_here

    def block(jj, carry):
        j = first + jj
        rows = pl.ds(pl.multiple_of(jj * ROUTE_BLOCK, ROUTE_BLOCK), ROUTE_BLOCK)
        tok = pl.ds(pl.multiple_of(j * ROUTE_BLOCK, ROUTE_BLOCK), ROUTE_BLOCK)
        bases, fits = _block_windows(base_ref, fits_ref, j, n_blocks)

        def run(win):
            for e, base in enumerate(bases):
                ywin[e * win:(e + 1) * win, :] = y_ref[0, e, pl.ds(base, win), :]
            hits = _one_hots(slot_ref, tok, bases, win)
            out_ref[rows, :] = x1_ref[rows, :] + _dot(hits, ywin[0:len(bases) * win, :], ((0,), (0,)))

        pl.when(fits)(functools.partial(run, SLOT_WIN_SMALL))
        pl.when(jnp.logical_not(fits))(functools.partial(run, SLOT_WIN))
        return carry

    lax.fori_loop(0, blocks_here, block, 0)


def combine(base, fits, slot_t, y, x1, *, batch, seq, tn=512, tt=2048):
    n_exp, rows, d = y.shape[1], y.shape[2], y.shape[3]
    grid_spec = pltpu.PrefetchScalarGridSpec(
        num_scalar_prefetch=2,
        grid=(batch, d // tn, seq // tt),
        in_specs=[pl.BlockSpec((1, n_exp, 1, seq), lambda b, n, t, base, fits: (b, 0, 0, 0)),
                  pl.BlockSpec((1, n_exp, rows, tn), lambda b, n, t, base, fits: (b, 0, 0, n),
                               pipeline_mode=pl.Buffered(1)),
                  pl.BlockSpec((tt, tn), lambda b, n, t, base, fits: (b * (seq // tt) + t, n))],
        out_specs=pl.BlockSpec((tt, tn), lambda b, n, t, base, fits: (b * (seq // tt) + t, n)),
        scratch_shapes=[pltpu.VMEM((n_exp * SLOT_WIN, tn), BF16)],
    )
    return pl.pallas_call(
        functools.partial(_combine_kernel, n_blocks=seq // ROUTE_BLOCK),
        grid_spec=grid_spec,
        out_shape=jax.ShapeDtypeStruct((batch * seq, d), F32),
        compiler_params=pltpu.CompilerParams(
            dimension_semantics=("arbitrary", "arbitrary", "arbitrary"),
            vmem_limit_bytes=V7X_VMEM_LIMIT_BYTES),
    )(base, fits, slot_t, y, x1)


def kernel(x, norm1_w, w_in, attn_q_norm_w, attn_k_norm_w, hgrn_lb_fwd, hgrn_lb_bwd, hgrn_out_norm_w,
           w_out, norm2_w, w_router, w_expert_gate, w_expert_up, w_expert_down):
    batch, seq, d_model = x.shape
    depth = w_in.shape[0]
    cap = max(1, CAPACITY_FACTOR * seq // N_EXPERTS)
    hgrn_col = 3 * ATTN_WIDTH
    lb_f_all = jnp.cumsum(jax.nn.softmax(hgrn_lb_fwd.astype(F32), axis=0), axis=0)
    lb_b_all = jnp.cumsum(jax.nn.softmax(hgrn_lb_bwd.astype(F32), axis=0), axis=0)

    x2d = x.reshape(batch * seq, d_model)
    for l in range(depth):
        proj = in_proj(x2d, norm1_w[l], w_in[l], attn_q_norm_w[l], attn_k_norm_w[l])
        o_f, o_b = hgrn(proj, lb_f_all[l], lb_b_all[l], batch=batch, seq=seq, col0=hgrn_col)
        a_out = attention(proj, batch=batch, seq=seq)
        x1, h_ext, aff = out_proj(x2d, a_out, o_f, o_b, proj, hgrn_out_norm_w[l], w_out[l],
                                  norm2_w[l], w_router[l], hg_col=hgrn_col + 4 * HGRN_WIDTH)
        slot, start = routing(aff, batch=batch, seq=seq, cap=cap)

        def expert_major(t):
            return t.reshape(batch, -1, LANES)[:, :, :N_EXPERTS].transpose(0, 2, 1)

        slot_t = expert_major(slot).reshape(batch, N_EXPERTS, 1, seq)
        start_t = expert_major(start)
        base_t = start_t // SLOT_ALIGN * SLOT_ALIGN
        end_t = jnp.concatenate([start_t[:, :, 1:], jnp.full((batch, N_EXPERTS, 1), cap, jnp.int32)], axis=2)
        fits = (jnp.max(end_t - base_t, axis=1) <= SLOT_WIN_SMALL).astype(jnp.int32).reshape(-1)
        base = base_t.reshape(-1)
        xin = gather(base, fits, slot_t, h_ext, batch=batch, seq=seq, cap=cap)
        y = expert_ffn(xin, w_expert_gate[l], w_expert_up[l], w_expert_down[l], cap=cap)
        x2d = combine(base, fits, slot_t, y, x1, batch=batch, seq=seq)
    return x2d.reshape(batch, seq, d_model)
```

```python
import functools

import jax
import jax.numpy as jnp
from jax import lax
from jax.experimental import pallas as pl
from jax.experimental.pallas import tpu as pltpu

F32 = jnp.float32
BF16 = jnp.bfloat16

NORM_EPS = 1e-6
NEG_BIG = -1e30
ATTN_HEAD_DIM = 64
ATTN_HEADS = 8
ATTN_WIDTH = ATTN_HEADS * ATTN_HEAD_DIM
DILATED_PATTERNS = ((128, 1), (512, 4), (2048, 16))
ALIBI_MAX_BIAS = 8.0
HGRN_DIM = 128
HGRN_HEADS = 4
HGRN_WIDTH = HGRN_HEADS * HGRN_DIM
N_EXPERTS = 16
CAPACITY_FACTOR = 2
V7X_VMEM_LIMIT_BYTES = 56 * 1024 * 1024


def _dot(a, b, dims=((1,), (0,))):
    return lax.dot_general(a, b, (dims, ((), ())), preferred_element_type=F32)


def _dot_nt(a, b):
    return _dot(a, b, ((1,), (1,)))


def _cast_chunks(src_ref, dst_ref, rows=256):
    for r0 in range(0, src_ref.shape[0], rows):
        dst_ref[r0:r0 + rows, :] = src_ref[r0:r0 + rows, :].astype(dst_ref.dtype)


def _in_proj_kernel(x_ref, nw_ref, w32_ref, qw_ref, kw_ref, o_ref, w_ref, *, n_chunk):
    @pl.when(pl.program_id(0) == 0)
    def _():
        _cast_chunks(w32_ref, w_ref)

    x = x_ref[...]
    h = x * lax.rsqrt(jnp.mean(x * x, axis=-1, keepdims=True) + NORM_EPS) * nw_ref[...]
    h = h.astype(BF16)
    n_total = o_ref.shape[1]
    for c in range(n_total // n_chunk):
        cols = slice(c * n_chunk, (c + 1) * n_chunk)
        o_ref[:, cols] = _dot(h, w_ref[:, cols])
    for sl in range(ATTN_WIDTH // LANES):
        cols = slice(sl * LANES, (sl + 1) * LANES)
        o_ref[:, cols] = _head_rms(o_ref[:, cols], qw_ref[:, cols]) * (ATTN_HEAD_DIM ** -0.5)
        kcols = slice(ATTN_WIDTH + sl * LANES, ATTN_WIDTH + (sl + 1) * LANES)
        o_ref[:, kcols] = _head_rms(o_ref[:, kcols], kw_ref[:, cols])


def in_proj(x2d, norm_w, w, qw, kw, *, tm=256, n_chunk=512):
    m, d = x2d.shape
    n = w.shape[1]
    wspec = pl.BlockSpec((1, ATTN_WIDTH), lambda i: (0, 0))
    return pl.pallas_call(
        functools.partial(_in_proj_kernel, n_chunk=n_chunk),
        grid=(m // tm,),
        in_specs=[
            pl.BlockSpec((tm, d), lambda i: (i, 0)),
            pl.BlockSpec((1, d), lambda i: (0, 0)),
            pl.BlockSpec((d, n), lambda i: (0, 0), pipeline_mode=pl.Buffered(1)),
            wspec, wspec,
        ],
        out_specs=pl.BlockSpec((tm, n), lambda i: (i, 0)),
        out_shape=jax.ShapeDtypeStruct((m, n), F32),
        scratch_shapes=[pltpu.VMEM((d, n), BF16)],
        compiler_params=pltpu.CompilerParams(
            dimension_semantics=("arbitrary",), vmem_limit_bytes=V7X_VMEM_LIMIT_BYTES),
    )(x2d, norm_w.reshape(1, d), w,
      jnp.tile(qw, ATTN_HEADS).reshape(1, -1), jnp.tile(kw, ATTN_HEADS).reshape(1, -1))


HGRN_CHUNK = 64
HGRN_SUB = 16
HGRN_MAX_LOG_DECAY = 80.0


def _hgrn_chunk(load, store, lb, st_ref, *, reverse, bounded_decay):
    q_raw, z, v = load()
    c, dk = q_raw.shape
    sub = HGRN_SUB
    q = q_raw * jax.nn.sigmoid(q_raw)
    ez = jnp.exp(-jnp.abs(z))
    inv = 1.0 / (1.0 + ez)
    pos = z >= 0.0
    f = lb + (1.0 - lb) * jnp.where(pos, inv, ez * inv)
    k = (1.0 - lb) * jnp.where(pos, ez * inv, inv)
    g = jnp.log(f)

    row = lax.broadcasted_iota(jnp.int32, (c, c), 0)
    col = lax.broadcasted_iota(jnp.int32, (c, c), 1)
    tri = (col >= row) if reverse else (col <= row)
    tri_bf = jnp.where(tri, 1.0, 0.0).astype(BF16)
    g_hi, g_lo = _split2(g)
    yield
    cum = _dot(tri_bf, g_hi) + _dot(tri_bf, g_lo)
    yield

    def edge(r):
        return cum[r:r + 1, :]

    if reverse:
        ref_half, ref_q_lo, ref_q_hi, ref_end = edge(32), edge(16), edge(48), edge(0)
    else:
        ref_half, ref_q_lo, ref_q_hi, ref_end = edge(31), edge(15), edge(47), edge(c - 1)

    r1 = lax.broadcasted_iota(jnp.int32, (c, 1), 0)
    ref_quarter = jnp.where(r1 < 32, ref_q_lo, ref_q_hi)

    st = st_ref[...]
    v_bf = v.astype(BF16)
    tb, sb = row // sub, col // sub
    if reverse:
        m1 = (tb < 2) & (sb >= 2)
        m2 = ((tb == 0) & (sb == 1)) | ((tb == 2) & (sb == 3))
    else:
        m1 = (tb >= 2) & (sb < 2)
        m2 = ((tb == 1) & (sb == 0)) | ((tb == 3) & (sb == 2))
    if bounded_decay:
        first = (sub - 1) if reverse else 0
        blocks = [slice(b * sub, (b + 1) * sub) for b in range(c // sub)]
        refs_d = [edge(b * sub + first) for b in range(c // sub)]
        q3 = jnp.concatenate([q[rows] * jnp.exp(cum[rows] - d) for rows, d in zip(blocks, refs_d)], axis=0)
        k3 = jnp.concatenate([k[rows] * jnp.exp(d - cum[rows]) for rows, d in zip(blocks, refs_d)], axis=0)

        def scaled(x, factors):
            return jnp.concatenate([x[rows] * f for rows, f in zip(blocks, factors)], axis=0).astype(BF16)

        refs_q = [ref_q_lo, ref_q_lo, ref_q_hi, ref_q_hi]
        qe = scaled(q3, [jnp.exp(d) for d in refs_d])
        ke = scaled(k3, [jnp.exp(ref_end - d) for d in refs_d])
        q1 = scaled(q3, [jnp.exp(jnp.minimum(d - ref_half, 0.0)) for d in refs_d])
        k1 = scaled(k3, [jnp.exp(jnp.minimum(ref_half - d, 0.0)) for d in refs_d])
        q2 = scaled(q3, [jnp.exp(jnp.minimum(d - rq, 0.0)) for d, rq in zip(refs_d, refs_q)])
        k2 = scaled(k3, [jnp.exp(jnp.minimum(rq - d, 0.0)) for d, rq in zip(refs_d, refs_q)])
        q3, k3 = q3.astype(BF16), k3.astype(BF16)
        m3 = (tb == sb) & ((col >= row) if reverse else (col <= row))
    else:
        qe = (q * jnp.exp(cum)).astype(BF16)
        ke = (k * jnp.exp(ref_end - cum)).astype(BF16)
        q1 = (q * jnp.exp(jnp.minimum(cum - ref_half, 0.0))).astype(BF16)
        k1 = (k * jnp.exp(jnp.minimum(ref_half - cum, 0.0))).astype(BF16)
        q2 = (q * jnp.exp(jnp.minimum(cum - ref_quarter, 0.0))).astype(BF16)
        k2 = (k * jnp.exp(jnp.minimum(ref_quarter - cum, 0.0))).astype(BF16)
    yield

    o = _dot_nt(qe, st.astype(BF16))
    st_ref[...] = st * jnp.exp(ref_end) + _dot(v_bf, ke, ((0,), (0,)))
    a = jnp.where(m1, _dot_nt(q1, k1), 0.0) + jnp.where(m2, _dot_nt(q2, k2), 0.0)
    if bounded_decay:
        a = a + jnp.where(m3, _dot_nt(q3, k3), 0.0)
        a_bf = a.astype(BF16)
        yield
        store(o + _dot(a_bf, v_bf))
        return
    yield

    t_loc = lax.broadcasted_iota(jnp.int32, (sub, 1), 0)
    lane = lax.broadcasted_iota(jnp.int32, (sub, c), 1)
    blocks = []
    for blk in range(c // sub):
        rows = slice(blk * sub, (blk + 1) * sub)
        cum_b, q_b = cum[rows], q[rows]
        a_b = jnp.zeros((sub, c), F32)
        for s_loc in range(sub):
            s = blk * sub + s_loc
            keep = (t_loc <= s_loc) if reverse else (t_loc >= s_loc)
            e = jnp.exp(jnp.where(keep, cum_b - cum[s:s + 1, :], NEG_BIG))
            p = (q_b * k[s:s + 1, :]) * e
            a_b = jnp.where(lane == s, jnp.sum(p, axis=-1, keepdims=True), a_b)
        blocks.append(a_b)
    a = a + jnp.concatenate(blocks, axis=0)
    store(o + _dot(a.astype(BF16), v_bf))


def _hgrn_kernel(bounded_ref, qf_ref, zf_ref, vf_ref, qb_ref, zb_ref, vb_ref, lbf_ref, lbb_ref,
                 of_ref, ob_ref, sf_ref, sb_ref):
    @pl.when(pl.program_id(1) == 0)
    def _():
        sf_ref[...] = jnp.zeros_like(sf_ref)
        sb_ref[...] = jnp.zeros_like(sb_ref)

    n_chunks = qf_ref.shape[0] // HGRN_CHUNK

    def run(bounded_decay):
        def body(ci, carry):
            rf = pl.ds(pl.multiple_of(ci * HGRN_CHUNK, HGRN_CHUNK), HGRN_CHUNK)
            rb = pl.ds(pl.multiple_of((n_chunks - 1 - ci) * HGRN_CHUNK, HGRN_CHUNK), HGRN_CHUNK)
            chains = []
            for h in range(HGRN_HEADS):
                cols = slice(h * HGRN_DIM, (h + 1) * HGRN_DIM)
                for rows, q_ref, z_ref, v_ref, lb_ref, o_ref, st_ref, reverse in (
                        (rf, qf_ref, zf_ref, vf_ref, lbf_ref, of_ref, sf_ref, False),
                        (rb, qb_ref, zb_ref, vb_ref, lbb_ref, ob_ref, sb_ref, True)):
                    def load(rows=rows, cols=cols, q_ref=q_ref, z_ref=z_ref, v_ref=v_ref):
                        return q_ref[rows, cols], z_ref[rows, cols], v_ref[rows, cols]

                    def store(o, rows=rows, cols=cols, o_ref=o_ref):
                        o_ref[rows, cols] = o

                    chains.append(_hgrn_chunk(load, store, lb_ref[:, cols], st_ref.at[h],
                                              reverse=reverse, bounded_decay=bounded_decay))
            while chains:
                chains = [ch for ch in chains if next(ch, True) is None]
            return carry

        lax.fori_loop(0, n_chunks, body, 0)

    pl.when(bounded_ref[0] == 1)(functools.partial(run, True))
    pl.when(bounded_ref[0] != 1)(functools.partial(run, False))


def hgrn(proj, lb_f, lb_b, *, batch, seq, col0, tile=512):
    nt = seq // tile
    cb = col0 // HGRN_WIDTH
    worst = -(HGRN_SUB - 1) * jnp.log(jnp.minimum(jnp.min(lb_f), jnp.min(lb_b)))
    bounded = (worst <= HGRN_MAX_LOG_DECAY).astype(jnp.int32).reshape(1)

    def fwd(colblock):
        return pl.BlockSpec((tile, HGRN_WIDTH), lambda b, i, flag: (b * nt + i, cb + colblock))

    def bwd(colblock):
        return pl.BlockSpec((tile, HGRN_WIDTH), lambda b, i, flag: (b * nt + nt - 1 - i, cb + colblock))

    lb_spec = pl.BlockSpec((1, HGRN_WIDTH), lambda b, i, flag: (0, 0))
    out_shape = jax.ShapeDtypeStruct((batch * seq, HGRN_WIDTH), F32)
    state = pltpu.VMEM((HGRN_HEADS, HGRN_DIM, HGRN_DIM), F32)
    grid_spec = pltpu.PrefetchScalarGridSpec(
        num_scalar_prefetch=1,
        grid=(batch, nt),
        in_specs=[fwd(0), fwd(1), fwd(3), bwd(0), bwd(2), bwd(3), lb_spec, lb_spec],
        out_specs=[
            pl.BlockSpec((tile, HGRN_WIDTH), lambda b, i, flag: (b * nt + i, 0)),
            pl.BlockSpec((tile, HGRN_WIDTH), lambda b, i, flag: (b * nt + nt - 1 - i, 0)),
        ],
        scratch_shapes=[state, state],
    )
    return pl.pallas_call(
        _hgrn_kernel,
        grid_spec=grid_spec,
        out_shape=[out_shape, out_shape],
        compiler_params=pltpu.CompilerParams(
            dimension_semantics=("arbitrary", "arbitrary"),
            vmem_limit_bytes=V7X_VMEM_LIMIT_BYTES),
    )(bounded, proj, proj, proj, proj, proj, proj, lb_f.reshape(1, -1), lb_b.reshape(1, -1))


ATTN_HALF = 64
ATTN_QT = 128
LANES = 128
ATTN_SLABS = ATTN_WIDTH // LANES


def _head_rms(xs, w):
    lo = lax.broadcasted_iota(jnp.int32, (1, LANES), 1) < ATTN_HEAD_DIM
    sq = xs * xs
    s_lo = jnp.sum(jnp.where(lo, sq, 0.0), axis=-1, keepdims=True)
    s_hi = jnp.sum(jnp.where(lo, 0.0, sq), axis=-1, keepdims=True)
    ms = jnp.where(lo, s_lo, s_hi) * (1.0 / ATTN_HEAD_DIM)
    return xs * lax.rsqrt(ms + NORM_EPS) * w


def _attn_kernel(*refs, seq):
    n_pat = len(DILATED_PATTERNS)
    ns = ATTN_SLABS
    q_refs, k_refs, kp_refs, kn_refs, v_refs, vp_refs, vn_refs = [refs[g * ns:(g + 1) * ns] for g in range(7)]
    bias_refs = refs[7 * ns:7 * ns + n_pat]
    o_ref, kwin, vwin, qn, s_scr, o_scr, l_scr = refs[7 * ns + n_pat:]
    tile = o_ref.shape[0]
    i = pl.program_id(1)
    lo = lax.broadcasted_iota(jnp.int32, (1, LANES), 1) < ATTN_HEAD_DIM

    for pi, ((_, dil), bias_ref) in enumerate(zip(DILATED_PATTERNS, bias_refs)):
        sub_rows = tile // dil
        qt = min(ATTN_QT, sub_rows)
        kt = qt + 2 * ATTN_HALF
        halo = ATTN_HALF * dil
        kcol = lax.broadcasted_iota(jnp.int32, (1, kt), 1)
        n_sub = sub_rows // qt

        def fill(r, ws, dil=dil, sub_rows=sub_rows, halo=halo):
            def rows_of(ref, start, n):
                return ref[pl.ds(start + r, n, stride=dil), :]

            main = slice(ATTN_HALF, ATTN_HALF + sub_rows)
            after = slice(ATTN_HALF + sub_rows, 2 * ATTN_HALF + sub_rows)
            for sl in range(ns):
                w = ws * ns + sl
                kwin[w, 0:ATTN_HALF, :] = rows_of(kp_refs[sl], tile - halo, ATTN_HALF).astype(BF16)
                kwin[w, main, :] = rows_of(k_refs[sl], 0, sub_rows).astype(BF16)
                kwin[w, after, :] = rows_of(kn_refs[sl], 0, ATTN_HALF).astype(BF16)
                vwin[w, 0:ATTN_HALF, :] = rows_of(vp_refs[sl], tile - halo, ATTN_HALF).astype(BF16)
                vwin[w, main, :] = rows_of(v_refs[sl], 0, sub_rows).astype(BF16)
                vwin[w, after, :] = rows_of(vn_refs[sl], 0, ATTN_HALF).astype(BF16)
                qn[w, 0:sub_rows, :] = rows_of(q_refs[sl], 0, sub_rows).astype(BF16)

        def sub(j, r, ws, ss, pi=pi, dil=dil, bias_ref=bias_ref, qt=qt, kt=kt, kcol=kcol):
            r0 = j * qt if isinstance(j, int) else pl.multiple_of(j * qt, qt)
            lk = (i * tile) // dil + j * qt - ATTN_HALF + kcol
            edge = jnp.where((lk >= 0) & (lk < seq // dil), 0.0, NEG_BIG)
            out_rows = pl.ds(r + j * (qt * dil), qt, stride=dil)
            for sl in range(ns):
                qs = qn[ws * ns + sl, pl.ds(r0, qt), :]
                ks = kwin[ws * ns + sl, pl.ds(r0, kt), :]
                for hh in range(2):
                    qh = jnp.where(lo if hh == 0 else jnp.logical_not(lo), qs, jnp.zeros_like(qs))
                    s_scr[ss * ATTN_HEADS + 2 * sl + hh, 0:qt, 0:kt] = (
                        _dot_nt(qh, ks) + bias_ref[2 * sl + hh] + edge)
            for sl in range(ns):
                vs = vwin[ws * ns + sl, pl.ds(r0, kt), :]
                o_slab = l_slab = None
                for hh in range(2):
                    s = s_scr[ss * ATTN_HEADS + 2 * sl + hh, 0:qt, 0:kt]
                    m = jnp.max(s, axis=-1, keepdims=True)
                    p = jnp.exp(s - m)
                    den = jnp.sum(p, axis=-1, keepdims=True)
                    oh = _dot(p.astype(BF16), vs) * (1.0 / den)
                    lse = m + jnp.log(den)
                    o_slab = oh if hh == 0 else jnp.where(lo, o_slab, oh)
                    l_slab = lse if hh == 0 else jnp.where(lo, l_slab, lse)
                o_scr[pi * ns + sl, out_rows, :] = o_slab
                l_scr[pi * ns + sl, out_rows, :] = l_slab

        if n_sub >= 2:
            assert n_sub % 2 == 0

            def subsequence(r, carry, fill=fill, sub=sub, n_sub=n_sub):
                fill(r, 0)
                if n_sub == 2:
                    sub(0, r, 0, 0)
                    sub(1, r, 0, 1)
                else:
                    def pair(jj, carry2):
                        sub(2 * jj, r, 0, 0)
                        sub(2 * jj + 1, r, 0, 1)
                        return carry2
                    lax.fori_loop(0, n_sub // 2, pair, 0)
                return carry

            lax.fori_loop(0, dil, subsequence, 0)
        else:
            assert dil % 2 == 0

            def subsequence_pair(rp, carry, fill=fill, sub=sub):
                fill(2 * rp, 0)
                fill(2 * rp + 1, 1)
                sub(0, 2 * rp, 0, 0)
                sub(0, 2 * rp + 1, 1, 1)
                return carry

            lax.fori_loop(0, dil // 2, subsequence_pair, 0)

    def merge(c, carry):
        rows = pl.ds(pl.multiple_of(c * ATTN_QT, ATTN_QT), ATTN_QT)
        for sl in range(ns):
            ls = [l_scr[p * ns + sl, rows, :] for p in range(n_pat)]
            mx = functools.reduce(jnp.maximum, ls)
            ws = [jnp.exp(l - mx) for l in ls]
            num = sum(w * o_scr[p * ns + sl, rows, :] for p, w in enumerate(ws))
            o_ref[rows, sl * LANES:(sl + 1) * LANES] = num / sum(ws)
        return carry

    lax.fori_loop(0, tile // ATTN_QT, merge, 0)


def _attn_bias(dilation, qt):
    slopes = jnp.exp2(-ALIBI_MAX_BIAS * jnp.arange(1, ATTN_HEADS + 1, dtype=F32) / ATTN_HEADS)
    t = jnp.arange(qt)[:, None]
    j = jnp.arange(qt + 2 * ATTN_HALF)[None, :]
    dist = jnp.abs(j - ATTN_HALF - t)
    alibi = -slopes[:, None, None] * (dilation * dist).astype(F32)[None]
    return jnp.where((dist <= ATTN_HALF)[None], alibi, NEG_BIG)


def attention(proj, *, batch, seq, tile=1024):
    nt = seq // tile
    n_pat = len(DILATED_PATTERNS)
    assert all(tile % (ATTN_HALF * dil) == 0 for _, dil in DILATED_PATTERNS)

    def slabs(c, shift):
        def spec(sl):
            def index(b, i):
                return (b * nt + jnp.clip(i + shift, 0, nt - 1), c * ATTN_SLABS + sl)
            return pl.BlockSpec((tile, LANES), index)
        return [spec(sl) for sl in range(ATTN_SLABS)]

    groups = [(0, 0), (1, 0), (1, -1), (1, 1), (2, 0), (2, -1), (2, 1)]
    biases = [_attn_bias(dil, min(ATTN_QT, tile // dil)) for _, dil in DILATED_PATTERNS]
    bias_specs = [pl.BlockSpec(bias.shape, lambda b, i: (0, 0, 0)) for bias in biases]
    return pl.pallas_call(
        functools.partial(_attn_kernel, seq=seq),
        grid=(batch, nt),
        in_specs=[s for c, shift in groups for s in slabs(c, shift)] + bias_specs,
        out_specs=pl.BlockSpec((tile, ATTN_WIDTH), lambda b, i: (b * nt + i, 0)),
        out_shape=jax.ShapeDtypeStruct((batch * seq, ATTN_WIDTH), F32),
        scratch_shapes=[pltpu.VMEM((2 * ATTN_SLABS, tile + 2 * ATTN_HALF, LANES), BF16),
                        pltpu.VMEM((2 * ATTN_SLABS, tile + 2 * ATTN_HALF, LANES), BF16),
                        pltpu.VMEM((2 * ATTN_SLABS, tile, LANES), BF16),
                        pltpu.VMEM((2 * ATTN_HEADS, ATTN_QT, ATTN_QT + 2 * ATTN_HALF), F32),
                        pltpu.VMEM((n_pat * ATTN_SLABS, tile, LANES), F32),
                        pltpu.VMEM((n_pat * ATTN_SLABS, tile, LANES), F32)],
        compiler_params=pltpu.CompilerParams(
            dimension_semantics=("arbitrary", "arbitrary"), vmem_limit_bytes=V7X_VMEM_LIMIT_BYTES),
    )(*([proj] * (len(groups) * ATTN_SLABS)), *biases)


def _split2(x):
    hi = x.astype(BF16)
    return hi, (x - hi.astype(F32)).astype(BF16)


def _out_proj_kernel(x_ref, a_ref, of_ref, ob_ref, hg_ref, hw_ref, wo32_ref, n2_ref, wr_ref,
                     x1_ref, h2_ref, aff_ref, wo_ref):
    @pl.when(pl.program_id(0) == 0)
    def _():
        _cast_chunks(wo32_ref, wo_ref)

    a_out = a_ref[...]
    o = of_ref[...] + ob_ref[...]
    hg = hg_ref[...]
    hw = hw_ref[...]
    b_parts = []
    for sl in range(HGRN_HEADS):
        cols = slice(sl * HGRN_DIM, (sl + 1) * HGRN_DIM)
        os_ = o[:, cols]
        y = os_ * lax.rsqrt(jnp.mean(os_ * os_, axis=-1, keepdims=True) + NORM_EPS) * hw
        g = hg[:, cols]
        b_parts.append(y * (g * jax.nn.sigmoid(g)))
    mixed = jnp.concatenate([a_out] + b_parts, axis=-1).astype(BF16)

    x1 = x_ref[...] + _dot(mixed, wo_ref[...])
    x1_ref[...] = x1
    h2 = x1 * lax.rsqrt(jnp.mean(x1 * x1, axis=-1, keepdims=True) + NORM_EPS) * n2_ref[...]
    d = x1.shape[1]
    h_hi, h_lo = _split2(h2)
    h2_ref[:, 0:d] = h_hi

    both = _dot(h_hi, wr_ref[...])
    logits = both[:, 0:LANES] + both[:, LANES:2 * LANES] + _dot(h_lo, wr_ref[:, 0:LANES])
    valid = lax.broadcasted_iota(jnp.int32, (1, LANES), 1) < N_EXPERTS
    logits = jnp.where(valid, logits, NEG_BIG)
    ex = jnp.exp(logits - jnp.max(logits, axis=-1, keepdims=True))
    aff = ex / jnp.sum(ex, axis=-1, keepdims=True)
    aff_ref[...] = aff
    a_hi, a_lo = _split2(aff)
    h2_ref[:, d:d + LANES] = a_hi
    h2_ref[:, d + LANES:d + 2 * LANES] = a_lo


def out_proj(x2d, a_out, o_f, o_b, proj, hgrn_norm_w, w_out, norm2_w, w_router, *, hg_col, tm=256):
    m, d = x2d.shape
    wr = jnp.pad(w_router, ((0, 0), (0, LANES - N_EXPERTS)))
    wr_pair = jnp.concatenate(_split2(wr), axis=1)

    def rows(width, colblock=0):
        return pl.BlockSpec((tm, width), lambda i: (i, colblock))

    def const(shape):
        return pl.BlockSpec(shape, lambda i: (0, 0))

    return pl.pallas_call(
        _out_proj_kernel,
        grid=(m // tm,),
        in_specs=[rows(d), rows(ATTN_WIDTH),
                  rows(HGRN_WIDTH), rows(HGRN_WIDTH), rows(HGRN_WIDTH, hg_col // HGRN_WIDTH),
                  const((1, HGRN_DIM)), const((ATTN_WIDTH + HGRN_WIDTH, d)),
                  const((1, d)), const((d, 2 * LANES))],
        out_specs=[rows(d), rows(d + 2 * LANES), rows(LANES)],
        out_shape=[jax.ShapeDtypeStruct((m, d), F32), jax.ShapeDtypeStruct((m, d + 2 * LANES), BF16),
                   jax.ShapeDtypeStruct((m, LANES), F32)],
        scratch_shapes=[pltpu.VMEM((ATTN_WIDTH + HGRN_WIDTH, d), BF16)],
        compiler_params=pltpu.CompilerParams(
            dimension_semantics=("arbitrary",), vmem_limit_bytes=V7X_VMEM_LIMIT_BYTES),
    )(x2d, a_out, o_f, o_b, proj, hgrn_norm_w.reshape(1, -1), w_out,
      norm2_w.reshape(1, -1), wr_pair)


ROUTE_BLOCK = 256
COUNT_ROWS = 512


def _routing_kernel(aff_ref, slot_ref, start_ref, *, cap):
    seq = aff_ref.shape[0]

    def count(pred):
        def body(c, acc):
            blk = aff_ref[pl.ds(pl.multiple_of(c * COUNT_ROWS, COUNT_ROWS), COUNT_ROWS), :]
            hits = jnp.where(pred(blk), 1, 0).reshape(COUNT_ROWS // 8, 8, LANES)
            return acc + jnp.sum(hits, axis=0)
        acc = lax.fori_loop(0, seq // COUNT_ROWS, body, jnp.zeros((8, LANES), jnp.int32), unroll=4)
        return jnp.sum(acc, axis=0, keepdims=True)

    def bit_step(t, thr_bits):
        cand = thr_bits | jnp.left_shift(jnp.int32(1), 30 - t)
        cand_f = pltpu.bitcast(cand, F32)
        return jnp.where(count(lambda blk: blk >= cand_f) >= cap, cand, thr_bits)

    thr = pltpu.bitcast(lax.fori_loop(0, 31, bit_step, jnp.zeros((1, LANES), jnp.int32)), F32)
    need = (cap - count(lambda blk: blk > thr)).astype(F32)

    row = lax.broadcasted_iota(jnp.int32, (ROUTE_BLOCK, ROUTE_BLOCK), 0)
    col = lax.broadcasted_iota(jnp.int32, (ROUTE_BLOCK, ROUTE_BLOCK), 1)
    before = jnp.where(col < row, 1.0, 0.0).astype(BF16)

    group = 2

    def assign(jg, carry):
        c_eq, c_sel = carry
        starts = [pl.multiple_of((jg * group + u) * ROUTE_BLOCK, ROUTE_BLOCK) for u in range(group)]
        blks = [aff_ref[pl.ds(r0, ROUTE_BLOCK), :] for r0 in starts]
        gts = [blk > thr for blk in blks]
        eqs = [blk == thr for blk in blks]
        eq_fs = [jnp.where(eq, 1.0, 0.0) for eq in eqs]
        eq_pre = [_dot(before, eq_f.astype(BF16)) for eq_f in eq_fs]
        sels = []
        for u in range(group):
            sels.append(gts[u] | (eqs[u] & (eq_pre[u] + c_eq < need)))
            c_eq = c_eq + jnp.sum(eq_fs[u], axis=0, keepdims=True)
        sel_fs = [jnp.where(sel, 1.0, 0.0) for sel in sels]
        sel_pre = [_dot(before, sel_f.astype(BF16)) for sel_f in sel_fs]
        for u in range(group):
            slot_ref[pl.ds(starts[u], ROUTE_BLOCK), :] = jnp.where(sels[u], sel_pre[u] + c_sel, -1.0).astype(jnp.int32)
            start_ref[pl.ds(jg * group + u, 1), :] = c_sel.astype(jnp.int32)
            c_sel = c_sel + jnp.sum(sel_fs[u], axis=0, keepdims=True)
        return c_eq, c_sel

    zero = jnp.zeros((1, LANES), F32)
    lax.fori_loop(0, seq // (ROUTE_BLOCK * group), assign, (zero, zero))


def routing(aff, *, batch, seq, cap):
    nblk = seq // ROUTE_BLOCK
    return pl.pallas_call(
        functools.partial(_routing_kernel, cap=cap),
        grid=(batch,),
        in_specs=[pl.BlockSpec((seq, LANES), lambda b: (b, 0))],
        out_specs=[pl.BlockSpec((seq, LANES), lambda b: (b, 0)), pl.BlockSpec((nblk, LANES), lambda b: (b, 0))],
        out_shape=[jax.ShapeDtypeStruct((batch * seq, LANES), jnp.int32),
                   jax.ShapeDtypeStruct((batch * nblk, LANES), jnp.int32)],
        compiler_params=pltpu.CompilerParams(
            dimension_semantics=("arbitrary",), vmem_limit_bytes=V7X_VMEM_LIMIT_BYTES),
    )(aff)


SLOT_ALIGN = 16
SLOT_WIN = ROUTE_BLOCK + SLOT_ALIGN
SLOT_WIN_SMALL = 64
FFN_ROWS = 256


def _one_hots(slot_ref, tok, bases, win):
    r = lax.broadcasted_iota(jnp.int32, (win, ROUTE_BLOCK), 0)
    return jnp.concatenate(
        [jnp.where((slot_ref[0, e, :, tok] - base) == r, 1.0, 0.0).astype(BF16) for e, base in enumerate(bases)],
        axis=0)


def _block_windows(base_ref, fits_ref, j, n_blocks):
    b = pl.program_id(0)
    n_exp = N_EXPERTS
    bases = [pl.multiple_of(base_ref[(b * n_exp + e) * n_blocks + j], SLOT_ALIGN) for e in range(n_exp)]
    return bases, fits_ref[b * n_blocks + j] == 1


def _gather_kernel(base_ref, fits_ref, slot_ref, h_ref, xin_ref, *, n_blocks):
    xin_ref[...] = jnp.zeros_like(xin_ref)

    def block(j, carry):
        tok = pl.ds(pl.multiple_of(j * ROUTE_BLOCK, ROUTE_BLOCK), ROUTE_BLOCK)
        bases, fits = _block_windows(base_ref, fits_ref, j, n_blocks)

        def run(win):
            rows = _dot(_one_hots(slot_ref, tok, bases, win), h_ref[tok, :]).astype(BF16)
            for e, base in enumerate(bases):
                xin_ref[0, e, pl.ds(base, win), :] += rows[e * win:(e + 1) * win]

        pl.when(fits)(functools.partial(run, SLOT_WIN_SMALL))
        pl.when(jnp.logical_not(fits))(functools.partial(run, SLOT_WIN))
        return carry

    lax.fori_loop(0, n_blocks, block, 0)


def gather(base, fits, slot_t, h_ext, *, batch, seq, cap, tn=256):
    n_exp = slot_t.shape[1]
    width = h_ext.shape[1]
    rows = cap + SLOT_WIN
    grid_spec = pltpu.PrefetchScalarGridSpec(
        num_scalar_prefetch=2,
        grid=(batch, width // tn),
        in_specs=[pl.BlockSpec((1, n_exp, 1, seq), lambda b, n, base, fits: (b, 0, 0, 0)),
                  pl.BlockSpec((seq, tn), lambda b, n, base, fits: (b, n))],
        out_specs=pl.BlockSpec((1, n_exp, rows, tn), lambda b, n, base, fits: (b, 0, 0, n)),
    )
    return pl.pallas_call(
        functools.partial(_gather_kernel, n_blocks=seq // ROUTE_BLOCK),
        grid_spec=grid_spec,
        out_shape=jax.ShapeDtypeStruct((batch, n_exp, rows, width), BF16),
        compiler_params=pltpu.CompilerParams(
            dimension_semantics=("arbitrary", "arbitrary"), vmem_limit_bytes=V7X_VMEM_LIMIT_BYTES),
    )(base, fits, slot_t, h_ext)


def _ffn_kernel(xin_ref, g_ref, wg32_ref, wu32_ref, wd32_ref, y_ref, wg_ref, wu_ref, wd_ref, *, cap):
    e = pl.program_id(0)

    @pl.when(pl.program_id(1) == 0)
    def _():
        _cast_chunks(wg32_ref.at[0], wg_ref)
        _cast_chunks(wu32_ref.at[0], wu_ref)
        _cast_chunks(wd32_ref.at[0], wd_ref)

    lane = lax.broadcasted_iota(jnp.int32, (1, 2 * LANES), 1)
    mine = (lane == e) | (lane == LANES + e)
    for rb in range(cap // FFN_ROWS):
        rows = slice(rb * FFN_ROWS, (rb + 1) * FFN_ROWS)
        xb = xin_ref[0, 0, rows, :]
        gate = jnp.sum(jnp.where(mine, g_ref[0, 0, rows, :].astype(F32), 0.0), axis=-1, keepdims=True)
        gate_h = _dot(xb, wg_ref[...])
        hid = (gate_h * jax.nn.sigmoid(gate_h)) * _dot(xb, wu_ref[...])
        y_ref[0, 0, rows, :] = (_dot(hid.astype(BF16), wd_ref[...]) * gate).astype(BF16)
    y_ref[0, 0, cap:, :] = jnp.zeros((y_ref.shape[2] - cap, y_ref.shape[3]), BF16)


def expert_ffn(xin, wg, wu, wd, *, cap):
    batch, n_exp, rows, width = xin.shape
    _, d, f = wg.shape
    return pl.pallas_call(
        functools.partial(_ffn_kernel, cap=cap),
        grid=(n_exp, batch),
        in_specs=[pl.BlockSpec((1, 1, cap, d), lambda e, b: (b, e, 0, 0)),
                  pl.BlockSpec((1, 1, cap, 2 * LANES), lambda e, b: (b, e, 0, d // (2 * LANES))),
                  pl.BlockSpec((1, d, f), lambda e, b: (e, 0, 0)),
                  pl.BlockSpec((1, d, f), lambda e, b: (e, 0, 0)),
                  pl.BlockSpec((1, f, d), lambda e, b: (e, 0, 0))],
        out_specs=pl.BlockSpec((1, 1, rows, d), lambda e, b: (b, e, 0, 0)),
        out_shape=jax.ShapeDtypeStruct((batch, n_exp, rows, d), BF16),
        scratch_shapes=[pltpu.VMEM((d, f), BF16), pltpu.VMEM((d, f), BF16), pltpu.VMEM((f, d), BF16)],
        compiler_params=pltpu.CompilerParams(
            dimension_semantics=("arbitrary", "arbitrary"), vmem_limit_bytes=V7X_VMEM_LIMIT_BYTES),
    )(xin, xin, wg, wu, wd)


def _combine_kernel(base_ref, fits_ref, slot_ref, y_ref, x1_ref, out_ref, ywin, *, n_blocks):
    blocks_here = out_ref.shape[0] // ROUTE_BLOCK
    first = pl.program_id(2) * blocks_here

    def block(jj, carry):
        j = first + jj
        rows = pl.ds(pl.multiple_of(jj * ROUTE_BLOCK, ROUTE_BLOCK), ROUTE_BLOCK)
        tok = pl.ds(pl.multiple_of(j * ROUTE_BLOCK, ROUTE_BLOCK), ROUTE_BLOCK)
        bases, fits = _block_windows(base_ref, fits_ref, j, n_blocks)

        def run(win):
            for e, base in enumerate(bases):
                ywin[e * win:(e + 1) * win, :] = y_ref[0, e, pl.ds(base, win), :]
            hits = _one_hots(slot_ref, tok, bases, win)
            out_ref[rows, :] = x1_ref[rows, :] + _dot(hits, ywin[0:len(bases) * win, :], ((0,), (0,)))

        pl.when(fits)(functools.partial(run, SLOT_WIN_SMALL))
        pl.when(jnp.logical_not(fits))(functools.partial(run, SLOT_WIN))
        return carry

    lax.fori_loop(0, blocks_here, block, 0)


def combine(base, fits, slot_t, y, x1, *, batch, seq, tn=512, tt=1024):
    n_exp, rows, d = y.shape[1], y.shape[2], y.shape[3]
    grid_spec = pltpu.PrefetchScalarGridSpec(
        num_scalar_prefetch=2,
        grid=(batch, d // tn, seq // tt),
        in_specs=[pl.BlockSpec((1, n_exp, 1, seq), lambda b, n, t, base, fits: (b, 0, 0, 0)),
                  pl.BlockSpec((1, n_exp, rows, tn), lambda b, n, t, base, fits: (b, 0, 0, n)),
                  pl.BlockSpec((tt, tn), lambda b, n, t, base, fits: (b * (seq // tt) + t, n))],
        out_specs=pl.BlockSpec((tt, tn), lambda b, n, t, base, fits: (b * (seq // tt) + t, n)),
        scratch_shapes=[pltpu.VMEM((n_exp * SLOT_WIN, tn), BF16)],
    )
    return pl.pallas_call(
        functools.partial(_combine_kernel, n_blocks=seq // ROUTE_BLOCK),
        grid_spec=grid_spec,
        out_shape=jax.ShapeDtypeStruct((batch * seq, d), F32),
        compiler_params=pltpu.CompilerParams(
            dimension_semantics=("arbitrary", "arbitrary", "arbitrary"),
            vmem_limit_bytes=V7X_VMEM_LIMIT_BYTES),
    )(base, fits, slot_t, y, x1)


def kernel(x, norm1_w, w_in, attn_q_norm_w, attn_k_norm_w, hgrn_lb_fwd, hgrn_lb_bwd, hgrn_out_norm_w,
           w_out, norm2_w, w_router, w_expert_gate, w_expert_up, w_expert_down):
    batch, seq, d_model = x.shape
    depth = w_in.shape[0]
    cap = max(1, CAPACITY_FACTOR * seq // N_EXPERTS)
    hgrn_col = 3 * ATTN_WIDTH
    lb_f_all = jnp.cumsum(jax.nn.softmax(hgrn_lb_fwd.astype(F32), axis=0), axis=0)
    lb_b_all = jnp.cumsum(jax.nn.softmax(hgrn_lb_bwd.astype(F32), axis=0), axis=0)

    x2d = x.reshape(batch * seq, d_model)
    for l in range(depth):
        proj = in_proj(x2d, norm1_w[l], w_in[l], attn_q_norm_w[l], attn_k_norm_w[l])
        o_f, o_b = hgrn(proj, lb_f_all[l], lb_b_all[l], batch=batch, seq=seq, col0=hgrn_col)
        a_out = attention(proj, batch=batch, seq=seq)
        x1, h_ext, aff = out_proj(x2d, a_out, o_f, o_b, proj, hgrn_out_norm_w[l], w_out[l],
                                  norm2_w[l], w_router[l], hg_col=hgrn_col + 4 * HGRN_WIDTH)
        slot, start = routing(aff, batch=batch, seq=seq, cap=cap)

        def expert_major(t):
            return t.reshape(batch, -1, LANES)[:, :, :N_EXPERTS].transpose(0, 2, 1)

        slot_t = expert_major(slot).reshape(batch, N_EXPERTS, 1, seq)
        start_t = expert_major(start)
        base_t = start_t // SLOT_ALIGN * SLOT_ALIGN
        end_t = jnp.concatenate([start_t[:, :, 1:], jnp.full((batch, N_EXPERTS, 1), cap, jnp.int32)], axis=2)
        fits = (jnp.max(end_t - base_t, axis=1) <= SLOT_WIN_SMALL).astype(jnp.int32).reshape(-1)
        base = base_t.reshape(-1)
        xin = gather(base, fits, slot_t, h_ext, batch=batch, seq=seq, cap=cap)
        y = expert_ffn(xin, w_expert_gate[l], w_expert_up[l], w_expert_down[l], cap=cap)
        x2d = combine(base, fits, slot_t, y, x1, batch=batch, seq=seq)
    return x2d.reshape(batch, seq, d_model)
```

```python
import functools

import jax
import jax.numpy as jnp
from jax import lax
from jax.experimental import pallas as pl
from jax.experimental.pallas import tpu as pltpu

F32 = jnp.float32
BF16 = jnp.bfloat16

NORM_EPS = 1e-6
NEG_BIG = -1e30
ATTN_HEAD_DIM = 64
ATTN_HEADS = 8
ATTN_WIDTH = ATTN_HEADS * ATTN_HEAD_DIM
DILATED_PATTERNS = ((128, 1), (512, 4), (2048, 16))
ALIBI_MAX_BIAS = 8.0
HGRN_DIM = 128
HGRN_HEADS = 4
HGRN_WIDTH = HGRN_HEADS * HGRN_DIM
N_EXPERTS = 16
CAPACITY_FACTOR = 2
V7X_VMEM_LIMIT_BYTES = 56 * 1024 * 1024


def _dot(a, b, dims=((1,), (0,))):
    return lax.dot_general(a, b, (dims, ((), ())), preferred_element_type=F32)


def _dot_nt(a, b):
    return _dot(a, b, ((1,), (1,)))


def _cast_chunks(src_ref, dst_ref, rows=256):
    for r0 in range(0, src_ref.shape[0], rows):
        dst_ref[r0:r0 + rows, :] = src_ref[r0:r0 + rows, :].astype(dst_ref.dtype)


def _in_proj_kernel(x_ref, nw_ref, w32_ref, qw_ref, kw_ref, o_ref, w_ref, *, n_chunk):
    @pl.when(pl.program_id(0) == 0)
    def _():
        _cast_chunks(w32_ref, w_ref)

    x = x_ref[...]
    h = x * lax.rsqrt(jnp.mean(x * x, axis=-1, keepdims=True) + NORM_EPS) * nw_ref[...]
    h = h.astype(BF16)
    n_total = o_ref.shape[1]
    for c in range(n_total // n_chunk):
        cols = slice(c * n_chunk, (c + 1) * n_chunk)
        o_ref[:, cols] = _dot(h, w_ref[:, cols])
    for sl in range(ATTN_WIDTH // LANES):
        cols = slice(sl * LANES, (sl + 1) * LANES)
        o_ref[:, cols] = _head_rms(o_ref[:, cols], qw_ref[:, cols]) * (ATTN_HEAD_DIM ** -0.5)
        kcols = slice(ATTN_WIDTH + sl * LANES, ATTN_WIDTH + (sl + 1) * LANES)
        o_ref[:, kcols] = _head_rms(o_ref[:, kcols], kw_ref[:, cols])


def in_proj(x2d, norm_w, w, qw, kw, *, tm=256, n_chunk=512):
    m, d = x2d.shape
    n = w.shape[1]
    wspec = pl.BlockSpec((1, ATTN_WIDTH), lambda i: (0, 0))
    return pl.pallas_call(
        functools.partial(_in_proj_kernel, n_chunk=n_chunk),
        grid=(m // tm,),
        in_specs=[
            pl.BlockSpec((tm, d), lambda i: (i, 0)),
            pl.BlockSpec((1, d), lambda i: (0, 0)),
            pl.BlockSpec((d, n), lambda i: (0, 0), pipeline_mode=pl.Buffered(1)),
            wspec, wspec,
        ],
        out_specs=pl.BlockSpec((tm, n), lambda i: (i, 0)),
        out_shape=jax.ShapeDtypeStruct((m, n), F32),
        scratch_shapes=[pltpu.VMEM((d, n), BF16)],
        compiler_params=pltpu.CompilerParams(
            dimension_semantics=("arbitrary",), vmem_limit_bytes=V7X_VMEM_LIMIT_BYTES),
    )(x2d, norm_w.reshape(1, d), w,
      jnp.tile(qw, ATTN_HEADS).reshape(1, -1), jnp.tile(kw, ATTN_HEADS).reshape(1, -1))


HGRN_CHUNK = 64
HGRN_SUB = 16
HGRN_MAX_LOG_DECAY = 80.0


def _hgrn_chunk(load, store, lb, st_ref, *, reverse, bounded_decay):
    q_raw, z, v = load()
    c, dk = q_raw.shape
    sub = HGRN_SUB
    q = q_raw * jax.nn.sigmoid(q_raw)
    ez = jnp.exp(-jnp.abs(z))
    inv = 1.0 / (1.0 + ez)
    pos = z >= 0.0
    f = lb + (1.0 - lb) * jnp.where(pos, inv, ez * inv)
    k = (1.0 - lb) * jnp.where(pos, ez * inv, inv)
    g = jnp.log(f)

    row = lax.broadcasted_iota(jnp.int32, (c, c), 0)
    col = lax.broadcasted_iota(jnp.int32, (c, c), 1)
    tri = (col >= row) if reverse else (col <= row)
    tri_bf = jnp.where(tri, 1.0, 0.0).astype(BF16)
    g_hi, g_lo = _split2(g)
    yield
    cum = _dot(tri_bf, g_hi) + _dot(tri_bf, g_lo)
    yield

    def edge(r):
        return cum[r:r + 1, :]

    if reverse:
        ref_half, ref_q_lo, ref_q_hi, ref_end = edge(32), edge(16), edge(48), edge(0)
    else:
        ref_half, ref_q_lo, ref_q_hi, ref_end = edge(31), edge(15), edge(47), edge(c - 1)

    r1 = lax.broadcasted_iota(jnp.int32, (c, 1), 0)
    ref_quarter = jnp.where(r1 < 32, ref_q_lo, ref_q_hi)

    st = st_ref[...]
    v_bf = v.astype(BF16)
    tb, sb = row // sub, col // sub
    if reverse:
        m1 = (tb < 2) & (sb >= 2)
        m2 = ((tb == 0) & (sb == 1)) | ((tb == 2) & (sb == 3))
    else:
        m1 = (tb >= 2) & (sb < 2)
        m2 = ((tb == 1) & (sb == 0)) | ((tb == 3) & (sb == 2))
    if bounded_decay:
        first = (sub - 1) if reverse else 0
        blocks = [slice(b * sub, (b + 1) * sub) for b in range(c // sub)]
        refs_d = [edge(b * sub + first) for b in range(c // sub)]
        q3 = jnp.concatenate([q[rows] * jnp.exp(cum[rows] - d) for rows, d in zip(blocks, refs_d)], axis=0)
        k3 = jnp.concatenate([k[rows] * jnp.exp(d - cum[rows]) for rows, d in zip(blocks, refs_d)], axis=0)

        def scaled(x, factors):
            return jnp.concatenate([x[rows] * f for rows, f in zip(blocks, factors)], axis=0).astype(BF16)

        refs_q = [ref_q_lo, ref_q_lo, ref_q_hi, ref_q_hi]
        qe = scaled(q3, [jnp.exp(d) for d in refs_d])
        ke = scaled(k3, [jnp.exp(ref_end - d) for d in refs_d])
        q1 = scaled(q3, [jnp.exp(jnp.minimum(d - ref_half, 0.0)) for d in refs_d])
        k1 = scaled(k3, [jnp.exp(jnp.minimum(ref_half - d, 0.0)) for d in refs_d])
        q2 = scaled(q3, [jnp.exp(jnp.minimum(d - rq, 0.0)) for d, rq in zip(refs_d, refs_q)])
        k2 = scaled(k3, [jnp.exp(jnp.minimum(rq - d, 0.0)) for d, rq in zip(refs_d, refs_q)])
        q3, k3 = q3.astype(BF16), k3.astype(BF16)
        m3 = (tb == sb) & ((col >= row) if reverse else (col <= row))
    else:
        qe = (q * jnp.exp(cum)).astype(BF16)
        ke = (k * jnp.exp(ref_end - cum)).astype(BF16)
        q1 = (q * jnp.exp(jnp.minimum(cum - ref_half, 0.0))).astype(BF16)
        k1 = (k * jnp.exp(jnp.minimum(ref_half - cum, 0.0))).astype(BF16)
        q2 = (q * jnp.exp(jnp.minimum(cum - ref_quarter, 0.0))).astype(BF16)
        k2 = (k * jnp.exp(jnp.minimum(ref_quarter - cum, 0.0))).astype(BF16)
    yield

    o = _dot_nt(qe, st.astype(BF16))
    st_ref[...] = st * jnp.exp(ref_end) + _dot(v_bf, ke, ((0,), (0,)))
    a = jnp.where(m1, _dot_nt(q1, k1), 0.0) + jnp.where(m2, _dot_nt(q2, k2), 0.0)
    if bounded_decay:
        a = a + jnp.where(m3, _dot_nt(q3, k3), 0.0)
        a_bf = a.astype(BF16)
        yield
        store(o + _dot(a_bf, v_bf))
        return
    yield

    t_loc = lax.broadcasted_iota(jnp.int32, (sub, 1), 0)
    lane = lax.broadcasted_iota(jnp.int32, (sub, c), 1)
    blocks = []
    for blk in range(c // sub):
        rows = slice(blk * sub, (blk + 1) * sub)
        cum_b, q_b = cum[rows], q[rows]
        a_b = jnp.zeros((sub, c), F32)
        for s_loc in range(sub):
            s = blk * sub + s_loc
            keep = (t_loc <= s_loc) if reverse else (t_loc >= s_loc)
            e = jnp.exp(jnp.where(keep, cum_b - cum[s:s + 1, :], NEG_BIG))
            p = (q_b * k[s:s + 1, :]) * e
            a_b = jnp.where(lane == s, jnp.sum(p, axis=-1, keepdims=True), a_b)
        blocks.append(a_b)
    a = a + jnp.concatenate(blocks, axis=0)
    store(o + _dot(a.astype(BF16), v_bf))


def _hgrn_kernel(bounded_ref, qf_ref, zf_ref, vf_ref, qb_ref, zb_ref, vb_ref, lbf_ref, lbb_ref,
                 of_ref, ob_ref, sf_ref, sb_ref):
    @pl.when(pl.program_id(1) == 0)
    def _():
        sf_ref[...] = jnp.zeros_like(sf_ref)
        sb_ref[...] = jnp.zeros_like(sb_ref)

    n_chunks = qf_ref.shape[0] // HGRN_CHUNK

    def run(bounded_decay):
        def body(ci, carry):
            rf = pl.ds(pl.multiple_of(ci * HGRN_CHUNK, HGRN_CHUNK), HGRN_CHUNK)
            rb = pl.ds(pl.multiple_of((n_chunks - 1 - ci) * HGRN_CHUNK, HGRN_CHUNK), HGRN_CHUNK)
            chains = []
            for h in range(HGRN_HEADS):
                cols = slice(h * HGRN_DIM, (h + 1) * HGRN_DIM)
                for rows, q_ref, z_ref, v_ref, lb_ref, o_ref, st_ref, reverse in (
                        (rf, qf_ref, zf_ref, vf_ref, lbf_ref, of_ref, sf_ref, False),
                        (rb, qb_ref, zb_ref, vb_ref, lbb_ref, ob_ref, sb_ref, True)):
                    def load(rows=rows, cols=cols, q_ref=q_ref, z_ref=z_ref, v_ref=v_ref):
                        return q_ref[rows, cols], z_ref[rows, cols], v_ref[rows, cols]

                    def store(o, rows=rows, cols=cols, o_ref=o_ref):
                        o_ref[rows, cols] = o

                    chains.append(_hgrn_chunk(load, store, lb_ref[:, cols], st_ref.at[h],
                                              reverse=reverse, bounded_decay=bounded_decay))
            while chains:
                chains = [ch for ch in chains if next(ch, True) is None]
            return carry

        lax.fori_loop(0, n_chunks, body, 0)

    pl.when(bounded_ref[0] == 1)(functools.partial(run, True))
    pl.when(bounded_ref[0] != 1)(functools.partial(run, False))


def hgrn(proj, lb_f, lb_b, *, batch, seq, col0, tile=512):
    nt = seq // tile
    cb = col0 // HGRN_WIDTH
    worst = -(HGRN_SUB - 1) * jnp.log(jnp.minimum(jnp.min(lb_f), jnp.min(lb_b)))
    bounded = (worst <= HGRN_MAX_LOG_DECAY).astype(jnp.int32).reshape(1)

    def fwd(colblock):
        return pl.BlockSpec((tile, HGRN_WIDTH), lambda b, i, flag: (b * nt + i, cb + colblock))

    def bwd(colblock):
        return pl.BlockSpec((tile, HGRN_WIDTH), lambda b, i, flag: (b * nt + nt - 1 - i, cb + colblock))

    lb_spec = pl.BlockSpec((1, HGRN_WIDTH), lambda b, i, flag: (0, 0))
    out_shape = jax.ShapeDtypeStruct((batch * seq, HGRN_WIDTH), F32)
    state = pltpu.VMEM((HGRN_HEADS, HGRN_DIM, HGRN_DIM), F32)
    grid_spec = pltpu.PrefetchScalarGridSpec(
        num_scalar_prefetch=1,
        grid=(batch, nt),
        in_specs=[fwd(0), fwd(1), fwd(3), bwd(0), bwd(2), bwd(3), lb_spec, lb_spec],
        out_specs=[
            pl.BlockSpec((tile, HGRN_WIDTH), lambda b, i, flag: (b * nt + i, 0)),
            pl.BlockSpec((tile, HGRN_WIDTH), lambda b, i, flag: (b * nt + nt - 1 - i, 0)),
        ],
        scratch_shapes=[state, state],
    )
    return pl.pallas_call(
        _hgrn_kernel,
        grid_spec=grid_spec,
        out_shape=[out_shape, out_shape],
        compiler_params=pltpu.CompilerParams(
            dimension_semantics=("arbitrary", "arbitrary"),
            vmem_limit_bytes=V7X_VMEM_LIMIT_BYTES),
    )(bounded, proj, proj, proj, proj, proj, proj, lb_f.reshape(1, -1), lb_b.reshape(1, -1))


ATTN_HALF = 64
ATTN_QT = 128
LANES = 128
ATTN_SLABS = ATTN_WIDTH // LANES


def _head_rms(xs, w):
    lo = lax.broadcasted_iota(jnp.int32, (1, LANES), 1) < ATTN_HEAD_DIM
    sq = xs * xs
    s_lo = jnp.sum(jnp.where(lo, sq, 0.0), axis=-1, keepdims=True)
    s_hi = jnp.sum(jnp.where(lo, 0.0, sq), axis=-1, keepdims=True)
    ms = jnp.where(lo, s_lo, s_hi) * (1.0 / ATTN_HEAD_DIM)
    return xs * lax.rsqrt(ms + NORM_EPS) * w


def _attn_kernel(shifted_ref, *refs, seq):
    pl.when(shifted_ref[0] == 1)(lambda: _attn_tile(*refs, seq=seq, shifted=True))
    pl.when(shifted_ref[0] != 1)(lambda: _attn_tile(*refs, seq=seq, shifted=False))


def _attn_tile(*refs, seq, shifted):
    n_pat = len(DILATED_PATTERNS)
    ns = ATTN_SLABS
    q_refs, k_refs, kp_refs, kn_refs, v_refs, vp_refs, vn_refs = [refs[g * ns:(g + 1) * ns] for g in range(7)]
    bias_refs = refs[7 * ns:7 * ns + n_pat]
    o_ref, kwin, vwin, qn, s_scr, o_scr, l_scr = refs[7 * ns + n_pat:]
    tile = o_ref.shape[0]
    i = pl.program_id(1)
    lo = lax.broadcasted_iota(jnp.int32, (1, LANES), 1) < ATTN_HEAD_DIM

    for pi, ((_, dil), bias_ref) in enumerate(zip(DILATED_PATTERNS, bias_refs)):
        sub_rows = tile // dil
        qt = min(ATTN_QT, sub_rows)
        kt = qt + 2 * ATTN_HALF
        halo = ATTN_HALF * dil
        kcol = lax.broadcasted_iota(jnp.int32, (1, kt), 1)
        n_sub = sub_rows // qt

        def fill(r, ws, dil=dil, sub_rows=sub_rows, halo=halo):
            def rows_of(ref, start, n):
                return ref[pl.ds(start + r, n, stride=dil), :]

            main = slice(ATTN_HALF, ATTN_HALF + sub_rows)
            after = slice(ATTN_HALF + sub_rows, 2 * ATTN_HALF + sub_rows)
            for sl in range(ns):
                w = ws * ns + sl
                kwin[w, 0:ATTN_HALF, :] = rows_of(kp_refs[sl], tile - halo, ATTN_HALF).astype(BF16)
                kwin[w, main, :] = rows_of(k_refs[sl], 0, sub_rows).astype(BF16)
                kwin[w, after, :] = rows_of(kn_refs[sl], 0, ATTN_HALF).astype(BF16)
                vwin[w, 0:ATTN_HALF, :] = rows_of(vp_refs[sl], tile - halo, ATTN_HALF).astype(BF16)
                vwin[w, main, :] = rows_of(v_refs[sl], 0, sub_rows).astype(BF16)
                vwin[w, after, :] = rows_of(vn_refs[sl], 0, ATTN_HALF).astype(BF16)
                qn[w, 0:sub_rows, :] = rows_of(q_refs[sl], 0, sub_rows).astype(BF16)

        def sub(j, r, ws, ss, pi=pi, dil=dil, bias_ref=bias_ref, qt=qt, kt=kt, kcol=kcol):
            r0 = j * qt if isinstance(j, int) else pl.multiple_of(j * qt, qt)
            lk = (i * tile) // dil + j * qt - ATTN_HALF + kcol
            edge = jnp.where((lk >= 0) & (lk < seq // dil), 0.0, NEG_BIG)
            out_rows = pl.ds(r + j * (qt * dil), qt, stride=dil)
            for sl in range(ns):
                qs = qn[ws * ns + sl, pl.ds(r0, qt), :]
                ks = kwin[ws * ns + sl, pl.ds(r0, kt), :]
                for hh in range(2):
                    qh = jnp.where(lo if hh == 0 else jnp.logical_not(lo), qs, jnp.zeros_like(qs))
                    s_scr[ss * ATTN_HEADS + 2 * sl + hh, 0:qt, 0:kt] = (
                        _dot_nt(qh, ks) + bias_ref[2 * sl + hh] + edge)
            for sl in range(ns):
                vs = vwin[ws * ns + sl, pl.ds(r0, kt), :]
                o_slab = l_slab = None
                for hh in range(2):
                    s = s_scr[ss * ATTN_HEADS + 2 * sl + hh, 0:qt, 0:kt]
                    if shifted:
                        p = jnp.exp(s)
                    else:
                        m = jnp.max(s, axis=-1, keepdims=True)
                        p = jnp.exp(s - m)
                    den = jnp.sum(p, axis=-1, keepdims=True)
                    oh = _dot(p.astype(BF16), vs) * (1.0 / den)
                    lse = jnp.log(den) if shifted else m + jnp.log(den)
                    o_slab = oh if hh == 0 else jnp.where(lo, o_slab, oh)
                    l_slab = lse if hh == 0 else jnp.where(lo, l_slab, lse)
                o_scr[pi * ns + sl, out_rows, :] = o_slab
                l_scr[pi * ns + sl, out_rows, :] = l_slab

        if n_sub >= 2:
            assert n_sub % 2 == 0

            def subsequence(r, carry, fill=fill, sub=sub, n_sub=n_sub):
                fill(r, 0)
                if n_sub == 2:
                    sub(0, r, 0, 0)
                    sub(1, r, 0, 1)
                else:
                    def pair(jj, carry2):
                        sub(2 * jj, r, 0, 0)
                        sub(2 * jj + 1, r, 0, 1)
                        return carry2
                    lax.fori_loop(0, n_sub // 2, pair, 0)
                return carry

            lax.fori_loop(0, dil, subsequence, 0)
        else:
            assert dil % 2 == 0

            def subsequence_pair(rp, carry, fill=fill, sub=sub):
                fill(2 * rp, 0)
                fill(2 * rp + 1, 1)
                sub(0, 2 * rp, 0, 0)
                sub(0, 2 * rp + 1, 1, 1)
                return carry

            lax.fori_loop(0, dil // 2, subsequence_pair, 0)

    def merge(c, carry):
        rows = pl.ds(pl.multiple_of(c * ATTN_QT, ATTN_QT), ATTN_QT)
        for sl in range(ns):
            ls = [l_scr[p * ns + sl, rows, :] for p in range(n_pat)]
            mx = functools.reduce(jnp.maximum, ls)
            ws = [jnp.exp(l - mx) for l in ls]
            num = sum(w * o_scr[p * ns + sl, rows, :] for p, w in enumerate(ws))
            o_ref[rows, sl * LANES:(sl + 1) * LANES] = num / sum(ws)
        return carry

    lax.fori_loop(0, tile // ATTN_QT, merge, 0)


def _attn_bias(dilation, qt, shift):
    slopes = jnp.exp2(-ALIBI_MAX_BIAS * jnp.arange(1, ATTN_HEADS + 1, dtype=F32) / ATTN_HEADS)
    t = jnp.arange(qt)[:, None]
    j = jnp.arange(qt + 2 * ATTN_HALF)[None, :]
    dist = jnp.abs(j - ATTN_HALF - t)
    alibi = -slopes[:, None, None] * (dilation * dist).astype(F32)[None] - shift
    return jnp.where((dist <= ATTN_HALF)[None], alibi, NEG_BIG)


ATTN_MAX_SHIFT = 30.0


def attention(proj, qw, kw, *, batch, seq, tile=1024):
    nt = seq // tile
    n_pat = len(DILATED_PATTERNS)
    assert all(tile % (ATTN_HALF * dil) == 0 for _, dil in DILATED_PATTERNS)
    bound = (1.0 + 2.0 ** -6) * (ATTN_HEAD_DIM ** 0.5) * jnp.max(jnp.abs(qw)) * jnp.max(jnp.abs(kw))
    use_shift = bound <= ATTN_MAX_SHIFT
    shift = jnp.where(use_shift, bound, 0.0).astype(F32)

    def slabs(c, shift_tiles):
        def spec(sl):
            def index(b, i, flag):
                return (b * nt + jnp.clip(i + shift_tiles, 0, nt - 1), c * ATTN_SLABS + sl)
            return pl.BlockSpec((tile, LANES), index)
        return [spec(sl) for sl in range(ATTN_SLABS)]

    groups = [(0, 0), (1, 0), (1, -1), (1, 1), (2, 0), (2, -1), (2, 1)]
    biases = [_attn_bias(dil, min(ATTN_QT, tile // dil), shift) for _, dil in DILATED_PATTERNS]
    bias_specs = [pl.BlockSpec(bias.shape, lambda b, i, flag: (0, 0, 0)) for bias in biases]
    grid_spec = pltpu.PrefetchScalarGridSpec(
        num_scalar_prefetch=1,
        grid=(batch, nt),
        in_specs=[s for c, shift_tiles in groups for s in slabs(c, shift_tiles)] + bias_specs,
        out_specs=pl.BlockSpec((tile, ATTN_WIDTH), lambda b, i, flag: (b * nt + i, 0)),
        scratch_shapes=[pltpu.VMEM((2 * ATTN_SLABS, tile + 2 * ATTN_HALF, LANES), BF16),
                        pltpu.VMEM((2 * ATTN_SLABS, tile + 2 * ATTN_HALF, LANES), BF16),
                        pltpu.VMEM((2 * ATTN_SLABS, tile, LANES), BF16),
                        pltpu.VMEM((2 * ATTN_HEADS, ATTN_QT, ATTN_QT + 2 * ATTN_HALF), F32),
                        pltpu.VMEM((n_pat * ATTN_SLABS, tile, LANES), F32),
                        pltpu.VMEM((n_pat * ATTN_SLABS, tile, LANES), F32)],
    )
    return pl.pallas_call(
        functools.partial(_attn_kernel, seq=seq),
        grid_spec=grid_spec,
        out_shape=jax.ShapeDtypeStruct((batch * seq, ATTN_WIDTH), F32),
        compiler_params=pltpu.CompilerParams(
            dimension_semantics=("arbitrary", "arbitrary"), vmem_limit_bytes=V7X_VMEM_LIMIT_BYTES),
    )(use_shift.astype(jnp.int32).reshape(1), *([proj] * (len(groups) * ATTN_SLABS)), *biases)


def _split2(x):
    hi = x.astype(BF16)
    return hi, (x - hi.astype(F32)).astype(BF16)


def _out_proj_kernel(x_ref, a_ref, of_ref, ob_ref, hg_ref, hw_ref, wo32_ref, n2_ref, wr_ref,
                     x1_ref, h2_ref, aff_ref, wo_ref):
    @pl.when(pl.program_id(0) == 0)
    def _():
        _cast_chunks(wo32_ref, wo_ref)

    a_out = a_ref[...]
    o = of_ref[...] + ob_ref[...]
    hg = hg_ref[...]
    hw = hw_ref[...]
    b_parts = []
    for sl in range(HGRN_HEADS):
        cols = slice(sl * HGRN_DIM, (sl + 1) * HGRN_DIM)
        os_ = o[:, cols]
        y = os_ * lax.rsqrt(jnp.mean(os_ * os_, axis=-1, keepdims=True) + NORM_EPS) * hw
        g = hg[:, cols]
        b_parts.append(y * (g * jax.nn.sigmoid(g)))
    mixed = jnp.concatenate([a_out] + b_parts, axis=-1).astype(BF16)

    x1 = x_ref[...] + _dot(mixed, wo_ref[...])
    x1_ref[...] = x1
    h2 = x1 * lax.rsqrt(jnp.mean(x1 * x1, axis=-1, keepdims=True) + NORM_EPS) * n2_ref[...]
    d = x1.shape[1]
    h_hi, h_lo = _split2(h2)
    h2_ref[:, 0:d] = h_hi

    both = _dot(h_hi, wr_ref[...])
    logits = both[:, 0:LANES] + both[:, LANES:2 * LANES] + _dot(h_lo, wr_ref[:, 0:LANES])
    valid = lax.broadcasted_iota(jnp.int32, (1, LANES), 1) < N_EXPERTS
    logits = jnp.where(valid, logits, NEG_BIG)
    ex = jnp.exp(logits - jnp.max(logits, axis=-1, keepdims=True))
    aff = ex / jnp.sum(ex, axis=-1, keepdims=True)
    aff_ref[...] = aff
    a_hi, a_lo = _split2(aff)
    h2_ref[:, d:d + LANES] = a_hi
    h2_ref[:, d + LANES:d + 2 * LANES] = a_lo


def out_proj(x2d, a_out, o_f, o_b, proj, hgrn_norm_w, w_out, norm2_w, w_router, *, hg_col, tm=256):
    m, d = x2d.shape
    wr = jnp.pad(w_router, ((0, 0), (0, LANES - N_EXPERTS)))
    wr_pair = jnp.concatenate(_split2(wr), axis=1)

    def rows(width, colblock=0):
        return pl.BlockSpec((tm, width), lambda i: (i, colblock))

    def const(shape):
        return pl.BlockSpec(shape, lambda i: (0, 0))

    return pl.pallas_call(
        _out_proj_kernel,
        grid=(m // tm,),
        in_specs=[rows(d), rows(ATTN_WIDTH),
                  rows(HGRN_WIDTH), rows(HGRN_WIDTH), rows(HGRN_WIDTH, hg_col // HGRN_WIDTH),
                  const((1, HGRN_DIM)), const((ATTN_WIDTH + HGRN_WIDTH, d)),
                  const((1, d)), const((d, 2 * LANES))],
        out_specs=[rows(d), rows(d + 2 * LANES), rows(LANES)],
        out_shape=[jax.ShapeDtypeStruct((m, d), F32), jax.ShapeDtypeStruct((m, d + 2 * LANES), BF16),
                   jax.ShapeDtypeStruct((m, LANES), F32)],
        scratch_shapes=[pltpu.VMEM((ATTN_WIDTH + HGRN_WIDTH, d), BF16)],
        compiler_params=pltpu.CompilerParams(
            dimension_semantics=("arbitrary",), vmem_limit_bytes=V7X_VMEM_LIMIT_BYTES),
    )(x2d, a_out, o_f, o_b, proj, hgrn_norm_w.reshape(1, -1), w_out,
      norm2_w.reshape(1, -1), wr_pair)


ROUTE_BLOCK = 256
COUNT_ROWS = 512


def _routing_kernel(aff_ref, slot_ref, start_ref, *, cap):
    seq = aff_ref.shape[0]

    def count(pred):
        def body(c, acc):
            blk = aff_ref[pl.ds(pl.multiple_of(c * COUNT_ROWS, COUNT_ROWS), COUNT_ROWS), :]
            hits = jnp.where(pred(blk), 1, 0).reshape(COUNT_ROWS // 8, 8, LANES)
            return acc + jnp.sum(hits, axis=0)
        acc = lax.fori_loop(0, seq // COUNT_ROWS, body, jnp.zeros((8, LANES), jnp.int32), unroll=4)
        return jnp.sum(acc, axis=0, keepdims=True)

    def bit_step(t, thr_bits):
        cand = thr_bits | jnp.left_shift(jnp.int32(1), 30 - t)
        cand_f = pltpu.bitcast(cand, F32)
        return jnp.where(count(lambda blk: blk >= cand_f) >= cap, cand, thr_bits)

    thr = pltpu.bitcast(lax.fori_loop(0, 31, bit_step, jnp.zeros((1, LANES), jnp.int32)), F32)
    need = (cap - count(lambda blk: blk > thr)).astype(F32)

    row = lax.broadcasted_iota(jnp.int32, (ROUTE_BLOCK, ROUTE_BLOCK), 0)
    col = lax.broadcasted_iota(jnp.int32, (ROUTE_BLOCK, ROUTE_BLOCK), 1)
    before = jnp.where(col < row, 1.0, 0.0).astype(BF16)

    group = 2

    def assign(jg, carry):
        c_eq, c_sel = carry
        starts = [pl.multiple_of((jg * group + u) * ROUTE_BLOCK, ROUTE_BLOCK) for u in range(group)]
        blks = [aff_ref[pl.ds(r0, ROUTE_BLOCK), :] for r0 in starts]
        gts = [blk > thr for blk in blks]
        eqs = [blk == thr for blk in blks]
        eq_fs = [jnp.where(eq, 1.0, 0.0) for eq in eqs]
        eq_pre = [_dot(before, eq_f.astype(BF16)) for eq_f in eq_fs]
        sels = []
        for u in range(group):
            sels.append(gts[u] | (eqs[u] & (eq_pre[u] + c_eq < need)))
            c_eq = c_eq + jnp.sum(eq_fs[u], axis=0, keepdims=True)
        sel_fs = [jnp.where(sel, 1.0, 0.0) for sel in sels]
        sel_pre = [_dot(before, sel_f.astype(BF16)) for sel_f in sel_fs]
        for u in range(group):
            slot_ref[pl.ds(starts[u], ROUTE_BLOCK), :] = jnp.where(sels[u], sel_pre[u] + c_sel, -1.0).astype(jnp.int32)
            start_ref[pl.ds(jg * group + u, 1), :] = c_sel.astype(jnp.int32)
            c_sel = c_sel + jnp.sum(sel_fs[u], axis=0, keepdims=True)
        return c_eq, c_sel

    zero = jnp.zeros((1, LANES), F32)
    lax.fori_loop(0, seq // (ROUTE_BLOCK * group), assign, (zero, zero))


def routing(aff, *, batch, seq, cap):
    nblk = seq // ROUTE_BLOCK
    return pl.pallas_call(
        functools.partial(_routing_kernel, cap=cap),
        grid=(batch,),
        in_specs=[pl.BlockSpec((seq, LANES), lambda b: (b, 0))],
        out_specs=[pl.BlockSpec((seq, LANES), lambda b: (b, 0)), pl.BlockSpec((nblk, LANES), lambda b: (b, 0))],
        out_shape=[jax.ShapeDtypeStruct((batch * seq, LANES), jnp.int32),
                   jax.ShapeDtypeStruct((batch * nblk, LANES), jnp.int32)],
        compiler_params=pltpu.CompilerParams(
            dimension_semantics=("arbitrary",), vmem_limit_bytes=V7X_VMEM_LIMIT_BYTES),
    )(aff)


SLOT_ALIGN = 16
SLOT_WIN = ROUTE_BLOCK + SLOT_ALIGN
SLOT_WIN_SMALL = 64
FFN_ROWS = 256


def _one_hots(slot_ref, tok, bases, win):
    r = lax.broadcasted_iota(jnp.int32, (win, ROUTE_BLOCK), 0)
    return jnp.concatenate(
        [jnp.where((slot_ref[0, e, :, tok] - base) == r, 1.0, 0.0).astype(BF16) for e, base in enumerate(bases)],
        axis=0)


def _block_windows(base_ref, fits_ref, j, n_blocks):
    b = pl.program_id(0)
    n_exp = N_EXPERTS
    bases = [pl.multiple_of(base_ref[(b * n_exp + e) * n_blocks + j], SLOT_ALIGN) for e in range(n_exp)]
    return bases, fits_ref[b * n_blocks + j] == 1


def _gather_kernel(base_ref, fits_ref, slot_ref, h_ref, xin_ref, *, n_blocks):
    xin_ref[...] = jnp.zeros_like(xin_ref)

    def block(j, carry):
        tok = pl.ds(pl.multiple_of(j * ROUTE_BLOCK, ROUTE_BLOCK), ROUTE_BLOCK)
        bases, fits = _block_windows(base_ref, fits_ref, j, n_blocks)

        def run(win):
            rows = _dot(_one_hots(slot_ref, tok, bases, win), h_ref[tok, :]).astype(BF16)
            for e, base in enumerate(bases):
                xin_ref[0, e, pl.ds(base, win), :] += rows[e * win:(e + 1) * win]

        pl.when(fits)(functools.partial(run, SLOT_WIN_SMALL))
        pl.when(jnp.logical_not(fits))(functools.partial(run, SLOT_WIN))
        return carry

    lax.fori_loop(0, n_blocks, block, 0)


def gather(base, fits, slot_t, h_ext, *, batch, seq, cap, tn=256):
    n_exp = slot_t.shape[1]
    width = h_ext.shape[1]
    rows = cap + SLOT_WIN
    grid_spec = pltpu.PrefetchScalarGridSpec(
        num_scalar_prefetch=2,
        grid=(batch, width // tn),
        in_specs=[pl.BlockSpec((1, n_exp, 1, seq), lambda b, n, base, fits: (b, 0, 0, 0)),
                  pl.BlockSpec((seq, tn), lambda b, n, base, fits: (b, n))],
        out_specs=pl.BlockSpec((1, n_exp, rows, tn), lambda b, n, base, fits: (b, 0, 0, n)),
    )
    return pl.pallas_call(
        functools.partial(_gather_kernel, n_blocks=seq // ROUTE_BLOCK),
        grid_spec=grid_spec,
        out_shape=jax.ShapeDtypeStruct((batch, n_exp, rows, width), BF16),
        compiler_params=pltpu.CompilerParams(
            dimension_semantics=("arbitrary", "arbitrary"), vmem_limit_bytes=V7X_VMEM_LIMIT_BYTES),
    )(base, fits, slot_t, h_ext)


def _ffn_kernel(xin_ref, g_ref, wg32_ref, wu32_ref, wd32_ref, y_ref, wg_ref, wu_ref, wd_ref, *, cap):
    e = pl.program_id(0)

    @pl.when(pl.program_id(1) == 0)
    def _():
        _cast_chunks(wg32_ref.at[0], wg_ref)
        _cast_chunks(wu32_ref.at[0], wu_ref)
        _cast_chunks(wd32_ref.at[0], wd_ref)

    lane = lax.broadcasted_iota(jnp.int32, (1, 2 * LANES), 1)
    mine = (lane == e) | (lane == LANES + e)
    for rb in range(cap // FFN_ROWS):
        rows = slice(rb * FFN_ROWS, (rb + 1) * FFN_ROWS)
        xb = xin_ref[0, 0, rows, :]
        gate = jnp.sum(jnp.where(mine, g_ref[0, 0, rows, :].astype(F32), 0.0), axis=-1, keepdims=True)
        gate_h = _dot(xb, wg_ref[...])
        hid = (gate_h * jax.nn.sigmoid(gate_h)) * _dot(xb, wu_ref[...])
        y_ref[0, 0, rows, :] = (_dot(hid.astype(BF16), wd_ref[...]) * gate).astype(BF16)
    y_ref[0, 0, cap:, :] = jnp.zeros((y_ref.shape[2] - cap, y_ref.shape[3]), BF16)


def expert_ffn(xin, wg, wu, wd, *, cap):
    batch, n_exp, rows, width = xin.shape
    _, d, f = wg.shape
    return pl.pallas_call(
        functools.partial(_ffn_kernel, cap=cap),
        grid=(n_exp, batch),
        in_specs=[pl.BlockSpec((1, 1, cap, d), lambda e, b: (b, e, 0, 0)),
                  pl.BlockSpec((1, 1, cap, 2 * LANES), lambda e, b: (b, e, 0, d // (2 * LANES))),
                  pl.BlockSpec((1, d, f), lambda e, b: (e, 0, 0)),
                  pl.BlockSpec((1, d, f), lambda e, b: (e, 0, 0)),
                  pl.BlockSpec((1, f, d), lambda e, b: (e, 0, 0))],
        out_specs=pl.BlockSpec((1, 1, rows, d), lambda e, b: (b, e, 0, 0)),
        out_shape=jax.ShapeDtypeStruct((batch, n_exp, rows, d), BF16),
        scratch_shapes=[pltpu.VMEM((d, f), BF16), pltpu.VMEM((d, f), BF16), pltpu.VMEM((f, d), BF16)],
        compiler_params=pltpu.CompilerParams(
            dimension_semantics=("arbitrary", "arbitrary"), vmem_limit_bytes=V7X_VMEM_LIMIT_BYTES),
    )(xin, xin, wg, wu, wd)


def _combine_kernel(base_ref, fits_ref, slot_ref, y_ref, x1_ref, out_ref, ywin, *, n_blocks):
    blocks_here = out_ref.shape[0] // ROUTE_BLOCK
    first = pl.program_id(2) * blocks_here

    def block(jj, carry):
        j = first + jj
        rows = pl.ds(pl.multiple_of(jj * ROUTE_BLOCK, ROUTE_BLOCK), ROUTE_BLOCK)
        tok = pl.ds(pl.multiple_of(j * ROUTE_BLOCK, ROUTE_BLOCK), ROUTE_BLOCK)
        bases, fits = _block_windows(base_ref, fits_ref, j, n_blocks)

        def run(win):
            for e, base in enumerate(bases):
                ywin[e * win:(e + 1) * win, :] = y_ref[0, e, pl.ds(base, win), :]
            hits = _one_hots(slot_ref, tok, bases, win)
            out_ref[rows, :] = x1_ref[rows, :] + _dot(hits, ywin[0:len(bases) * win, :], ((0,), (0,)))

        pl.when(fits)(functools.partial(run, SLOT_WIN_SMALL))
        pl.when(jnp.logical_not(fits))(functools.partial(run, SLOT_WIN))
        return carry

    lax.fori_loop(0, blocks_here, block, 0)


def combine(base, fits, slot_t, y, x1, *, batch, seq, tn=512, tt=1024):
    n_exp, rows, d = y.shape[1], y.shape[2], y.shape[3]
    grid_spec = pltpu.PrefetchScalarGridSpec(
        num_scalar_prefetch=2,
        grid=(batch, d // tn, seq // tt),
        in_specs=[pl.BlockSpec((1, n_exp, 1, seq), lambda b, n, t, base, fits: (b, 0, 0, 0)),
                  pl.BlockSpec((1, n_exp, rows, tn), lambda b, n, t, base, fits: (b, 0, 0, n)),
                  pl.BlockSpec((tt, tn), lambda b, n, t, base, fits: (b * (seq // tt) + t, n))],
        out_specs=pl.BlockSpec((tt, tn), lambda b, n, t, base, fits: (b * (seq // tt) + t, n)),
        scratch_shapes=[pltpu.VMEM((n_exp * SLOT_WIN, tn), BF16)],
    )
    return pl.pallas_call(
        functools.partial(_combine_kernel, n_blocks=seq // ROUTE_BLOCK),
        grid_spec=grid_spec,
        out_shape=jax.ShapeDtypeStruct((batch * seq, d), F32),
        compiler_params=pltpu.CompilerParams(
            dimension_semantics=("arbitrary", "arbitrary", "arbitrary"),
            vmem_limit_bytes=V7X_VMEM_LIMIT_BYTES),
    )(base, fits, slot_t, y, x1)


def kernel(x, norm1_w, w_in, attn_q_norm_w, attn_k_norm_w, hgrn_lb_fwd, hgrn_lb_bwd, hgrn_out_norm_w,
           w_out, norm2_w, w_router, w_expert_gate, w_expert_up, w_expert_down):
    batch, seq, d_model = x.shape
    depth = w_in.shape[0]
    cap = max(1, CAPACITY_FACTOR * seq // N_EXPERTS)
    hgrn_col = 3 * ATTN_WIDTH
    lb_f_all = jnp.cumsum(jax.nn.softmax(hgrn_lb_fwd.astype(F32), axis=0), axis=0)
    lb_b_all = jnp.cumsum(jax.nn.softmax(hgrn_lb_bwd.astype(F32), axis=0), axis=0)

    x2d = x.reshape(batch * seq, d_model)
    for l in range(depth):
        proj = in_proj(x2d, norm1_w[l], w_in[l], attn_q_norm_w[l], attn_k_norm_w[l])
        o_f, o_b = hgrn(proj, lb_f_all[l], lb_b_all[l], batch=batch, seq=seq, col0=hgrn_col)
        a_out = attention(proj, attn_q_norm_w[l], attn_k_norm_w[l], batch=batch, seq=seq)
        x1, h_ext, aff = out_proj(x2d, a_out, o_f, o_b, proj, hgrn_out_norm_w[l], w_out[l],
                                  norm2_w[l], w_router[l], hg_col=hgrn_col + 4 * HGRN_WIDTH)
        slot, start = routing(aff, batch=batch, seq=seq, cap=cap)

        def expert_major(t):
            return t.reshape(batch, -1, LANES)[:, :, :N_EXPERTS].transpose(0, 2, 1)

        slot_t = expert_major(slot).reshape(batch, N_EXPERTS, 1, seq)
        start_t = expert_major(start)
        base_t = start_t // SLOT_ALIGN * SLOT_ALIGN
        end_t = jnp.concatenate([start_t[:, :, 1:], jnp.full((batch, N_EXPERTS, 1), cap, jnp.int32)], axis=2)
        fits = (jnp.max(end_t - base_t, axis=1) <= SLOT_WIN_SMALL).astype(jnp.int32).reshape(-1)
        base = base_t.reshape(-1)
        xin = gather(base, fits, slot_t, h_ext, batch=batch, seq=seq, cap=cap)
        y = expert_ffn(xin, w_expert_gate[l], w_expert_up[l], w_expert_down[l], cap=cap)
        x2d = combine(base, fits, slot_t, y, x1, batch=batch, seq=seq)
    return x2d.reshape(batch, seq, d_model)
```

```python
import functools

import jax
import jax.numpy as jnp
from jax import lax
from jax.experimental import pallas as pl
from jax.experimental.pallas import tpu as pltpu

F32 = jnp.float32
BF16 = jnp.bfloat16

NORM_EPS = 1e-6
NEG_BIG = -1e30
ATTN_HEAD_DIM = 64
ATTN_HEADS = 8
ATTN_WIDTH = ATTN_HEADS * ATTN_HEAD_DIM
DILATED_PATTERNS = ((128, 1), (512, 4), (2048, 16))
ALIBI_MAX_BIAS = 8.0
HGRN_DIM = 128
HGRN_HEADS = 4
HGRN_WIDTH = HGRN_HEADS * HGRN_DIM
N_EXPERTS = 16
CAPACITY_FACTOR = 2
V7X_VMEM_LIMIT_BYTES = 56 * 1024 * 1024


def _dot(a, b, dims=((1,), (0,))):
    return lax.dot_general(a, b, (dims, ((), ())), preferred_element_type=F32)


def _dot_nt(a, b):
    return _dot(a, b, ((1,), (1,)))


def _cast_chunks(src_ref, dst_ref, rows=256):
    for r0 in range(0, src_ref.shape[0], rows):
        dst_ref[r0:r0 + rows, :] = src_ref[r0:r0 + rows, :].astype(dst_ref.dtype)


def _in_proj_kernel(x_ref, nw_ref, w32_ref, qw_ref, kw_ref, o_ref, w_ref, *, n_chunk):
    @pl.when(pl.program_id(0) == 0)
    def _():
        _cast_chunks(w32_ref, w_ref)

    x = x_ref[...]
    h = x * lax.rsqrt(jnp.mean(x * x, axis=-1, keepdims=True) + NORM_EPS) * nw_ref[...]
    h = h.astype(BF16)
    n_total = o_ref.shape[1]
    for c in range(n_total // n_chunk):
        cols = slice(c * n_chunk, (c + 1) * n_chunk)
        o_ref[:, cols] = _dot(h, w_ref[:, cols])
    for sl in range(ATTN_WIDTH // LANES):
        cols = slice(sl * LANES, (sl + 1) * LANES)
        o_ref[:, cols] = _head_rms(o_ref[:, cols], qw_ref[:, cols]) * (ATTN_HEAD_DIM ** -0.5)
        kcols = slice(ATTN_WIDTH + sl * LANES, ATTN_WIDTH + (sl + 1) * LANES)
        o_ref[:, kcols] = _head_rms(o_ref[:, kcols], kw_ref[:, cols])


def in_proj(x2d, norm_w, w, qw, kw, *, tm=256, n_chunk=512):
    m, d = x2d.shape
    n = w.shape[1]
    wspec = pl.BlockSpec((1, ATTN_WIDTH), lambda i: (0, 0))
    return pl.pallas_call(
        functools.partial(_in_proj_kernel, n_chunk=n_chunk),
        grid=(m // tm,),
        in_specs=[
            pl.BlockSpec((tm, d), lambda i: (i, 0)),
            pl.BlockSpec((1, d), lambda i: (0, 0)),
            pl.BlockSpec((d, n), lambda i: (0, 0), pipeline_mode=pl.Buffered(1)),
            wspec, wspec,
        ],
        out_specs=pl.BlockSpec((tm, n), lambda i: (i, 0)),
        out_shape=jax.ShapeDtypeStruct((m, n), F32),
        scratch_shapes=[pltpu.VMEM((d, n), BF16)],
        compiler_params=pltpu.CompilerParams(
            dimension_semantics=("arbitrary",), vmem_limit_bytes=V7X_VMEM_LIMIT_BYTES),
    )(x2d, norm_w.reshape(1, d), w,
      jnp.tile(qw, ATTN_HEADS).reshape(1, -1), jnp.tile(kw, ATTN_HEADS).reshape(1, -1))


HGRN_CHUNK = 64
HGRN_SUB = 16
HGRN_MAX_LOG_DECAY = 80.0


def _hgrn_chunk(load, store, lb, st_ref, *, reverse, bounded_decay):
    q_raw, z, v = load()
    c, dk = q_raw.shape
    sub = HGRN_SUB
    q = q_raw * jax.nn.sigmoid(q_raw)
    ez = jnp.exp(-jnp.abs(z))
    inv = 1.0 / (1.0 + ez)
    pos = z >= 0.0
    f = lb + (1.0 - lb) * jnp.where(pos, inv, ez * inv)
    k = (1.0 - lb) * jnp.where(pos, ez * inv, inv)
    g = jnp.log(f)

    row = lax.broadcasted_iota(jnp.int32, (c, c), 0)
    col = lax.broadcasted_iota(jnp.int32, (c, c), 1)
    tri = (col >= row) if reverse else (col <= row)
    tri_bf = jnp.where(tri, 1.0, 0.0).astype(BF16)
    g_hi, g_lo = _split2(g)
    yield
    cum = _dot(tri_bf, g_hi) + _dot(tri_bf, g_lo)
    yield

    def edge(r):
        return cum[r:r + 1, :]

    if reverse:
        ref_half, ref_q_lo, ref_q_hi, ref_end = edge(32), edge(16), edge(48), edge(0)
    else:
        ref_half, ref_q_lo, ref_q_hi, ref_end = edge(31), edge(15), edge(47), edge(c - 1)

    r1 = lax.broadcasted_iota(jnp.int32, (c, 1), 0)
    ref_quarter = jnp.where(r1 < 32, ref_q_lo, ref_q_hi)

    st = st_ref[...]
    v_bf = v.astype(BF16)
    tb, sb = row // sub, col // sub
    if reverse:
        m1 = (tb < 2) & (sb >= 2)
        m2 = ((tb == 0) & (sb == 1)) | ((tb == 2) & (sb == 3))
    else:
        m1 = (tb >= 2) & (sb < 2)
        m2 = ((tb == 1) & (sb == 0)) | ((tb == 3) & (sb == 2))
    if bounded_decay:
        first = (sub - 1) if reverse else 0
        blocks = [slice(b * sub, (b + 1) * sub) for b in range(c // sub)]
        refs_d = [edge(b * sub + first) for b in range(c // sub)]
        q3 = jnp.concatenate([q[rows] * jnp.exp(cum[rows] - d) for rows, d in zip(blocks, refs_d)], axis=0)
        k3 = jnp.concatenate([k[rows] * jnp.exp(d - cum[rows]) for rows, d in zip(blocks, refs_d)], axis=0)

        def scaled(x, factors):
            return jnp.concatenate([x[rows] * f for rows, f in zip(blocks, factors)], axis=0).astype(BF16)

        refs_q = [ref_q_lo, ref_q_lo, ref_q_hi, ref_q_hi]
        qe = scaled(q3, [jnp.exp(d) for d in refs_d])
        ke = scaled(k3, [jnp.exp(ref_end - d) for d in refs_d])
        q1 = scaled(q3, [jnp.exp(jnp.minimum(d - ref_half, 0.0)) for d in refs_d])
        k1 = scaled(k3, [jnp.exp(jnp.minimum(ref_half - d, 0.0)) for d in refs_d])
        q2 = scaled(q3, [jnp.exp(jnp.minimum(d - rq, 0.0)) for d, rq in zip(refs_d, refs_q)])
        k2 = scaled(k3, [jnp.exp(jnp.minimum(rq - d, 0.0)) for d, rq in zip(refs_d, refs_q)])
        q3, k3 = q3.astype(BF16), k3.astype(BF16)
        m3 = (tb == sb) & ((col >= row) if reverse else (col <= row))
    else:
        qe = (q * jnp.exp(cum)).astype(BF16)
        ke = (k * jnp.exp(ref_end - cum)).astype(BF16)
        q1 = (q * jnp.exp(jnp.minimum(cum - ref_half, 0.0))).astype(BF16)
        k1 = (k * jnp.exp(jnp.minimum(ref_half - cum, 0.0))).astype(BF16)
        q2 = (q * jnp.exp(jnp.minimum(cum - ref_quarter, 0.0))).astype(BF16)
        k2 = (k * jnp.exp(jnp.minimum(ref_quarter - cum, 0.0))).astype(BF16)
    yield

    o = _dot_nt(qe, st.astype(BF16))
    st_ref[...] = st * jnp.exp(ref_end) + _dot(v_bf, ke, ((0,), (0,)))
    a = jnp.where(m1, _dot_nt(q1, k1), 0.0) + jnp.where(m2, _dot_nt(q2, k2), 0.0)
    if bounded_decay:
        a = a + jnp.where(m3, _dot_nt(q3, k3), 0.0)
        a_bf = a.astype(BF16)
        yield
        store(o + _dot(a_bf, v_bf))
        return
    yield

    t_loc = lax.broadcasted_iota(jnp.int32, (sub, 1), 0)
    lane = lax.broadcasted_iota(jnp.int32, (sub, c), 1)
    blocks = []
    for blk in range(c // sub):
        rows = slice(blk * sub, (blk + 1) * sub)
        cum_b, q_b = cum[rows], q[rows]
        a_b = jnp.zeros((sub, c), F32)
        for s_loc in range(sub):
            s = blk * sub + s_loc
            keep = (t_loc <= s_loc) if reverse else (t_loc >= s_loc)
            e = jnp.exp(jnp.where(keep, cum_b - cum[s:s + 1, :], NEG_BIG))
            p = (q_b * k[s:s + 1, :]) * e
            a_b = jnp.where(lane == s, jnp.sum(p, axis=-1, keepdims=True), a_b)
        blocks.append(a_b)
    a = a + jnp.concatenate(blocks, axis=0)
    store(o + _dot(a.astype(BF16), v_bf))


def _hgrn_kernel(bounded_ref, qf_ref, zf_ref, vf_ref, qb_ref, zb_ref, vb_ref, lbf_ref, lbb_ref,
                 of_ref, ob_ref, sf_ref, sb_ref):
    @pl.when(pl.program_id(1) == 0)
    def _():
        sf_ref[...] = jnp.zeros_like(sf_ref)
        sb_ref[...] = jnp.zeros_like(sb_ref)

    n_chunks = qf_ref.shape[0] // HGRN_CHUNK

    def run(bounded_decay):
        def body(ci, carry):
            rf = pl.ds(pl.multiple_of(ci * HGRN_CHUNK, HGRN_CHUNK), HGRN_CHUNK)
            rb = pl.ds(pl.multiple_of((n_chunks - 1 - ci) * HGRN_CHUNK, HGRN_CHUNK), HGRN_CHUNK)
            chains = []
            for h in range(HGRN_HEADS):
                cols = slice(h * HGRN_DIM, (h + 1) * HGRN_DIM)
                for rows, q_ref, z_ref, v_ref, lb_ref, o_ref, st_ref, reverse in (
                        (rf, qf_ref, zf_ref, vf_ref, lbf_ref, of_ref, sf_ref, False),
                        (rb, qb_ref, zb_ref, vb_ref, lbb_ref, ob_ref, sb_ref, True)):
                    def load(rows=rows, cols=cols, q_ref=q_ref, z_ref=z_ref, v_ref=v_ref):
                        return q_ref[rows, cols], z_ref[rows, cols], v_ref[rows, cols]

                    def store(o, rows=rows, cols=cols, o_ref=o_ref):
                        o_ref[rows, cols] = o

                    chains.append(_hgrn_chunk(load, store, lb_ref[:, cols], st_ref.at[h],
                                              reverse=reverse, bounded_decay=bounded_decay))
            while chains:
                chains = [ch for ch in chains if next(ch, True) is None]
            return carry

        lax.fori_loop(0, n_chunks, body, 0)

    pl.when(bounded_ref[0] == 1)(functools.partial(run, True))
    pl.when(bounded_ref[0] != 1)(functools.partial(run, False))


def hgrn(proj, lb_f, lb_b, *, batch, seq, col0, tile=512):
    nt = seq // tile
    cb = col0 // HGRN_WIDTH
    worst = -(HGRN_SUB - 1) * jnp.log(jnp.minimum(jnp.min(lb_f), jnp.min(lb_b)))
    bounded = (worst <= HGRN_MAX_LOG_DECAY).astype(jnp.int32).reshape(1)

    def fwd(colblock):
        return pl.BlockSpec((tile, HGRN_WIDTH), lambda b, i, flag: (b * nt + i, cb + colblock))

    def bwd(colblock):
        return pl.BlockSpec((tile, HGRN_WIDTH), lambda b, i, flag: (b * nt + nt - 1 - i, cb + colblock))

    lb_spec = pl.BlockSpec((1, HGRN_WIDTH), lambda b, i, flag: (0, 0))
    out_shape = jax.ShapeDtypeStruct((batch * seq, HGRN_WIDTH), F32)
    state = pltpu.VMEM((HGRN_HEADS, HGRN_DIM, HGRN_DIM), F32)
    grid_spec = pltpu.PrefetchScalarGridSpec(
        num_scalar_prefetch=1,
        grid=(batch, nt),
        in_specs=[fwd(0), fwd(1), fwd(3), bwd(0), bwd(2), bwd(3), lb_spec, lb_spec],
        out_specs=[
            pl.BlockSpec((tile, HGRN_WIDTH), lambda b, i, flag: (b * nt + i, 0)),
            pl.BlockSpec((tile, HGRN_WIDTH), lambda b, i, flag: (b * nt + nt - 1 - i, 0)),
        ],
        scratch_shapes=[state, state],
    )
    return pl.pallas_call(
        _hgrn_kernel,
        grid_spec=grid_spec,
        out_shape=[out_shape, out_shape],
        compiler_params=pltpu.CompilerParams(
            dimension_semantics=("arbitrary", "arbitrary"),
            vmem_limit_bytes=V7X_VMEM_LIMIT_BYTES),
    )(bounded, proj, proj, proj, proj, proj, proj, lb_f.reshape(1, -1), lb_b.reshape(1, -1))


ATTN_HALF = 64
ATTN_QT = 128
LANES = 128
ATTN_SLABS = ATTN_WIDTH // LANES


def _head_rms(xs, w):
    lo = lax.broadcasted_iota(jnp.int32, (1, LANES), 1) < ATTN_HEAD_DIM
    sq = xs * xs
    s_lo = jnp.sum(jnp.where(lo, sq, 0.0), axis=-1, keepdims=True)
    s_hi = jnp.sum(jnp.where(lo, 0.0, sq), axis=-1, keepdims=True)
    ms = jnp.where(lo, s_lo, s_hi) * (1.0 / ATTN_HEAD_DIM)
    return xs * lax.rsqrt(ms + NORM_EPS) * w


def _attn_kernel(shifted_ref, *refs, seq):
    pl.when(shifted_ref[0] == 1)(lambda: _attn_tile(*refs, seq=seq, shifted=True))
    pl.when(shifted_ref[0] != 1)(lambda: _attn_tile(*refs, seq=seq, shifted=False))


def _attn_tile(*refs, seq, shifted):
    n_pat = len(DILATED_PATTERNS)
    ns = ATTN_SLABS
    q_refs, k_refs, kp_refs, kn_refs, v_refs, vp_refs, vn_refs = [refs[g * ns:(g + 1) * ns] for g in range(7)]
    bias_refs = refs[7 * ns:7 * ns + n_pat]
    o_ref, kwin, vwin, qn, s_scr, o_scr, l_scr = refs[7 * ns + n_pat:]
    tile = o_ref.shape[0]
    i = pl.program_id(1)
    lo = lax.broadcasted_iota(jnp.int32, (1, LANES), 1) < ATTN_HEAD_DIM

    for pi, ((_, dil), bias_ref) in enumerate(zip(DILATED_PATTERNS, bias_refs)):
        sub_rows = tile // dil
        qt = min(ATTN_QT, sub_rows)
        kt = qt + 2 * ATTN_HALF
        halo = ATTN_HALF * dil
        kcol = lax.broadcasted_iota(jnp.int32, (1, kt), 1)
        n_sub = sub_rows // qt

        def fill(r, ws, dil=dil, sub_rows=sub_rows, halo=halo):
            def rows_of(ref, start, n):
                return ref[pl.ds(start + r, n, stride=dil), :]

            main = slice(ATTN_HALF, ATTN_HALF + sub_rows)
            after = slice(ATTN_HALF + sub_rows, 2 * ATTN_HALF + sub_rows)
            for sl in range(ns):
                w = ws * ns + sl
                kwin[w, 0:ATTN_HALF, :] = rows_of(kp_refs[sl], tile - halo, ATTN_HALF).astype(BF16)
                kwin[w, main, :] = rows_of(k_refs[sl], 0, sub_rows).astype(BF16)
                kwin[w, after, :] = rows_of(kn_refs[sl], 0, ATTN_HALF).astype(BF16)
                vwin[w, 0:ATTN_HALF, :] = rows_of(vp_refs[sl], tile - halo, ATTN_HALF).astype(BF16)
                vwin[w, main, :] = rows_of(v_refs[sl], 0, sub_rows).astype(BF16)
                vwin[w, after, :] = rows_of(vn_refs[sl], 0, ATTN_HALF).astype(BF16)
                qn[w, 0:sub_rows, :] = rows_of(q_refs[sl], 0, sub_rows).astype(BF16)

        def sub(j, r, ws, ss, pi=pi, dil=dil, bias_ref=bias_ref, qt=qt, kt=kt, kcol=kcol):
            r0 = j * qt if isinstance(j, int) else pl.multiple_of(j * qt, qt)
            lk = (i * tile) // dil + j * qt - ATTN_HALF + kcol
            edge = jnp.where((lk >= 0) & (lk < seq // dil), 0.0, NEG_BIG)
            out_rows = pl.ds(r + j * (qt * dil), qt, stride=dil)
            for sl in range(ns):
                qs = qn[ws * ns + sl, pl.ds(r0, qt), :]
                ks = kwin[ws * ns + sl, pl.ds(r0, kt), :]
                for hh in range(2):
                    qh = jnp.where(lo if hh == 0 else jnp.logical_not(lo), qs, jnp.zeros_like(qs))
                    s_scr[ss * ATTN_HEADS + 2 * sl + hh, 0:qt, 0:kt] = (
                        _dot_nt(qh, ks) + bias_ref[2 * sl + hh] + edge)
            for sl in range(ns):
                vs = vwin[ws * ns + sl, pl.ds(r0, kt), :]
                o_slab = l_slab = None
                for hh in range(2):
                    s = s_scr[ss * ATTN_HEADS + 2 * sl + hh, 0:qt, 0:kt]
                    if shifted:
                        p = jnp.exp(s)
                    else:
                        m = jnp.max(s, axis=-1, keepdims=True)
                        p = jnp.exp(s - m)
                    den = jnp.sum(p, axis=-1, keepdims=True)
                    oh = _dot(p.astype(BF16), vs) * (1.0 / den)
                    lse = jnp.log(den) if shifted else m + jnp.log(den)
                    o_slab = oh if hh == 0 else jnp.where(lo, o_slab, oh)
                    l_slab = lse if hh == 0 else jnp.where(lo, l_slab, lse)
                o_scr[pi * ns + sl, out_rows, :] = o_slab
                l_scr[pi * ns + sl, out_rows, :] = l_slab

        if n_sub >= 2:
            assert n_sub % 2 == 0

            def subsequence(r, carry, fill=fill, sub=sub, n_sub=n_sub):
                fill(r, 0)
                if n_sub == 2:
                    sub(0, r, 0, 0)
                    sub(1, r, 0, 1)
                else:
                    def pair(jj, carry2):
                        sub(2 * jj, r, 0, 0)
                        sub(2 * jj + 1, r, 0, 1)
                        return carry2
                    lax.fori_loop(0, n_sub // 2, pair, 0)
                return carry

            lax.fori_loop(0, dil, subsequence, 0)
        else:
            assert dil % 2 == 0

            def subsequence_pair(rp, carry, fill=fill, sub=sub):
                fill(2 * rp, 0)
                fill(2 * rp + 1, 1)
                sub(0, 2 * rp, 0, 0)
                sub(0, 2 * rp + 1, 1, 1)
                return carry

            lax.fori_loop(0, dil // 2, subsequence_pair, 0)

    def merge(c, carry):
        rows = pl.ds(pl.multiple_of(c * ATTN_QT, ATTN_QT), ATTN_QT)
        for sl in range(ns):
            ls = [l_scr[p * ns + sl, rows, :] for p in range(n_pat)]
            mx = functools.reduce(jnp.maximum, ls)
            ws = [jnp.exp(l - mx) for l in ls]
            num = sum(w * o_scr[p * ns + sl, rows, :] for p, w in enumerate(ws))
            o_ref[rows, sl * LANES:(sl + 1) * LANES] = num / sum(ws)
        return carry

    lax.fori_loop(0, tile // ATTN_QT, merge, 0)


def _attn_bias(dilation, qt, shift):
    slopes = jnp.exp2(-ALIBI_MAX_BIAS * jnp.arange(1, ATTN_HEADS + 1, dtype=F32) / ATTN_HEADS)
    t = jnp.arange(qt)[:, None]
    j = jnp.arange(qt + 2 * ATTN_HALF)[None, :]
    dist = jnp.abs(j - ATTN_HALF - t)
    alibi = -slopes[:, None, None] * (dilation * dist).astype(F32)[None] - shift
    return jnp.where((dist <= ATTN_HALF)[None], alibi, NEG_BIG)


ATTN_MAX_SHIFT = 30.0


def attention(proj, qw, kw, *, batch, seq, tile=1024):
    nt = seq // tile
    n_pat = len(DILATED_PATTERNS)
    assert all(tile % (ATTN_HALF * dil) == 0 for _, dil in DILATED_PATTERNS)
    bound = (1.0 + 2.0 ** -6) * (ATTN_HEAD_DIM ** 0.5) * jnp.max(jnp.abs(qw)) * jnp.max(jnp.abs(kw))
    use_shift = bound <= ATTN_MAX_SHIFT
    shift = jnp.where(use_shift, bound, 0.0).astype(F32)

    def slabs(c, shift_tiles):
        def spec(sl):
            def index(b, i, flag):
                return (b * nt + jnp.clip(i + shift_tiles, 0, nt - 1), c * ATTN_SLABS + sl)
            return pl.BlockSpec((tile, LANES), index)
        return [spec(sl) for sl in range(ATTN_SLABS)]

    groups = [(0, 0), (1, 0), (1, -1), (1, 1), (2, 0), (2, -1), (2, 1)]
    biases = [_attn_bias(dil, min(ATTN_QT, tile // dil), shift) for _, dil in DILATED_PATTERNS]
    bias_specs = [pl.BlockSpec(bias.shape, lambda b, i, flag: (0, 0, 0)) for bias in biases]
    grid_spec = pltpu.PrefetchScalarGridSpec(
        num_scalar_prefetch=1,
        grid=(batch, nt),
        in_specs=[s for c, shift_tiles in groups for s in slabs(c, shift_tiles)] + bias_specs,
        out_specs=pl.BlockSpec((tile, ATTN_WIDTH), lambda b, i, flag: (b * nt + i, 0)),
        scratch_shapes=[pltpu.VMEM((2 * ATTN_SLABS, tile + 2 * ATTN_HALF, LANES), BF16),
                        pltpu.VMEM((2 * ATTN_SLABS, tile + 2 * ATTN_HALF, LANES), BF16),
                        pltpu.VMEM((2 * ATTN_SLABS, tile, LANES), BF16),
                        pltpu.VMEM((2 * ATTN_HEADS, ATTN_QT, ATTN_QT + 2 * ATTN_HALF), F32),
                        pltpu.VMEM((n_pat * ATTN_SLABS, tile, LANES), F32),
                        pltpu.VMEM((n_pat * ATTN_SLABS, tile, LANES), F32)],
    )
    return pl.pallas_call(
        functools.partial(_attn_kernel, seq=seq),
        grid_spec=grid_spec,
        out_shape=jax.ShapeDtypeStruct((batch * seq, ATTN_WIDTH), F32),
        compiler_params=pltpu.CompilerParams(
            dimension_semantics=("arbitrary", "arbitrary"), vmem_limit_bytes=V7X_VMEM_LIMIT_BYTES),
    )(use_shift.astype(jnp.int32).reshape(1), *([proj] * (len(groups) * ATTN_SLABS)), *biases)


def _split2(x):
    hi = x.astype(BF16)
    return hi, (x - hi.astype(F32)).astype(BF16)


def _out_proj_kernel(x_ref, a_ref, of_ref, ob_ref, hg_ref, hw_ref, wo32_ref, n2_ref, wr_ref,
                     x1_ref, h2_ref, aff_ref, wo_ref):
    @pl.when(pl.program_id(0) == 0)
    def _():
        _cast_chunks(wo32_ref, wo_ref)

    a_out = a_ref[...]
    o = of_ref[...] + ob_ref[...]
    hg = hg_ref[...]
    hw = hw_ref[...]
    b_parts = []
    for sl in range(HGRN_HEADS):
        cols = slice(sl * HGRN_DIM, (sl + 1) * HGRN_DIM)
        os_ = o[:, cols]
        y = os_ * lax.rsqrt(jnp.mean(os_ * os_, axis=-1, keepdims=True) + NORM_EPS) * hw
        g = hg[:, cols]
        b_parts.append(y * (g * jax.nn.sigmoid(g)))
    mixed = jnp.concatenate([a_out] + b_parts, axis=-1).astype(BF16)

    x1 = x_ref[...] + _dot(mixed, wo_ref[...])
    x1_ref[...] = x1
    h2 = x1 * lax.rsqrt(jnp.mean(x1 * x1, axis=-1, keepdims=True) + NORM_EPS) * n2_ref[...]
    d = x1.shape[1]
    h_hi, h_lo = _split2(h2)
    h2_ref[:, 0:d] = h_hi

    both = _dot(h_hi, wr_ref[...])
    logits = both[:, 0:LANES] + both[:, LANES:2 * LANES] + _dot(h_lo, wr_ref[:, 0:LANES])
    valid = lax.broadcasted_iota(jnp.int32, (1, LANES), 1) < N_EXPERTS
    logits = jnp.where(valid, logits, NEG_BIG)
    ex = jnp.exp(logits - jnp.max(logits, axis=-1, keepdims=True))
    aff = ex / jnp.sum(ex, axis=-1, keepdims=True)
    aff_ref[...] = aff
    a_hi, a_lo = _split2(aff)
    h2_ref[:, d:d + LANES] = a_hi
    h2_ref[:, d + LANES:d + 2 * LANES] = a_lo


def out_proj(x2d, a_out, o_f, o_b, proj, hgrn_norm_w, w_out, norm2_w, w_router, *, hg_col, tm=256):
    m, d = x2d.shape
    wr = jnp.pad(w_router, ((0, 0), (0, LANES - N_EXPERTS)))
    wr_pair = jnp.concatenate(_split2(wr), axis=1)

    def rows(width, colblock=0):
        return pl.BlockSpec((tm, width), lambda i: (i, colblock))

    def const(shape):
        return pl.BlockSpec(shape, lambda i: (0, 0))

    return pl.pallas_call(
        _out_proj_kernel,
        grid=(m // tm,),
        in_specs=[rows(d), rows(ATTN_WIDTH),
                  rows(HGRN_WIDTH), rows(HGRN_WIDTH), rows(HGRN_WIDTH, hg_col // HGRN_WIDTH),
                  const((1, HGRN_DIM)), const((ATTN_WIDTH + HGRN_WIDTH, d)),
                  const((1, d)), const((d, 2 * LANES))],
        out_specs=[rows(d), rows(d + 2 * LANES), rows(LANES)],
        out_shape=[jax.ShapeDtypeStruct((m, d), F32), jax.ShapeDtypeStruct((m, d + 2 * LANES), BF16),
                   jax.ShapeDtypeStruct((m, LANES), F32)],
        scratch_shapes=[pltpu.VMEM((ATTN_WIDTH + HGRN_WIDTH, d), BF16)],
        compiler_params=pltpu.CompilerParams(
            dimension_semantics=("arbitrary",), vmem_limit_bytes=V7X_VMEM_LIMIT_BYTES),
    )(x2d, a_out, o_f, o_b, proj, hgrn_norm_w.reshape(1, -1), w_out,
      norm2_w.reshape(1, -1), wr_pair)


ROUTE_BLOCK = 256
COUNT_ROWS = 512


def _routing_kernel(aff_ref, slot_ref, start_ref, *, cap):
    seq = aff_ref.shape[0]

    def count(pred):
        def body(c, acc):
            blk = aff_ref[pl.ds(pl.multiple_of(c * COUNT_ROWS, COUNT_ROWS), COUNT_ROWS), :]
            hits = jnp.where(pred(blk), 1, 0).reshape(COUNT_ROWS // 8, 8, LANES)
            return acc + jnp.sum(hits, axis=0)
        acc = lax.fori_loop(0, seq // COUNT_ROWS, body, jnp.zeros((8, LANES), jnp.int32), unroll=4)
        return jnp.sum(acc, axis=0, keepdims=True)

    def bit_step(t, thr_bits):
        cand = thr_bits | jnp.left_shift(jnp.int32(1), 30 - t)
        cand_f = pltpu.bitcast(cand, F32)
        return jnp.where(count(lambda blk: blk >= cand_f) >= cap, cand, thr_bits)

    thr = pltpu.bitcast(lax.fori_loop(0, 31, bit_step, jnp.zeros((1, LANES), jnp.int32)), F32)
    need = (cap - count(lambda blk: blk > thr)).astype(F32)

    row = lax.broadcasted_iota(jnp.int32, (ROUTE_BLOCK, ROUTE_BLOCK), 0)
    col = lax.broadcasted_iota(jnp.int32, (ROUTE_BLOCK, ROUTE_BLOCK), 1)
    before = jnp.where(col < row, 1.0, 0.0).astype(BF16)

    group = 2

    def assign(jg, carry):
        c_eq, c_sel = carry
        starts = [pl.multiple_of((jg * group + u) * ROUTE_BLOCK, ROUTE_BLOCK) for u in range(group)]
        blks = [aff_ref[pl.ds(r0, ROUTE_BLOCK), :] for r0 in starts]
        gts = [blk > thr for blk in blks]
        eqs = [blk == thr for blk in blks]
        eq_fs = [jnp.where(eq, 1.0, 0.0) for eq in eqs]
        eq_pre = [_dot(before, eq_f.astype(BF16)) for eq_f in eq_fs]
        sels = []
        for u in range(group):
            sels.append(gts[u] | (eqs[u] & (eq_pre[u] + c_eq < need)))
            c_eq = c_eq + jnp.sum(eq_fs[u], axis=0, keepdims=True)
        sel_fs = [jnp.where(sel, 1.0, 0.0) for sel in sels]
        sel_pre = [_dot(before, sel_f.astype(BF16)) for sel_f in sel_fs]
        for u in range(group):
            slot_ref[pl.ds(starts[u], ROUTE_BLOCK), :] = jnp.where(sels[u], sel_pre[u] + c_sel, -1.0).astype(jnp.int32)
            start_ref[pl.ds(jg * group + u, 1), :] = c_sel.astype(jnp.int32)
            c_sel = c_sel + jnp.sum(sel_fs[u], axis=0, keepdims=True)
        return c_eq, c_sel

    zero = jnp.zeros((1, LANES), F32)
    lax.fori_loop(0, seq // (ROUTE_BLOCK * group), assign, (zero, zero))


def routing(aff, *, batch, seq, cap):
    nblk = seq // ROUTE_BLOCK
    return pl.pallas_call(
        functools.partial(_routing_kernel, cap=cap),
        grid=(batch,),
        in_specs=[pl.BlockSpec((seq, LANES), lambda b: (b, 0))],
        out_specs=[pl.BlockSpec((seq, LANES), lambda b: (b, 0)), pl.BlockSpec((nblk, LANES), lambda b: (b, 0))],
        out_shape=[jax.ShapeDtypeStruct((batch * seq, LANES), jnp.int32),
                   jax.ShapeDtypeStruct((batch * nblk, LANES), jnp.int32)],
        compiler_params=pltpu.CompilerParams(
            dimension_semantics=("arbitrary",), vmem_limit_bytes=V7X_VMEM_LIMIT_BYTES),
    )(aff)


SLOT_ALIGN = 16
SLOT_WIN = ROUTE_BLOCK + SLOT_ALIGN
SLOT_WIN_SMALL = 64
FFN_ROWS = 256


def _one_hots(slot_ref, tok, bases, win):
    r = lax.broadcasted_iota(jnp.int32, (win, ROUTE_BLOCK), 0)
    return jnp.concatenate(
        [jnp.where((slot_ref[0, e, :, tok] - base) == r, 1.0, 0.0).astype(BF16) for e, base in enumerate(bases)],
        axis=0)


BLOCKS_PER_STEP = 4


def _block_windows(base_ref, fits_ref, j, n_blocks):
    b = pl.program_id(0)
    n_exp = N_EXPERTS
    bases = [pl.multiple_of(base_ref[(b * n_exp + e) * n_blocks + j], SLOT_ALIGN) for e in range(n_exp)]
    return bases, fits_ref[b * n_blocks + j] == 1


def _for_block_groups(base_ref, fits_ref, first, n_groups, n_blocks, run):
    def group(g, carry):
        j0 = first + g * BLOCKS_PER_STEP
        js = [j0 + u for u in range(BLOCKS_PER_STEP)]
        windows = [_block_windows(base_ref, fits_ref, j, n_blocks) for j in js]
        blocks = [(j, bases) for j, (bases, _) in zip(js, windows)]
        fits = functools.reduce(jnp.logical_and, [f for _, f in windows])
        pl.when(fits)(functools.partial(run, SLOT_WIN_SMALL, blocks))

        @pl.when(jnp.logical_not(fits))
        def _():
            def one(u, carry2):
                bases, _ = _block_windows(base_ref, fits_ref, j0 + u, n_blocks)
                run(SLOT_WIN, [(j0 + u, bases)])
                return carry2
            lax.fori_loop(0, BLOCKS_PER_STEP, one, 0)

        return carry

    lax.fori_loop(0, n_groups, group, 0)


def _token_rows(j):
    return pl.ds(pl.multiple_of(j * ROUTE_BLOCK, ROUTE_BLOCK), ROUTE_BLOCK)


def _gather_kernel(base_ref, fits_ref, slot_ref, h_ref, xin_ref, *, n_blocks):
    xin_ref[...] = jnp.zeros_like(xin_ref)

    def run(win, blocks):
        rows = [_dot(_one_hots(slot_ref, _token_rows(j), bases, win), h_ref[_token_rows(j), :]).astype(BF16)
                for j, bases in blocks]
        for (j, bases), rows_j in zip(blocks, rows):
            for e, base in enumerate(bases):
                xin_ref[0, e, pl.ds(base, win), :] += rows_j[e * win:(e + 1) * win]

    _for_block_groups(base_ref, fits_ref, 0, n_blocks // BLOCKS_PER_STEP, n_blocks, run)


def gather(base, fits, slot_t, h_ext, *, batch, seq, cap, tn=256):
    n_exp = slot_t.shape[1]
    width = h_ext.shape[1]
    rows = cap + SLOT_WIN
    grid_spec = pltpu.PrefetchScalarGridSpec(
        num_scalar_prefetch=2,
        grid=(batch, width // tn),
        in_specs=[pl.BlockSpec((1, n_exp, 1, seq), lambda b, n, base, fits: (b, 0, 0, 0)),
                  pl.BlockSpec((seq, tn), lambda b, n, base, fits: (b, n))],
        out_specs=pl.BlockSpec((1, n_exp, rows, tn), lambda b, n, base, fits: (b, 0, 0, n)),
    )
    return pl.pallas_call(
        functools.partial(_gather_kernel, n_blocks=seq // ROUTE_BLOCK),
        grid_spec=grid_spec,
        out_shape=jax.ShapeDtypeStruct((batch, n_exp, rows, width), BF16),
        compiler_params=pltpu.CompilerParams(
            dimension_semantics=("arbitrary", "arbitrary"), vmem_limit_bytes=V7X_VMEM_LIMIT_BYTES),
    )(base, fits, slot_t, h_ext)


def _ffn_kernel(xin_ref, g_ref, wg32_ref, wu32_ref, wd32_ref, y_ref, wg_ref, wu_ref, wd_ref, *, cap):
    e = pl.program_id(0)

    @pl.when(pl.program_id(1) == 0)
    def _():
        _cast_chunks(wg32_ref.at[0], wg_ref)
        _cast_chunks(wu32_ref.at[0], wu_ref)
        _cast_chunks(wd32_ref.at[0], wd_ref)

    lane = lax.broadcasted_iota(jnp.int32, (1, 2 * LANES), 1)
    mine = (lane == e) | (lane == LANES + e)
    for rb in range(cap // FFN_ROWS):
        rows = slice(rb * FFN_ROWS, (rb + 1) * FFN_ROWS)
        xb = xin_ref[0, 0, rows, :]
        gate = jnp.sum(jnp.where(mine, g_ref[0, 0, rows, :].astype(F32), 0.0), axis=-1, keepdims=True)
        gate_h = _dot(xb, wg_ref[...])
        hid = (gate_h * jax.nn.sigmoid(gate_h)) * _dot(xb, wu_ref[...])
        y_ref[0, 0, rows, :] = (_dot(hid.astype(BF16), wd_ref[...]) * gate).astype(BF16)
    y_ref[0, 0, cap:, :] = jnp.zeros((y_ref.shape[2] - cap, y_ref.shape[3]), BF16)


def expert_ffn(xin, wg, wu, wd, *, cap):
    batch, n_exp, rows, width = xin.shape
    _, d, f = wg.shape
    return pl.pallas_call(
        functools.partial(_ffn_kernel, cap=cap),
        grid=(n_exp, batch),
        in_specs=[pl.BlockSpec((1, 1, cap, d), lambda e, b: (b, e, 0, 0)),
                  pl.BlockSpec((1, 1, cap, 2 * LANES), lambda e, b: (b, e, 0, d // (2 * LANES))),
                  pl.BlockSpec((1, d, f), lambda e, b: (e, 0, 0)),
                  pl.BlockSpec((1, d, f), lambda e, b: (e, 0, 0)),
                  pl.BlockSpec((1, f, d), lambda e, b: (e, 0, 0))],
        out_specs=pl.BlockSpec((1, 1, rows, d), lambda e, b: (b, e, 0, 0)),
        out_shape=jax.ShapeDtypeStruct((batch, n_exp, rows, d), BF16),
        scratch_shapes=[pltpu.VMEM((d, f), BF16), pltpu.VMEM((d, f), BF16), pltpu.VMEM((f, d), BF16)],
        compiler_params=pltpu.CompilerParams(
            dimension_semantics=("arbitrary", "arbitrary"), vmem_limit_bytes=V7X_VMEM_LIMIT_BYTES),
    )(xin, xin, wg, wu, wd)


def _combine_kernel(base_ref, fits_ref, slot_ref, y_ref, x1_ref, out_ref, ywin, *, n_blocks):
    blocks_here = out_ref.shape[0] // ROUTE_BLOCK
    first = pl.program_id(2) * blocks_here

    def run(win, blocks):
        stacked = N_EXPERTS * win
        separate = stacked * len(blocks) <= ywin.shape[0]
        for u, (j, bases) in enumerate(blocks):
            off = u * stacked if separate else 0
            rows = _token_rows(j - first)
            for e, base in enumerate(bases):
                ywin[off + e * win:off + (e + 1) * win, :] = y_ref[0, e, pl.ds(base, win), :]
            hits = _one_hots(slot_ref, _token_rows(j), bases, win)
            out_ref[rows, :] = x1_ref[rows, :] + _dot(hits, ywin[off:off + stacked, :], ((0,), (0,)))

    _for_block_groups(base_ref, fits_ref, first, blocks_here // BLOCKS_PER_STEP, n_blocks, run)


def combine(base, fits, slot_t, y, x1, *, batch, seq, tn=512, tt=1024):
    n_exp, rows, d = y.shape[1], y.shape[2], y.shape[3]
    grid_spec = pltpu.PrefetchScalarGridSpec(
        num_scalar_prefetch=2,
        grid=(batch, d // tn, seq // tt),
        in_specs=[pl.BlockSpec((1, n_exp, 1, seq), lambda b, n, t, base, fits: (b, 0, 0, 0)),
                  pl.BlockSpec((1, n_exp, rows, tn), lambda b, n, t, base, fits: (b, 0, 0, n)),
                  pl.BlockSpec((tt, tn), lambda b, n, t, base, fits: (b * (seq // tt) + t, n))],
        out_specs=pl.BlockSpec((tt, tn), lambda b, n, t, base, fits: (b * (seq // tt) + t, n)),
        scratch_shapes=[pltpu.VMEM((n_exp * SLOT_WIN, tn), BF16)],
    )
    return pl.pallas_call(
        functools.partial(_combine_kernel, n_blocks=seq // ROUTE_BLOCK),
        grid_spec=grid_spec,
        out_shape=jax.ShapeDtypeStruct((batch * seq, d), F32),
        compiler_params=pltpu.CompilerParams(
            dimension_semantics=("arbitrary", "arbitrary", "arbitrary"),
            vmem_limit_bytes=V7X_VMEM_LIMIT_BYTES),
    )(base, fits, slot_t, y, x1)


def kernel(x, norm1_w, w_in, attn_q_norm_w, attn_k_norm_w, hgrn_lb_fwd, hgrn_lb_bwd, hgrn_out_norm_w,
           w_out, norm2_w, w_router, w_expert_gate, w_expert_up, w_expert_down):
    batch, seq, d_model = x.shape
    depth = w_in.shape[0]
    cap = max(1, CAPACITY_FACTOR * seq // N_EXPERTS)
    hgrn_col = 3 * ATTN_WIDTH
    lb_f_all = jnp.cumsum(jax.nn.softmax(hgrn_lb_fwd.astype(F32), axis=0), axis=0)
    lb_b_all = jnp.cumsum(jax.nn.softmax(hgrn_lb_bwd.astype(F32), axis=0), axis=0)

    x2d = x.reshape(batch * seq, d_model)
    for l in range(depth):
        proj = in_proj(x2d, norm1_w[l], w_in[l], attn_q_norm_w[l], attn_k_norm_w[l])
        o_f, o_b = hgrn(proj, lb_f_all[l], lb_b_all[l], batch=batch, seq=seq, col0=hgrn_col)
        a_out = attention(proj, attn_q_norm_w[l], attn_k_norm_w[l], batch=batch, seq=seq)
        x1, h_ext, aff = out_proj(x2d, a_out, o_f, o_b, proj, hgrn_out_norm_w[l], w_out[l],
                                  norm2_w[l], w_router[l], hg_col=hgrn_col + 4 * HGRN_WIDTH)
        slot, start = routing(aff, batch=batch, seq=seq, cap=cap)

        def expert_major(t):
            return t.reshape(batch, -1, LANES)[:, :, :N_EXPERTS].transpose(0, 2, 1)

        slot_t = expert_major(slot).reshape(batch, N_EXPERTS, 1, seq)
        start_t = expert_major(start)
        base_t = start_t // SLOT_ALIGN * SLOT_ALIGN
        end_t = jnp.concatenate([start_t[:, :, 1:], jnp.full((batch, N_EXPERTS, 1), cap, jnp.int32)], axis=2)
        fits = (jnp.max(end_t - base_t, axis=1) <= SLOT_WIN_SMALL).astype(jnp.int32).reshape(-1)
        base = base_t.reshape(-1)
        xin = gather(base, fits, slot_t, h_ext, batch=batch, seq=seq, cap=cap)
        y = expert_ffn(xin, w_expert_gate[l], w_expert_up[l], w_expert_down[l], cap=cap)
        x2d = combine(base, fits, slot_t, y, x1, batch=batch, seq=seq)
    return x2d.reshape(batch, seq, d_model)
```

```python
import functools

import jax
import jax.numpy as jnp
from jax import lax
from jax.experimental import pallas as pl
from jax.experimental.pallas import tpu as pltpu

F32 = jnp.float32
BF16 = jnp.bfloat16

NORM_EPS = 1e-6
NEG_BIG = -1e30
ATTN_HEAD_DIM = 64
ATTN_HEADS = 8
ATTN_WIDTH = ATTN_HEADS * ATTN_HEAD_DIM
DILATED_PATTERNS = ((128, 1), (512, 4), (2048, 16))
ALIBI_MAX_BIAS = 8.0
HGRN_DIM = 128
HGRN_HEADS = 4
HGRN_WIDTH = HGRN_HEADS * HGRN_DIM
N_EXPERTS = 16
CAPACITY_FACTOR = 2
V7X_VMEM_LIMIT_BYTES = 56 * 1024 * 1024


def _dot(a, b, dims=((1,), (0,))):
    return lax.dot_general(a, b, (dims, ((), ())), preferred_element_type=F32)


def _dot_nt(a, b):
    return _dot(a, b, ((1,), (1,)))


def _cast_chunks(src_ref, dst_ref, rows=256):
    for r0 in range(0, src_ref.shape[0], rows):
        dst_ref[r0:r0 + rows, :] = src_ref[r0:r0 + rows, :].astype(dst_ref.dtype)


def _in_proj_kernel(x_ref, nw_ref, w32_ref, qw_ref, kw_ref, o_ref, w_ref, *, n_chunk):
    @pl.when(pl.program_id(0) == 0)
    def _():
        _cast_chunks(w32_ref, w_ref)

    x = x_ref[...]
    h = x * lax.rsqrt(jnp.mean(x * x, axis=-1, keepdims=True) + NORM_EPS) * nw_ref[...]
    h = h.astype(BF16)
    n_total = o_ref.shape[1]
    for c in range(n_total // n_chunk):
        cols = slice(c * n_chunk, (c + 1) * n_chunk)
        o_ref[:, cols] = _dot(h, w_ref[:, cols])
    for sl in range(ATTN_WIDTH // LANES):
        cols = slice(sl * LANES, (sl + 1) * LANES)
        o_ref[:, cols] = _head_rms(o_ref[:, cols], qw_ref[:, cols]) * (ATTN_HEAD_DIM ** -0.5)
        kcols = slice(ATTN_WIDTH + sl * LANES, ATTN_WIDTH + (sl + 1) * LANES)
        o_ref[:, kcols] = _head_rms(o_ref[:, kcols], kw_ref[:, cols])


def in_proj(x2d, norm_w, w, qw, kw, *, tm=256, n_chunk=512):
    m, d = x2d.shape
    n = w.shape[1]
    wspec = pl.BlockSpec((1, ATTN_WIDTH), lambda i: (0, 0))
    return pl.pallas_call(
        functools.partial(_in_proj_kernel, n_chunk=n_chunk),
        grid=(m // tm,),
        in_specs=[
            pl.BlockSpec((tm, d), lambda i: (i, 0)),
            pl.BlockSpec((1, d), lambda i: (0, 0)),
            pl.BlockSpec((d, n), lambda i: (0, 0), pipeline_mode=pl.Buffered(1)),
            wspec, wspec,
        ],
        out_specs=pl.BlockSpec((tm, n), lambda i: (i, 0)),
        out_shape=jax.ShapeDtypeStruct((m, n), F32),
        scratch_shapes=[pltpu.VMEM((d, n), BF16)],
        compiler_params=pltpu.CompilerParams(
            dimension_semantics=("arbitrary",), vmem_limit_bytes=V7X_VMEM_LIMIT_BYTES),
    )(x2d, norm_w.reshape(1, d), w,
      jnp.tile(qw, ATTN_HEADS).reshape(1, -1), jnp.tile(kw, ATTN_HEADS).reshape(1, -1))


HGRN_CHUNK = 64
HGRN_SUB = 16
HGRN_MAX_LOG_DECAY = 80.0


def _hgrn_chunk(load, store, lb, st_ref, *, reverse, bounded_decay):
    q_raw, z, v = load()
    c, dk = q_raw.shape
    sub = HGRN_SUB
    q = q_raw * jax.nn.sigmoid(q_raw)
    ez = jnp.exp(-jnp.abs(z))
    inv = 1.0 / (1.0 + ez)
    pos = z >= 0.0
    f = lb + (1.0 - lb) * jnp.where(pos, inv, ez * inv)
    k = (1.0 - lb) * jnp.where(pos, ez * inv, inv)
    g = jnp.log(f)

    row = lax.broadcasted_iota(jnp.int32, (c, c), 0)
    col = lax.broadcasted_iota(jnp.int32, (c, c), 1)
    tri = (col >= row) if reverse else (col <= row)
    tri_bf = jnp.where(tri, 1.0, 0.0).astype(BF16)
    g_hi, g_lo = _split2(g)
    yield
    cum = _dot(tri_bf, g_hi) + _dot(tri_bf, g_lo)
    yield

    def edge(r):
        return cum[r:r + 1, :]

    if reverse:
        ref_half, ref_q_lo, ref_q_hi, ref_end = edge(32), edge(16), edge(48), edge(0)
    else:
        ref_half, ref_q_lo, ref_q_hi, ref_end = edge(31), edge(15), edge(47), edge(c - 1)

    r1 = lax.broadcasted_iota(jnp.int32, (c, 1), 0)
    ref_quarter = jnp.where(r1 < 32, ref_q_lo, ref_q_hi)

    st = st_ref[...]
    v_bf = v.astype(BF16)
    tb, sb = row // sub, col // sub
    if reverse:
        m1 = (tb < 2) & (sb >= 2)
        m2 = ((tb == 0) & (sb == 1)) | ((tb == 2) & (sb == 3))
    else:
        m1 = (tb >= 2) & (sb < 2)
        m2 = ((tb == 1) & (sb == 0)) | ((tb == 3) & (sb == 2))
    if bounded_decay:
        first = (sub - 1) if reverse else 0
        blocks = [slice(b * sub, (b + 1) * sub) for b in range(c // sub)]
        refs_d = [edge(b * sub + first) for b in range(c // sub)]
        q3 = jnp.concatenate([q[rows] * jnp.exp(cum[rows] - d) for rows, d in zip(blocks, refs_d)], axis=0)
        k3 = jnp.concatenate([k[rows] * jnp.exp(d - cum[rows]) for rows, d in zip(blocks, refs_d)], axis=0)

        def scaled(x, factors):
            return jnp.concatenate([x[rows] * f for rows, f in zip(blocks, factors)], axis=0).astype(BF16)

        refs_q = [ref_q_lo, ref_q_lo, ref_q_hi, ref_q_hi]
        qe = scaled(q3, [jnp.exp(d) for d in refs_d])
        ke = scaled(k3, [jnp.exp(ref_end - d) for d in refs_d])
        q1 = scaled(q3, [jnp.exp(jnp.minimum(d - ref_half, 0.0)) for d in refs_d])
        k1 = scaled(k3, [jnp.exp(jnp.minimum(ref_half - d, 0.0)) for d in refs_d])
        q2 = scaled(q3, [jnp.exp(jnp.minimum(d - rq, 0.0)) for d, rq in zip(refs_d, refs_q)])
        k2 = scaled(k3, [jnp.exp(jnp.minimum(rq - d, 0.0)) for d, rq in zip(refs_d, refs_q)])
        q3, k3 = q3.astype(BF16), k3.astype(BF16)
        m3 = (tb == sb) & ((col >= row) if reverse else (col <= row))
    else:
        qe = (q * jnp.exp(cum)).astype(BF16)
        ke = (k * jnp.exp(ref_end - cum)).astype(BF16)
        q1 = (q * jnp.exp(jnp.minimum(cum - ref_half, 0.0))).astype(BF16)
        k1 = (k * jnp.exp(jnp.minimum(ref_half - cum, 0.0))).astype(BF16)
        q2 = (q * jnp.exp(jnp.minimum(cum - ref_quarter, 0.0))).astype(BF16)
        k2 = (k * jnp.exp(jnp.minimum(ref_quarter - cum, 0.0))).astype(BF16)
    yield

    o = _dot_nt(qe, st.astype(BF16))
    st_ref[...] = st * jnp.exp(ref_end) + _dot(v_bf, ke, ((0,), (0,)))
    a = jnp.where(m1, _dot_nt(q1, k1), 0.0) + jnp.where(m2, _dot_nt(q2, k2), 0.0)
    if bounded_decay:
        a = a + jnp.where(m3, _dot_nt(q3, k3), 0.0)
        a_bf = a.astype(BF16)
        yield
        store(o + _dot(a_bf, v_bf))
        return
    yield

    t_loc = lax.broadcasted_iota(jnp.int32, (sub, 1), 0)
    lane = lax.broadcasted_iota(jnp.int32, (sub, c), 1)
    blocks = []
    for blk in range(c // sub):
        rows = slice(blk * sub, (blk + 1) * sub)
        cum_b, q_b = cum[rows], q[rows]
        a_b = jnp.zeros((sub, c), F32)
        for s_loc in range(sub):
            s = blk * sub + s_loc
            keep = (t_loc <= s_loc) if reverse else (t_loc >= s_loc)
            e = jnp.exp(jnp.where(keep, cum_b - cum[s:s + 1, :], NEG_BIG))
            p = (q_b * k[s:s + 1, :]) * e
            a_b = jnp.where(lane == s, jnp.sum(p, axis=-1, keepdims=True), a_b)
        blocks.append(a_b)
    a = a + jnp.concatenate(blocks, axis=0)
    store(o + _dot(a.astype(BF16), v_bf))


def _hgrn_kernel(bounded_ref, qf_ref, zf_ref, vf_ref, qb_ref, zb_ref, vb_ref, lbf_ref, lbb_ref,
                 of_ref, ob_ref, sf_ref, sb_ref):
    @pl.when(pl.program_id(1) == 0)
    def _():
        sf_ref[...] = jnp.zeros_like(sf_ref)
        sb_ref[...] = jnp.zeros_like(sb_ref)

    n_chunks = qf_ref.shape[0] // HGRN_CHUNK

    def run(bounded_decay):
        def body(ci, carry):
            rf = pl.ds(pl.multiple_of(ci * HGRN_CHUNK, HGRN_CHUNK), HGRN_CHUNK)
            rb = pl.ds(pl.multiple_of((n_chunks - 1 - ci) * HGRN_CHUNK, HGRN_CHUNK), HGRN_CHUNK)
            chains = []
            for h in range(HGRN_HEADS):
                cols = slice(h * HGRN_DIM, (h + 1) * HGRN_DIM)
                for rows, q_ref, z_ref, v_ref, lb_ref, o_ref, st_ref, reverse in (
                        (rf, qf_ref, zf_ref, vf_ref, lbf_ref, of_ref, sf_ref, False),
                        (rb, qb_ref, zb_ref, vb_ref, lbb_ref, ob_ref, sb_ref, True)):
                    def load(rows=rows, cols=cols, q_ref=q_ref, z_ref=z_ref, v_ref=v_ref):
                        return q_ref[rows, cols], z_ref[rows, cols], v_ref[rows, cols]

                    def store(o, rows=rows, cols=cols, o_ref=o_ref):
                        o_ref[rows, cols] = o.astype(o_ref.dtype)

                    chains.append(_hgrn_chunk(load, store, lb_ref[:, cols], st_ref.at[h],
                                              reverse=reverse, bounded_decay=bounded_decay))
            while chains:
                chains = [ch for ch in chains if next(ch, True) is None]
            return carry

        lax.fori_loop(0, n_chunks, body, 0)

    pl.when(bounded_ref[0] == 1)(functools.partial(run, True))
    pl.when(bounded_ref[0] != 1)(functools.partial(run, False))


def hgrn(proj, lb_f, lb_b, *, batch, seq, col0, tile=512):
    nt = seq // tile
    cb = col0 // HGRN_WIDTH
    worst = -(HGRN_SUB - 1) * jnp.log(jnp.minimum(jnp.min(lb_f), jnp.min(lb_b)))
    bounded = (worst <= HGRN_MAX_LOG_DECAY).astype(jnp.int32).reshape(1)

    def fwd(colblock):
        return pl.BlockSpec((tile, HGRN_WIDTH), lambda b, i, flag: (b * nt + i, cb + colblock))

    def bwd(colblock):
        return pl.BlockSpec((tile, HGRN_WIDTH), lambda b, i, flag: (b * nt + nt - 1 - i, cb + colblock))

    lb_spec = pl.BlockSpec((1, HGRN_WIDTH), lambda b, i, flag: (0, 0))
    out_shape = jax.ShapeDtypeStruct((batch * seq, HGRN_WIDTH), BF16)
    state = pltpu.VMEM((HGRN_HEADS, HGRN_DIM, HGRN_DIM), F32)
    grid_spec = pltpu.PrefetchScalarGridSpec(
        num_scalar_prefetch=1,
        grid=(batch, nt),
        in_specs=[fwd(0), fwd(1), fwd(3), bwd(0), bwd(2), bwd(3), lb_spec, lb_spec],
        out_specs=[
            pl.BlockSpec((tile, HGRN_WIDTH), lambda b, i, flag: (b * nt + i, 0)),
            pl.BlockSpec((tile, HGRN_WIDTH), lambda b, i, flag: (b * nt + nt - 1 - i, 0)),
        ],
        scratch_shapes=[state, state],
    )
    return pl.pallas_call(
        _hgrn_kernel,
        grid_spec=grid_spec,
        out_shape=[out_shape, out_shape],
        compiler_params=pltpu.CompilerParams(
            dimension_semantics=("arbitrary", "arbitrary"),
            vmem_limit_bytes=V7X_VMEM_LIMIT_BYTES),
    )(bounded, proj, proj, proj, proj, proj, proj, lb_f.reshape(1, -1), lb_b.reshape(1, -1))


ATTN_HALF = 64
ATTN_QT = 128
LANES = 128
ATTN_SLABS = ATTN_WIDTH // LANES


def _head_rms(xs, w):
    lo = lax.broadcasted_iota(jnp.int32, (1, LANES), 1) < ATTN_HEAD_DIM
    sq = xs * xs
    s_lo = jnp.sum(jnp.where(lo, sq, 0.0), axis=-1, keepdims=True)
    s_hi = jnp.sum(jnp.where(lo, 0.0, sq), axis=-1, keepdims=True)
    ms = jnp.where(lo, s_lo, s_hi) * (1.0 / ATTN_HEAD_DIM)
    return xs * lax.rsqrt(ms + NORM_EPS) * w


def _attn_kernel(shifted_ref, *refs, seq):
    pl.when(shifted_ref[0] == 1)(lambda: _attn_tile(*refs, seq=seq, shifted=True))
    pl.when(shifted_ref[0] != 1)(lambda: _attn_tile(*refs, seq=seq, shifted=False))


def _attn_tile(*refs, seq, shifted):
    n_pat = len(DILATED_PATTERNS)
    ns = ATTN_SLABS
    q_refs, k_refs, kp_refs, kn_refs, v_refs, vp_refs, vn_refs = [refs[g * ns:(g + 1) * ns] for g in range(7)]
    bias_refs = refs[7 * ns:7 * ns + n_pat]
    o_ref, kwin, vwin, qn, s_scr, o_scr, l_scr = refs[7 * ns + n_pat:]
    tile = o_ref.shape[0]
    i = pl.program_id(1)
    lo = lax.broadcasted_iota(jnp.int32, (1, LANES), 1) < ATTN_HEAD_DIM

    for pi, ((_, dil), bias_ref) in enumerate(zip(DILATED_PATTERNS, bias_refs)):
        sub_rows = tile // dil
        qt = min(ATTN_QT, sub_rows)
        kt = qt + 2 * ATTN_HALF
        halo = ATTN_HALF * dil
        kcol = lax.broadcasted_iota(jnp.int32, (1, kt), 1)
        n_sub = sub_rows // qt

        def fill(r, ws, dil=dil, sub_rows=sub_rows, halo=halo):
            def rows_of(ref, start, n):
                return ref[pl.ds(start + r, n, stride=dil), :]

            main = slice(ATTN_HALF, ATTN_HALF + sub_rows)
            after = slice(ATTN_HALF + sub_rows, 2 * ATTN_HALF + sub_rows)
            for sl in range(ns):
                w = ws * ns + sl
                kwin[w, 0:ATTN_HALF, :] = rows_of(kp_refs[sl], tile - halo, ATTN_HALF).astype(BF16)
                kwin[w, main, :] = rows_of(k_refs[sl], 0, sub_rows).astype(BF16)
                kwin[w, after, :] = rows_of(kn_refs[sl], 0, ATTN_HALF).astype(BF16)
                vwin[w, 0:ATTN_HALF, :] = rows_of(vp_refs[sl], tile - halo, ATTN_HALF).astype(BF16)
                vwin[w, main, :] = rows_of(v_refs[sl], 0, sub_rows).astype(BF16)
                vwin[w, after, :] = rows_of(vn_refs[sl], 0, ATTN_HALF).astype(BF16)
                qn[w, 0:sub_rows, :] = rows_of(q_refs[sl], 0, sub_rows).astype(BF16)

        def sub(j, r, ws, ss, pi=pi, dil=dil, bias_ref=bias_ref, qt=qt, kt=kt, kcol=kcol):
            r0 = j * qt if isinstance(j, int) else pl.multiple_of(j * qt, qt)
            lk = (i * tile) // dil + j * qt - ATTN_HALF + kcol
            edge = jnp.where((lk >= 0) & (lk < seq // dil), 0.0, NEG_BIG)
            out_rows = pl.ds(r + j * (qt * dil), qt, stride=dil)
            for sl in range(ns):
                qs = qn[ws * ns + sl, pl.ds(r0, qt), :]
                ks = kwin[ws * ns + sl, pl.ds(r0, kt), :]
                for hh in range(2):
                    qh = jnp.where(lo if hh == 0 else jnp.logical_not(lo), qs, jnp.zeros_like(qs))
                    s_scr[ss * ATTN_HEADS + 2 * sl + hh, 0:qt, 0:kt] = (
                        _dot_nt(qh, ks) + bias_ref[2 * sl + hh] + edge)
            for sl in range(ns):
                vs = vwin[ws * ns + sl, pl.ds(r0, kt), :]
                o_slab = l_slab = None
                for hh in range(2):
                    s = s_scr[ss * ATTN_HEADS + 2 * sl + hh, 0:qt, 0:kt]
                    if shifted:
                        p = jnp.exp(s)
                    else:
                        m = jnp.max(s, axis=-1, keepdims=True)
                        p = jnp.exp(s - m)
                    den = jnp.sum(p, axis=-1, keepdims=True)
                    oh = _dot(p.astype(BF16), vs) * (1.0 / den)
                    lse = jnp.log(den) if shifted else m + jnp.log(den)
                    o_slab = oh if hh == 0 else jnp.where(lo, o_slab, oh)
                    l_slab = lse if hh == 0 else jnp.where(lo, l_slab, lse)
                o_scr[pi * ns + sl, out_rows, :] = o_slab
                l_scr[pi * ns + sl, out_rows, :] = l_slab

        if n_sub >= 2:
            assert n_sub % 2 == 0

            def subsequence(r, carry, fill=fill, sub=sub, n_sub=n_sub):
                fill(r, 0)
                if n_sub == 2:
                    sub(0, r, 0, 0)
                    sub(1, r, 0, 1)
                else:
                    def pair(jj, carry2):
                        sub(2 * jj, r, 0, 0)
                        sub(2 * jj + 1, r, 0, 1)
                        return carry2
                    lax.fori_loop(0, n_sub // 2, pair, 0)
                return carry

            lax.fori_loop(0, dil, subsequence, 0)
        else:
            assert dil % 2 == 0

            def subsequence_pair(rp, carry, fill=fill, sub=sub):
                fill(2 * rp, 0)
                fill(2 * rp + 1, 1)
                sub(0, 2 * rp, 0, 0)
                sub(0, 2 * rp + 1, 1, 1)
                return carry

            lax.fori_loop(0, dil // 2, subsequence_pair, 0)

    def merge(c, carry):
        rows = pl.ds(pl.multiple_of(c * ATTN_QT, ATTN_QT), ATTN_QT)
        for sl in range(ns):
            ls = [l_scr[p * ns + sl, rows, :] for p in range(n_pat)]
            mx = functools.reduce(jnp.maximum, ls)
            ws = [jnp.exp(l - mx) for l in ls]
            num = sum(w * o_scr[p * ns + sl, rows, :] for p, w in enumerate(ws))
            o_ref[rows, sl * LANES:(sl + 1) * LANES] = (num / sum(ws)).astype(o_ref.dtype)
        return carry

    lax.fori_loop(0, tile // ATTN_QT, merge, 0)


def _attn_bias(dilation, qt, shift):
    slopes = jnp.exp2(-ALIBI_MAX_BIAS * jnp.arange(1, ATTN_HEADS + 1, dtype=F32) / ATTN_HEADS)
    t = jnp.arange(qt)[:, None]
    j = jnp.arange(qt + 2 * ATTN_HALF)[None, :]
    dist = jnp.abs(j - ATTN_HALF - t)
    alibi = -slopes[:, None, None] * (dilation * dist).astype(F32)[None] - shift
    return jnp.where((dist <= ATTN_HALF)[None], alibi, NEG_BIG)


ATTN_MAX_SHIFT = 30.0


def attention(proj, qw, kw, *, batch, seq, tile=1024):
    nt = seq // tile
    n_pat = len(DILATED_PATTERNS)
    assert all(tile % (ATTN_HALF * dil) == 0 for _, dil in DILATED_PATTERNS)
    bound = (1.0 + 2.0 ** -6) * (ATTN_HEAD_DIM ** 0.5) * jnp.max(jnp.abs(qw)) * jnp.max(jnp.abs(kw))
    use_shift = bound <= ATTN_MAX_SHIFT
    shift = jnp.where(use_shift, bound, 0.0).astype(F32)

    def slabs(c, shift_tiles):
        def spec(sl):
            def index(b, i, flag):
                return (b * nt + jnp.clip(i + shift_tiles, 0, nt - 1), c * ATTN_SLABS + sl)
            return pl.BlockSpec((tile, LANES), index)
        return [spec(sl) for sl in range(ATTN_SLABS)]

    groups = [(0, 0), (1, 0), (1, -1), (1, 1), (2, 0), (2, -1), (2, 1)]
    biases = [_attn_bias(dil, min(ATTN_QT, tile // dil), shift) for _, dil in DILATED_PATTERNS]
    bias_specs = [pl.BlockSpec(bias.shape, lambda b, i, flag: (0, 0, 0)) for bias in biases]
    grid_spec = pltpu.PrefetchScalarGridSpec(
        num_scalar_prefetch=1,
        grid=(batch, nt),
        in_specs=[s for c, shift_tiles in groups for s in slabs(c, shift_tiles)] + bias_specs,
        out_specs=pl.BlockSpec((tile, ATTN_WIDTH), lambda b, i, flag: (b * nt + i, 0)),
        scratch_shapes=[pltpu.VMEM((2 * ATTN_SLABS, tile + 2 * ATTN_HALF, LANES), BF16),
                        pltpu.VMEM((2 * ATTN_SLABS, tile + 2 * ATTN_HALF, LANES), BF16),
                        pltpu.VMEM((2 * ATTN_SLABS, tile, LANES), BF16),
                        pltpu.VMEM((2 * ATTN_HEADS, ATTN_QT, ATTN_QT + 2 * ATTN_HALF), F32),
                        pltpu.VMEM((n_pat * ATTN_SLABS, tile, LANES), F32),
                        pltpu.VMEM((n_pat * ATTN_SLABS, tile, LANES), F32)],
    )
    return pl.pallas_call(
        functools.partial(_attn_kernel, seq=seq),
        grid_spec=grid_spec,
        out_shape=jax.ShapeDtypeStruct((batch * seq, ATTN_WIDTH), BF16),
        compiler_params=pltpu.CompilerParams(
            dimension_semantics=("arbitrary", "arbitrary"), vmem_limit_bytes=V7X_VMEM_LIMIT_BYTES),
    )(use_shift.astype(jnp.int32).reshape(1), *([proj] * (len(groups) * ATTN_SLABS)), *biases)


def _split2(x):
    hi = x.astype(BF16)
    return hi, (x - hi.astype(F32)).astype(BF16)


def _out_proj_kernel(x_ref, a_ref, of_ref, ob_ref, hg_ref, hw_ref, wo32_ref, n2_ref, wr_ref,
                     x1_ref, h2_ref, aff_ref, wo_ref):
    @pl.when(pl.program_id(0) == 0)
    def _():
        _cast_chunks(wo32_ref, wo_ref)

    a_out = a_ref[...]
    o = of_ref[...].astype(F32) + ob_ref[...].astype(F32)
    hg = hg_ref[...]
    hw = hw_ref[...]
    b_parts = []
    for sl in range(HGRN_HEADS):
        cols = slice(sl * HGRN_DIM, (sl + 1) * HGRN_DIM)
        os_ = o[:, cols]
        y = os_ * lax.rsqrt(jnp.mean(os_ * os_, axis=-1, keepdims=True) + NORM_EPS) * hw
        g = hg[:, cols]
        b_parts.append((y * (g * jax.nn.sigmoid(g))).astype(BF16))
    mixed = jnp.concatenate([a_out] + b_parts, axis=-1)

    x1 = x_ref[...] + _dot(mixed, wo_ref[...])
    x1_ref[...] = x1
    h2 = x1 * lax.rsqrt(jnp.mean(x1 * x1, axis=-1, keepdims=True) + NORM_EPS) * n2_ref[...]
    d = x1.shape[1]
    h_hi, h_lo = _split2(h2)
    h2_ref[:, 0:d] = h_hi

    both = _dot(h_hi, wr_ref[...])
    logits = both[:, 0:LANES] + both[:, LANES:2 * LANES] + _dot(h_lo, wr_ref[:, 0:LANES])
    valid = lax.broadcasted_iota(jnp.int32, (1, LANES), 1) < N_EXPERTS
    logits = jnp.where(valid, logits, NEG_BIG)
    ex = jnp.exp(logits - jnp.max(logits, axis=-1, keepdims=True))
    aff = ex / jnp.sum(ex, axis=-1, keepdims=True)
    aff_ref[...] = aff
    a_hi, a_lo = _split2(aff)
    h2_ref[:, d:d + LANES] = a_hi
    h2_ref[:, d + LANES:d + 2 * LANES] = a_lo


def out_proj(x2d, a_out, o_f, o_b, proj, hgrn_norm_w, w_out, norm2_w, w_router, *, hg_col, tm=256):
    m, d = x2d.shape
    wr = jnp.pad(w_router, ((0, 0), (0, LANES - N_EXPERTS)))
    wr_pair = jnp.concatenate(_split2(wr), axis=1)

    def rows(width, colblock=0):
        return pl.BlockSpec((tm, width), lambda i: (i, colblock))

    def const(shape):
        return pl.BlockSpec(shape, lambda i: (0, 0))

    return pl.pallas_call(
        _out_proj_kernel,
        grid=(m // tm,),
        in_specs=[rows(d), rows(ATTN_WIDTH),
                  rows(HGRN_WIDTH), rows(HGRN_WIDTH), rows(HGRN_WIDTH, hg_col // HGRN_WIDTH),
                  const((1, HGRN_DIM)), const((ATTN_WIDTH + HGRN_WIDTH, d)),
                  const((1, d)), const((d, 2 * LANES))],
        out_specs=[rows(d), rows(d + 2 * LANES), rows(LANES)],
        out_shape=[jax.ShapeDtypeStruct((m, d), F32), jax.ShapeDtypeStruct((m, d + 2 * LANES), BF16),
                   jax.ShapeDtypeStruct((m, LANES), F32)],
        scratch_shapes=[pltpu.VMEM((ATTN_WIDTH + HGRN_WIDTH, d), BF16)],
        compiler_params=pltpu.CompilerParams(
            dimension_semantics=("arbitrary",), vmem_limit_bytes=V7X_VMEM_LIMIT_BYTES),
    )(x2d, a_out, o_f, o_b, proj, hgrn_norm_w.reshape(1, -1), w_out,
      norm2_w.reshape(1, -1), wr_pair)


ROUTE_BLOCK = 256
COUNT_ROWS = 512


def _routing_kernel(aff_ref, slot_ref, start_ref, *, cap):
    seq = aff_ref.shape[0]

    def count(pred):
        def body(c, acc):
            blk = aff_ref[pl.ds(pl.multiple_of(c * COUNT_ROWS, COUNT_ROWS), COUNT_ROWS), :]
            hits = jnp.where(pred(blk), 1, 0).reshape(COUNT_ROWS // 8, 8, LANES)
            return acc + jnp.sum(hits, axis=0)
        acc = lax.fori_loop(0, seq // COUNT_ROWS, body, jnp.zeros((8, LANES), jnp.int32), unroll=4)
        return jnp.sum(acc, axis=0, keepdims=True)

    def bit_step(t, thr_bits):
        cand = thr_bits | jnp.left_shift(jnp.int32(1), 30 - t)
        cand_f = pltpu.bitcast(cand, F32)
        return jnp.where(count(lambda blk: blk >= cand_f) >= cap, cand, thr_bits)

    thr = pltpu.bitcast(lax.fori_loop(0, 31, bit_step, jnp.zeros((1, LANES), jnp.int32)), F32)
    need = (cap - count(lambda blk: blk > thr)).astype(F32)

    row = lax.broadcasted_iota(jnp.int32, (ROUTE_BLOCK, ROUTE_BLOCK), 0)
    col = lax.broadcasted_iota(jnp.int32, (ROUTE_BLOCK, ROUTE_BLOCK), 1)
    before = jnp.where(col < row, 1.0, 0.0).astype(BF16)

    group = 2

    def assign(jg, carry):
        c_eq, c_sel = carry
        starts = [pl.multiple_of((jg * group + u) * ROUTE_BLOCK, ROUTE_BLOCK) for u in range(group)]
        blks = [aff_ref[pl.ds(r0, ROUTE_BLOCK), :] for r0 in starts]
        gts = [blk > thr for blk in blks]
        eqs = [blk == thr for blk in blks]
        eq_fs = [jnp.where(eq, 1.0, 0.0) for eq in eqs]
        eq_pre = [_dot(before, eq_f.astype(BF16)) for eq_f in eq_fs]
        sels = []
        for u in range(group):
            sels.append(gts[u] | (eqs[u] & (eq_pre[u] + c_eq < need)))
            c_eq = c_eq + jnp.sum(eq_fs[u], axis=0, keepdims=True)
        sel_fs = [jnp.where(sel, 1.0, 0.0) for sel in sels]
        sel_pre = [_dot(before, sel_f.astype(BF16)) for sel_f in sel_fs]
        for u in range(group):
            slot_ref[pl.ds(starts[u], ROUTE_BLOCK), :] = jnp.where(sels[u], sel_pre[u] + c_sel, -1.0).astype(jnp.int32)
            start_ref[pl.ds(jg * group + u, 1), :] = c_sel.astype(jnp.int32)
            c_sel = c_sel + jnp.sum(sel_fs[u], axis=0, keepdims=True)
        return c_eq, c_sel

    zero = jnp.zeros((1, LANES), F32)
    lax.fori_loop(0, seq // (ROUTE_BLOCK * group), assign, (zero, zero))


def routing(aff, *, batch, seq, cap):
    nblk = seq // ROUTE_BLOCK
    return pl.pallas_call(
        functools.partial(_routing_kernel, cap=cap),
        grid=(batch,),
        in_specs=[pl.BlockSpec((seq, LANES), lambda b: (b, 0))],
        out_specs=[pl.BlockSpec((seq, LANES), lambda b: (b, 0)), pl.BlockSpec((nblk, LANES), lambda b: (b, 0))],
        out_shape=[jax.ShapeDtypeStruct((batch * seq, LANES), jnp.int32),
                   jax.ShapeDtypeStruct((batch * nblk, LANES), jnp.int32)],
        compiler_params=pltpu.CompilerParams(
            dimension_semantics=("arbitrary",), vmem_limit_bytes=V7X_VMEM_LIMIT_BYTES),
    )(aff)


SLOT_ALIGN = 16
SLOT_WIN = ROUTE_BLOCK + SLOT_ALIGN
SLOT_WIN_SMALL = 64
FFN_ROWS = 256


def _one_hots(slot_ref, tok, bases, win):
    r = lax.broadcasted_iota(jnp.int32, (win, ROUTE_BLOCK), 0)
    return jnp.concatenate(
        [jnp.where((slot_ref[0, e, :, tok] - base) == r, 1.0, 0.0).astype(BF16) for e, base in enumerate(bases)],
        axis=0)


BLOCKS_PER_STEP = 4


def _block_windows(base_ref, fits_ref, j, n_blocks):
    b = pl.program_id(0)
    n_exp = N_EXPERTS
    bases = [pl.multiple_of(base_ref[(b * n_exp + e) * n_blocks + j], SLOT_ALIGN) for e in range(n_exp)]
    return bases, fits_ref[b * n_blocks + j] == 1


def _for_block_groups(base_ref, fits_ref, first, n_groups, n_blocks, run):
    def group(g, carry):
        j0 = first + g * BLOCKS_PER_STEP
        js = [j0 + u for u in range(BLOCKS_PER_STEP)]
        windows = [_block_windows(base_ref, fits_ref, j, n_blocks) for j in js]
        blocks = [(j, bases) for j, (bases, _) in zip(js, windows)]
        fits = functools.reduce(jnp.logical_and, [f for _, f in windows])
        pl.when(fits)(functools.partial(run, SLOT_WIN_SMALL, blocks))

        @pl.when(jnp.logical_not(fits))
        def _():
            def one(u, carry2):
                bases, _ = _block_windows(base_ref, fits_ref, j0 + u, n_blocks)
                run(SLOT_WIN, [(j0 + u, bases)])
                return carry2
            lax.fori_loop(0, BLOCKS_PER_STEP, one, 0)

        return carry

    lax.fori_loop(0, n_groups, group, 0)


def _token_rows(j):
    return pl.ds(pl.multiple_of(j * ROUTE_BLOCK, ROUTE_BLOCK), ROUTE_BLOCK)


def _gather_kernel(base_ref, fits_ref, slot_ref, h_ref, xin_ref, *, n_blocks):
    xin_ref[...] = jnp.zeros_like(xin_ref)

    def run(win, blocks):
        rows = [_dot(_one_hots(slot_ref, _token_rows(j), bases, win), h_ref[_token_rows(j), :]).astype(BF16)
                for j, bases in blocks]
        for (j, bases), rows_j in zip(blocks, rows):
            for e, base in enumerate(bases):
                xin_ref[0, e, pl.ds(base, win), :] += rows_j[e * win:(e + 1) * win]

    _for_block_groups(base_ref, fits_ref, 0, n_blocks // BLOCKS_PER_STEP, n_blocks, run)


def gather(base, fits, slot_t, h_ext, *, batch, seq, cap, tn=256):
    n_exp = slot_t.shape[1]
    width = h_ext.shape[1]
    rows = cap + SLOT_WIN
    grid_spec = pltpu.PrefetchScalarGridSpec(
        num_scalar_prefetch=2,
        grid=(batch, width // tn),
        in_specs=[pl.BlockSpec((1, n_exp, 1, seq), lambda b, n, base, fits: (b, 0, 0, 0)),
                  pl.BlockSpec((seq, tn), lambda b, n, base, fits: (b, n))],
        out_specs=pl.BlockSpec((1, n_exp, rows, tn), lambda b, n, base, fits: (b, 0, 0, n)),
    )
    return pl.pallas_call(
        functools.partial(_gather_kernel, n_blocks=seq // ROUTE_BLOCK),
        grid_spec=grid_spec,
        out_shape=jax.ShapeDtypeStruct((batch, n_exp, rows, width), BF16),
        compiler_params=pltpu.CompilerParams(
            dimension_semantics=("arbitrary", "arbitrary"), vmem_limit_bytes=V7X_VMEM_LIMIT_BYTES),
    )(base, fits, slot_t, h_ext)


def _ffn_kernel(xin_ref, g_ref, wg32_ref, wu32_ref, wd32_ref, y_ref, wg_ref, wu_ref, wd_ref, *, cap):
    e = pl.program_id(0)

    @pl.when(pl.program_id(1) == 0)
    def _():
        _cast_chunks(wg32_ref.at[0], wg_ref)
        _cast_chunks(wu32_ref.at[0], wu_ref)
        _cast_chunks(wd32_ref.at[0], wd_ref)

    lane = lax.broadcasted_iota(jnp.int32, (1, 2 * LANES), 1)
    mine = (lane == e) | (lane == LANES + e)
    for rb in range(cap // FFN_ROWS):
        rows = slice(rb * FFN_ROWS, (rb + 1) * FFN_ROWS)
        xb = xin_ref[0, 0, rows, :]
        gate = jnp.sum(jnp.where(mine, g_ref[0, 0, rows, :].astype(F32), 0.0), axis=-1, keepdims=True)
        gate_h = _dot(xb, wg_ref[...])
        hid = (gate_h * jax.nn.sigmoid(gate_h)) * _dot(xb, wu_ref[...])
        y_ref[0, 0, rows, :] = (_dot(hid.astype(BF16), wd_ref[...]) * gate).astype(BF16)
    y_ref[0, 0, cap:, :] = jnp.zeros((y_ref.shape[2] - cap, y_ref.shape[3]), BF16)


def expert_ffn(xin, wg, wu, wd, *, cap):
    batch, n_exp, rows, width = xin.shape
    _, d, f = wg.shape
    return pl.pallas_call(
        functools.partial(_ffn_kernel, cap=cap),
        grid=(n_exp, batch),
        in_specs=[pl.BlockSpec((1, 1, cap, d), lambda e, b: (b, e, 0, 0)),
                  pl.BlockSpec((1, 1, cap, 2 * LANES), lambda e, b: (b, e, 0, d // (2 * LANES))),
                  pl.BlockSpec((1, d, f), lambda e, b: (e, 0, 0)),
                  pl.BlockSpec((1, d, f), lambda e, b: (e, 0, 0)),
                  pl.BlockSpec((1, f, d), lambda e, b: (e, 0, 0))],
        out_specs=pl.BlockSpec((1, 1, rows, d), lambda e, b: (b, e, 0, 0)),
        out_shape=jax.ShapeDtypeStruct((batch, n_exp, rows, d), BF16),
        scratch_shapes=[pltpu.VMEM((d, f), BF16), pltpu.VMEM((d, f), BF16), pltpu.VMEM((f, d), BF16)],
        compiler_params=pltpu.CompilerParams(
            dimension_semantics=("arbitrary", "arbitrary"), vmem_limit_bytes=V7X_VMEM_LIMIT_BYTES),
    )(xin, xin, wg, wu, wd)


def _combine_kernel(base_ref, fits_ref, slot_ref, y_ref, x1_ref, out_ref, ywin, *, n_blocks):
    blocks_here = out_ref.shape[0] // ROUTE_BLOCK
    first = pl.program_id(2) * blocks_here

    def run(win, blocks):
        stacked = N_EXPERTS * win
        separate = stacked * len(blocks) <= ywin.shape[0]
        for u, (j, bases) in enumerate(blocks):
            off = u * stacked if separate else 0
            rows = _token_rows(j - first)
            for e, base in enumerate(bases):
                ywin[off + e * win:off + (e + 1) * win, :] = y_ref[0, e, pl.ds(base, win), :]
            hits = _one_hots(slot_ref, _token_rows(j), bases, win)
            out_ref[rows, :] = x1_ref[rows, :] + _dot(hits, ywin[off:off + stacked, :], ((0,), (0,)))

    _for_block_groups(base_ref, fits_ref, first, blocks_here // BLOCKS_PER_STEP, n_blocks, run)


def combine(base, fits, slot_t, y, x1, *, batch, seq, tn=512, tt=1024):
    n_exp, rows, d = y.shape[1], y.shape[2], y.shape[3]
    grid_spec = pltpu.PrefetchScalarGridSpec(
        num_scalar_prefetch=2,
        grid=(batch, d // tn, seq // tt),
        in_specs=[pl.BlockSpec((1, n_exp, 1, seq), lambda b, n, t, base, fits: (b, 0, 0, 0)),
                  pl.BlockSpec((1, n_exp, rows, tn), lambda b, n, t, base, fits: (b, 0, 0, n)),
                  pl.BlockSpec((tt, tn), lambda b, n, t, base, fits: (b * (seq // tt) + t, n))],
        out_specs=pl.BlockSpec((tt, tn), lambda b, n, t, base, fits: (b * (seq // tt) + t, n)),
        scratch_shapes=[pltpu.VMEM((n_exp * SLOT_WIN, tn), BF16)],
    )
    return pl.pallas_call(
        functools.partial(_combine_kernel, n_blocks=seq // ROUTE_BLOCK),
        grid_spec=grid_spec,
        out_shape=jax.ShapeDtypeStruct((batch * seq, d), F32),
        compiler_params=pltpu.CompilerParams(
            dimension_semantics=("arbitrary", "arbitrary", "arbitrary"),
            vmem_limit_bytes=V7X_VMEM_LIMIT_BYTES),
    )(base, fits, slot_t, y, x1)


def kernel(x, norm1_w, w_in, attn_q_norm_w, attn_k_norm_w, hgrn_lb_fwd, hgrn_lb_bwd, hgrn_out_norm_w,
           w_out, norm2_w, w_router, w_expert_gate, w_expert_up, w_expert_down):
    batch, seq, d_model = x.shape
    depth = w_in.shape[0]
    cap = max(1, CAPACITY_FACTOR * seq // N_EXPERTS)
    hgrn_col = 3 * ATTN_WIDTH
    lb_f_all = jnp.cumsum(jax.nn.softmax(hgrn_lb_fwd.astype(F32), axis=0), axis=0)
    lb_b_all = jnp.cumsum(jax.nn.softmax(hgrn_lb_bwd.astype(F32), axis=0), axis=0)

    x2d = x.reshape(batch * seq, d_model)
    for l in range(depth):
        proj = in_proj(x2d, norm1_w[l], w_in[l], attn_q_norm_w[l], attn_k_norm_w[l])
        o_f, o_b = hgrn(proj, lb_f_all[l], lb_b_all[l], batch=batch, seq=seq, col0=hgrn_col)
        a_out = attention(proj, attn_q_norm_w[l], attn_k_norm_w[l], batch=batch, seq=seq)
        x1, h_ext, aff = out_proj(x2d, a_out, o_f, o_b, proj, hgrn_out_norm_w[l], w_out[l],
                                  norm2_w[l], w_router[l], hg_col=hgrn_col + 4 * HGRN_WIDTH)
        slot, start = routing(aff, batch=batch, seq=seq, cap=cap)

        def expert_major(t):
            return t.reshape(batch, -1, LANES)[:, :, :N_EXPERTS].transpose(0, 2, 1)

        slot_t = expert_major(slot).reshape(batch, N_EXPERTS, 1, seq)
        start_t = expert_major(start)
        base_t = start_t // SLOT_ALIGN * SLOT_ALIGN
        end_t = jnp.concatenate([start_t[:, :, 1:], jnp.full((batch, N_EXPERTS, 1), cap, jnp.int32)], axis=2)
        fits = (jnp.max(end_t - base_t, axis=1) <= SLOT_WIN_SMALL).astype(jnp.int32).reshape(-1)
        base = base_t.reshape(-1)
        xin = gather(base, fits, slot_t, h_ext, batch=batch, seq=seq, cap=cap)
        y = expert_ffn(xin, w_expert_gate[l], w_expert_up[l], w_expert_down[l], cap=cap)
        x2d = combine(base, fits, slot_t, y, x1, batch=batch, seq=seq)
    return x2d.reshape(batch, seq, d_model)
```

```python
import functools

import jax
import jax.numpy as jnp
from jax import lax
from jax.experimental import pallas as pl
from jax.experimental.pallas import tpu as pltpu

F32 = jnp.float32
BF16 = jnp.bfloat16

NORM_EPS = 1e-6
NEG_BIG = -1e30
ATTN_HEAD_DIM = 64
ATTN_HEADS = 8
ATTN_WIDTH = ATTN_HEADS * ATTN_HEAD_DIM
DILATED_PATTERNS = ((128, 1), (512, 4), (2048, 16))
ALIBI_MAX_BIAS = 8.0
HGRN_DIM = 128
HGRN_HEADS = 4
HGRN_WIDTH = HGRN_HEADS * HGRN_DIM
N_EXPERTS = 16
CAPACITY_FACTOR = 2
V7X_VMEM_LIMIT_BYTES = 56 * 1024 * 1024


def _dot(a, b, dims=((1,), (0,))):
    return lax.dot_general(a, b, (dims, ((), ())), preferred_element_type=F32)


def _dot_nt(a, b):
    return _dot(a, b, ((1,), (1,)))


def _cast_chunks(src_ref, dst_ref, rows=256):
    for r0 in range(0, src_ref.shape[0], rows):
        dst_ref[r0:r0 + rows, :] = src_ref[r0:r0 + rows, :].astype(dst_ref.dtype)


def _in_proj_kernel(x_ref, nw_ref, w32_ref, qw_ref, kw_ref, o_ref, w_ref, *, n_chunk):
    @pl.when(pl.program_id(0) == 0)
    def _():
        _cast_chunks(w32_ref, w_ref)

    x = x_ref[...]
    h = x * lax.rsqrt(jnp.mean(x * x, axis=-1, keepdims=True) + NORM_EPS) * nw_ref[...]
    h = h.astype(BF16)
    n_total = o_ref.shape[1]
    for c in range(n_total // n_chunk):
        cols = slice(c * n_chunk, (c + 1) * n_chunk)
        o_ref[:, cols] = _dot(h, w_ref[:, cols])
    for sl in range(ATTN_WIDTH // LANES):
        cols = slice(sl * LANES, (sl + 1) * LANES)
        o_ref[:, cols] = _head_rms(o_ref[:, cols], qw_ref[:, cols]) * (ATTN_HEAD_DIM ** -0.5)
        kcols = slice(ATTN_WIDTH + sl * LANES, ATTN_WIDTH + (sl + 1) * LANES)
        o_ref[:, kcols] = _head_rms(o_ref[:, kcols], kw_ref[:, cols])


def in_proj(x2d, norm_w, w, qw, kw, *, tm=256, n_chunk=512):
    m, d = x2d.shape
    n = w.shape[1]
    wspec = pl.BlockSpec((1, ATTN_WIDTH), lambda i: (0, 0))
    return pl.pallas_call(
        functools.partial(_in_proj_kernel, n_chunk=n_chunk),
        grid=(m // tm,),
        in_specs=[
            pl.BlockSpec((tm, d), lambda i: (i, 0)),
            pl.BlockSpec((1, d), lambda i: (0, 0)),
            pl.BlockSpec((d, n), lambda i: (0, 0), pipeline_mode=pl.Buffered(1)),
            wspec, wspec,
        ],
        out_specs=pl.BlockSpec((tm, n), lambda i: (i, 0)),
        out_shape=jax.ShapeDtypeStruct((m, n), F32),
        scratch_shapes=[pltpu.VMEM((d, n), BF16)],
        compiler_params=pltpu.CompilerParams(
            dimension_semantics=("arbitrary",), vmem_limit_bytes=V7X_VMEM_LIMIT_BYTES),
    )(x2d, norm_w.reshape(1, d), w,
      jnp.tile(qw, ATTN_HEADS).reshape(1, -1), jnp.tile(kw, ATTN_HEADS).reshape(1, -1))


HGRN_CHUNK = 64
HGRN_SUB = 16
HGRN_MAX_LOG_DECAY = 80.0


def _hgrn_chunk(load, store, lb, st_ref, *, reverse, bounded_decay):
    q_raw, z, v = load()
    c, dk = q_raw.shape
    sub = HGRN_SUB
    q = q_raw * jax.nn.sigmoid(q_raw)
    ez = jnp.exp(-jnp.abs(z))
    inv = 1.0 / (1.0 + ez)
    pos = z >= 0.0
    f = lb + (1.0 - lb) * jnp.where(pos, inv, ez * inv)
    k = (1.0 - lb) * jnp.where(pos, ez * inv, inv)
    g = jnp.log(f)

    row = lax.broadcasted_iota(jnp.int32, (c, c), 0)
    col = lax.broadcasted_iota(jnp.int32, (c, c), 1)
    tri = (col >= row) if reverse else (col <= row)
    tri_bf = jnp.where(tri, 1.0, 0.0).astype(BF16)
    g_hi, g_lo = _split2(g)
    yield
    cum = _dot(tri_bf, g_hi) + _dot(tri_bf, g_lo)
    yield

    def edge(r):
        return cum[r:r + 1, :]

    if reverse:
        ref_half, ref_q_lo, ref_q_hi, ref_end = edge(32), edge(16), edge(48), edge(0)
    else:
        ref_half, ref_q_lo, ref_q_hi, ref_end = edge(31), edge(15), edge(47), edge(c - 1)

    r1 = lax.broadcasted_iota(jnp.int32, (c, 1), 0)
    ref_quarter = jnp.where(r1 < 32, ref_q_lo, ref_q_hi)

    v_bf = v.astype(BF16)
    tb, sb = row // sub, col // sub
    if reverse:
        m1 = (tb < 2) & (sb >= 2)
        m2 = ((tb == 0) & (sb == 1)) | ((tb == 2) & (sb == 3))
    else:
        m1 = (tb >= 2) & (sb < 2)
        m2 = ((tb == 1) & (sb == 0)) | ((tb == 3) & (sb == 2))
    if bounded_decay:
        first = (sub - 1) if reverse else 0
        blocks = [slice(b * sub, (b + 1) * sub) for b in range(c // sub)]
        refs_d = [edge(b * sub + first) for b in range(c // sub)]
        q3 = jnp.concatenate([q[rows] * jnp.exp(cum[rows] - d) for rows, d in zip(blocks, refs_d)], axis=0)
        k3 = jnp.concatenate([k[rows] * jnp.exp(d - cum[rows]) for rows, d in zip(blocks, refs_d)], axis=0)

        def scaled(x, factors):
            return jnp.concatenate([x[rows] * f for rows, f in zip(blocks, factors)], axis=0).astype(BF16)

        refs_q = [ref_q_lo, ref_q_lo, ref_q_hi, ref_q_hi]
        qe = scaled(q3, [jnp.exp(d) for d in refs_d])
        ke = scaled(k3, [jnp.exp(ref_end - d) for d in refs_d])
        q1 = scaled(q3, [jnp.exp(jnp.minimum(d - ref_half, 0.0)) for d in refs_d])
        k1 = scaled(k3, [jnp.exp(jnp.minimum(ref_half - d, 0.0)) for d in refs_d])
        q2 = scaled(q3, [jnp.exp(jnp.minimum(d - rq, 0.0)) for d, rq in zip(refs_d, refs_q)])
        k2 = scaled(k3, [jnp.exp(jnp.minimum(rq - d, 0.0)) for d, rq in zip(refs_d, refs_q)])
        q3, k3 = q3.astype(BF16), k3.astype(BF16)
        m3 = (tb == sb) & ((col >= row) if reverse else (col <= row))
    else:
        qe = (q * jnp.exp(cum)).astype(BF16)
        ke = (k * jnp.exp(ref_end - cum)).astype(BF16)
        q1 = (q * jnp.exp(jnp.minimum(cum - ref_half, 0.0))).astype(BF16)
        k1 = (k * jnp.exp(jnp.minimum(ref_half - cum, 0.0))).astype(BF16)
        q2 = (q * jnp.exp(jnp.minimum(cum - ref_quarter, 0.0))).astype(BF16)
        k2 = (k * jnp.exp(jnp.minimum(ref_quarter - cum, 0.0))).astype(BF16)
    yield

    st = st_ref[...]
    o = _dot_nt(qe, st.astype(BF16))
    st_ref[...] = st * jnp.exp(ref_end) + _dot(v_bf, ke, ((0,), (0,)))
    a = jnp.where(m1, _dot_nt(q1, k1), 0.0) + jnp.where(m2, _dot_nt(q2, k2), 0.0)
    if bounded_decay:
        a = a + jnp.where(m3, _dot_nt(q3, k3), 0.0)
        a_bf = a.astype(BF16)
        yield
        store(o + _dot(a_bf, v_bf))
        return
    yield

    t_loc = lax.broadcasted_iota(jnp.int32, (sub, 1), 0)
    lane = lax.broadcasted_iota(jnp.int32, (sub, c), 1)
    blocks = []
    for blk in range(c // sub):
        rows = slice(blk * sub, (blk + 1) * sub)
        cum_b, q_b = cum[rows], q[rows]
        a_b = jnp.zeros((sub, c), F32)
        for s_loc in range(sub):
            s = blk * sub + s_loc
            keep = (t_loc <= s_loc) if reverse else (t_loc >= s_loc)
            e = jnp.exp(jnp.where(keep, cum_b - cum[s:s + 1, :], NEG_BIG))
            p = (q_b * k[s:s + 1, :]) * e
            a_b = jnp.where(lane == s, jnp.sum(p, axis=-1, keepdims=True), a_b)
        blocks.append(a_b)
    a = a + jnp.concatenate(blocks, axis=0)
    store(o + _dot(a.astype(BF16), v_bf))


def _hgrn_kernel(bounded_ref, qf_ref, zf_ref, vf_ref, qb_ref, zb_ref, vb_ref, lbf_ref, lbb_ref,
                 of_ref, ob_ref, sf_ref, sb_ref):
    @pl.when(pl.program_id(1) == 0)
    def _():
        sf_ref[...] = jnp.zeros_like(sf_ref)
        sb_ref[...] = jnp.zeros_like(sb_ref)

    n_chunks = qf_ref.shape[0] // HGRN_CHUNK

    def run(bounded_decay):
        per_step = 2 if bounded_decay and n_chunks % 2 == 0 else 1

        def body(step, carry):
            chains = []
            for u in range(per_step):
                ci = step * per_step + u
                rf = pl.ds(pl.multiple_of(ci * HGRN_CHUNK, HGRN_CHUNK), HGRN_CHUNK)
                rb = pl.ds(pl.multiple_of((n_chunks - 1 - ci) * HGRN_CHUNK, HGRN_CHUNK), HGRN_CHUNK)
                for h in range(HGRN_HEADS):
                    cols = slice(h * HGRN_DIM, (h + 1) * HGRN_DIM)
                    for rows, q_ref, z_ref, v_ref, lb_ref, o_ref, st_ref, reverse in (
                            (rf, qf_ref, zf_ref, vf_ref, lbf_ref, of_ref, sf_ref, False),
                            (rb, qb_ref, zb_ref, vb_ref, lbb_ref, ob_ref, sb_ref, True)):
                        def load(rows=rows, cols=cols, q_ref=q_ref, z_ref=z_ref, v_ref=v_ref):
                            return q_ref[rows, cols], z_ref[rows, cols], v_ref[rows, cols]

                        def store(o, rows=rows, cols=cols, o_ref=o_ref):
                            o_ref[rows, cols] = o.astype(o_ref.dtype)

                        chains.append(_hgrn_chunk(load, store, lb_ref[:, cols], st_ref.at[h],
                                                  reverse=reverse, bounded_decay=bounded_decay))
            while chains:
                chains = [ch for ch in chains if next(ch, True) is None]
            return carry

        lax.fori_loop(0, n_chunks // per_step, body, 0)

    pl.when(bounded_ref[0] == 1)(functools.partial(run, True))
    pl.when(bounded_ref[0] != 1)(functools.partial(run, False))


def hgrn(proj, lb_f, lb_b, *, batch, seq, col0, tile=512):
    nt = seq // tile
    cb = col0 // HGRN_WIDTH
    worst = -(HGRN_SUB - 1) * jnp.log(jnp.minimum(jnp.min(lb_f), jnp.min(lb_b)))
    bounded = (worst <= HGRN_MAX_LOG_DECAY).astype(jnp.int32).reshape(1)

    def fwd(colblock):
        return pl.BlockSpec((tile, HGRN_WIDTH), lambda b, i, flag: (b * nt + i, cb + colblock))

    def bwd(colblock):
        return pl.BlockSpec((tile, HGRN_WIDTH), lambda b, i, flag: (b * nt + nt - 1 - i, cb + colblock))

    lb_spec = pl.BlockSpec((1, HGRN_WIDTH), lambda b, i, flag: (0, 0))
    out_shape = jax.ShapeDtypeStruct((batch * seq, HGRN_WIDTH), BF16)
    state = pltpu.VMEM((HGRN_HEADS, HGRN_DIM, HGRN_DIM), F32)
    grid_spec = pltpu.PrefetchScalarGridSpec(
        num_scalar_prefetch=1,
        grid=(batch, nt),
        in_specs=[fwd(0), fwd(1), fwd(3), bwd(0), bwd(2), bwd(3), lb_spec, lb_spec],
        out_specs=[
            pl.BlockSpec((tile, HGRN_WIDTH), lambda b, i, flag: (b * nt + i, 0)),
            pl.BlockSpec((tile, HGRN_WIDTH), lambda b, i, flag: (b * nt + nt - 1 - i, 0)),
        ],
        scratch_shapes=[state, state],
    )
    return pl.pallas_call(
        _hgrn_kernel,
        grid_spec=grid_spec,
        out_shape=[out_shape, out_shape],
        compiler_params=pltpu.CompilerParams(
            dimension_semantics=("arbitrary", "arbitrary"),
            vmem_limit_bytes=V7X_VMEM_LIMIT_BYTES),
    )(bounded, proj, proj, proj, proj, proj, proj, lb_f.reshape(1, -1), lb_b.reshape(1, -1))


ATTN_HALF = 64
ATTN_QT = 128
LANES = 128
ATTN_SLABS = ATTN_WIDTH // LANES


def _head_rms(xs, w):
    lo = lax.broadcasted_iota(jnp.int32, (1, LANES), 1) < ATTN_HEAD_DIM
    sq = xs * xs
    s_lo = jnp.sum(jnp.where(lo, sq, 0.0), axis=-1, keepdims=True)
    s_hi = jnp.sum(jnp.where(lo, 0.0, sq), axis=-1, keepdims=True)
    ms = jnp.where(lo, s_lo, s_hi) * (1.0 / ATTN_HEAD_DIM)
    return xs * lax.rsqrt(ms + NORM_EPS) * w


def _attn_kernel(shifted_ref, *refs, seq):
    pl.when(shifted_ref[0] == 1)(lambda: _attn_tile(*refs, seq=seq, shifted=True))
    pl.when(shifted_ref[0] != 1)(lambda: _attn_tile(*refs, seq=seq, shifted=False))


def _attn_tile(*refs, seq, shifted):
    n_pat = len(DILATED_PATTERNS)
    ns = ATTN_SLABS
    q_refs, k_refs, kp_refs, kn_refs, v_refs, vp_refs, vn_refs = [refs[g * ns:(g + 1) * ns] for g in range(7)]
    bias_refs = refs[7 * ns:7 * ns + n_pat]
    o_ref, kwin, vwin, qn, s_scr, o_scr, l_scr = refs[7 * ns + n_pat:]
    tile = o_ref.shape[0]
    i = pl.program_id(1)
    lo = lax.broadcasted_iota(jnp.int32, (1, LANES), 1) < ATTN_HEAD_DIM

    for pi, ((_, dil), bias_ref) in enumerate(zip(DILATED_PATTERNS, bias_refs)):
        sub_rows = tile // dil
        qt = min(ATTN_QT, sub_rows)
        kt = qt + 2 * ATTN_HALF
        halo = ATTN_HALF * dil
        kcol = lax.broadcasted_iota(jnp.int32, (1, kt), 1)
        n_sub = sub_rows // qt

        def fill(r, ws, dil=dil, sub_rows=sub_rows, halo=halo):
            def rows_of(ref, start, n):
                return ref[pl.ds(start + r, n, stride=dil), :]

            main = slice(ATTN_HALF, ATTN_HALF + sub_rows)
            after = slice(ATTN_HALF + sub_rows, 2 * ATTN_HALF + sub_rows)
            for sl in range(ns):
                w = ws * ns + sl
                kwin[w, 0:ATTN_HALF, :] = rows_of(kp_refs[sl], tile - halo, ATTN_HALF).astype(BF16)
                kwin[w, main, :] = rows_of(k_refs[sl], 0, sub_rows).astype(BF16)
                kwin[w, after, :] = rows_of(kn_refs[sl], 0, ATTN_HALF).astype(BF16)
                vwin[w, 0:ATTN_HALF, :] = rows_of(vp_refs[sl], tile - halo, ATTN_HALF).astype(BF16)
                vwin[w, main, :] = rows_of(v_refs[sl], 0, sub_rows).astype(BF16)
                vwin[w, after, :] = rows_of(vn_refs[sl], 0, ATTN_HALF).astype(BF16)
                qn[w, 0:sub_rows, :] = rows_of(q_refs[sl], 0, sub_rows).astype(BF16)

        def sub(j, r, ws, ss, pi=pi, dil=dil, bias_ref=bias_ref, qt=qt, kt=kt, kcol=kcol):
            r0 = j * qt if isinstance(j, int) else pl.multiple_of(j * qt, qt)
            lk = (i * tile) // dil + j * qt - ATTN_HALF + kcol
            edge = jnp.where((lk >= 0) & (lk < seq // dil), 0.0, NEG_BIG)
            out_rows = pl.ds(r + j * (qt * dil), qt, stride=dil)
            for sl in range(ns):
                qs = qn[ws * ns + sl, pl.ds(r0, qt), :]
                ks = kwin[ws * ns + sl, pl.ds(r0, kt), :]
                for hh in range(2):
                    qh = jnp.where(lo if hh == 0 else jnp.logical_not(lo), qs, jnp.zeros_like(qs))
                    s_scr[ss * ATTN_HEADS + 2 * sl + hh, 0:qt, 0:kt] = (
                        _dot_nt(qh, ks) + bias_ref[2 * sl + hh] + edge)
            for sl in range(ns):
                vs = vwin[ws * ns + sl, pl.ds(r0, kt), :]
                o_slab = l_slab = None
                for hh in range(2):
                    s = s_scr[ss * ATTN_HEADS + 2 * sl + hh, 0:qt, 0:kt]
                    if shifted:
                        p = jnp.exp(s)
                    else:
                        m = jnp.max(s, axis=-1, keepdims=True)
                        p = jnp.exp(s - m)
                    den = jnp.sum(p, axis=-1, keepdims=True)
                    oh = _dot(p.astype(BF16), vs) * (1.0 / den)
                    lse = jnp.log(den) if shifted else m + jnp.log(den)
                    o_slab = oh if hh == 0 else jnp.where(lo, o_slab, oh)
                    l_slab = lse if hh == 0 else jnp.where(lo, l_slab, lse)
                o_scr[pi * ns + sl, out_rows, :] = o_slab
                l_scr[pi * ns + sl, out_rows, :] = l_slab

        if n_sub >= 2:
            assert n_sub % 2 == 0

            def subsequence(r, carry, fill=fill, sub=sub, n_sub=n_sub):
                fill(r, 0)
                if n_sub == 2:
                    sub(0, r, 0, 0)
                    sub(1, r, 0, 1)
                else:
                    def pair(jj, carry2):
                        sub(2 * jj, r, 0, 0)
                        sub(2 * jj + 1, r, 0, 1)
                        return carry2
                    lax.fori_loop(0, n_sub // 2, pair, 0)
                return carry

            lax.fori_loop(0, dil, subsequence, 0)
        else:
            assert dil % 2 == 0

            def subsequence_pair(rp, carry, fill=fill, sub=sub):
                fill(2 * rp, 0)
                fill(2 * rp + 1, 1)
                sub(0, 2 * rp, 0, 0)
                sub(0, 2 * rp + 1, 1, 1)
                return carry

            lax.fori_loop(0, dil // 2, subsequence_pair, 0)

    def merge(c, carry):
        rows = pl.ds(pl.multiple_of(c * ATTN_QT, ATTN_QT), ATTN_QT)
        for sl in range(ns):
            ls = [l_scr[p * ns + sl, rows, :] for p in range(n_pat)]
            mx = functools.reduce(jnp.maximum, ls)
            ws = [jnp.exp(l - mx) for l in ls]
            num = sum(w * o_scr[p * ns + sl, rows, :] for p, w in enumerate(ws))
            o_ref[rows, sl * LANES:(sl + 1) * LANES] = (num / sum(ws)).astype(o_ref.dtype)
        return carry

    lax.fori_loop(0, tile // ATTN_QT, merge, 0)


def _attn_bias(dilation, qt, shift):
    slopes = jnp.exp2(-ALIBI_MAX_BIAS * jnp.arange(1, ATTN_HEADS + 1, dtype=F32) / ATTN_HEADS)
    t = jnp.arange(qt)[:, None]
    j = jnp.arange(qt + 2 * ATTN_HALF)[None, :]
    dist = jnp.abs(j - ATTN_HALF - t)
    alibi = -slopes[:, None, None] * (dilation * dist).astype(F32)[None] - shift
    return jnp.where((dist <= ATTN_HALF)[None], alibi, NEG_BIG)


ATTN_MAX_SHIFT = 30.0


def attention(proj, qw, kw, *, batch, seq, tile=1024):
    nt = seq // tile
    n_pat = len(DILATED_PATTERNS)
    assert all(tile % (ATTN_HALF * dil) == 0 for _, dil in DILATED_PATTERNS)
    bound = (1.0 + 2.0 ** -6) * (ATTN_HEAD_DIM ** 0.5) * jnp.max(jnp.abs(qw)) * jnp.max(jnp.abs(kw))
    use_shift = bound <= ATTN_MAX_SHIFT
    shift = jnp.where(use_shift, bound, 0.0).astype(F32)

    def slabs(c, shift_tiles):
        def spec(sl):
            def index(b, i, flag):
                return (b * nt + jnp.clip(i + shift_tiles, 0, nt - 1), c * ATTN_SLABS + sl)
            return pl.BlockSpec((tile, LANES), index)
        return [spec(sl) for sl in range(ATTN_SLABS)]

    groups = [(0, 0), (1, 0), (1, -1), (1, 1), (2, 0), (2, -1), (2, 1)]
    biases = [_attn_bias(dil, min(ATTN_QT, tile // dil), shift) for _, dil in DILATED_PATTERNS]
    bias_specs = [pl.BlockSpec(bias.shape, lambda b, i, flag: (0, 0, 0)) for bias in biases]
    grid_spec = pltpu.PrefetchScalarGridSpec(
        num_scalar_prefetch=1,
        grid=(batch, nt),
        in_specs=[s for c, shift_tiles in groups for s in slabs(c, shift_tiles)] + bias_specs,
        out_specs=pl.BlockSpec((tile, ATTN_WIDTH), lambda b, i, flag: (b * nt + i, 0)),
        scratch_shapes=[pltpu.VMEM((2 * ATTN_SLABS, tile + 2 * ATTN_HALF, LANES), BF16),
                        pltpu.VMEM((2 * ATTN_SLABS, tile + 2 * ATTN_HALF, LANES), BF16),
                        pltpu.VMEM((2 * ATTN_SLABS, tile, LANES), BF16),
                        pltpu.VMEM((2 * ATTN_HEADS, ATTN_QT, ATTN_QT + 2 * ATTN_HALF), F32),
                        pltpu.VMEM((n_pat * ATTN_SLABS, tile, LANES), F32),
                        pltpu.VMEM((n_pat * ATTN_SLABS, tile, LANES), F32)],
    )
    return pl.pallas_call(
        functools.partial(_attn_kernel, seq=seq),
        grid_spec=grid_spec,
        out_shape=jax.ShapeDtypeStruct((batch * seq, ATTN_WIDTH), BF16),
        compiler_params=pltpu.CompilerParams(
            dimension_semantics=("arbitrary", "arbitrary"), vmem_limit_bytes=V7X_VMEM_LIMIT_BYTES),
    )(use_shift.astype(jnp.int32).reshape(1), *([proj] * (len(groups) * ATTN_SLABS)), *biases)


def _split2(x):
    hi = x.astype(BF16)
    return hi, (x - hi.astype(F32)).astype(BF16)


def _out_proj_kernel(x_ref, a_ref, of_ref, ob_ref, hg_ref, hw_ref, wo32_ref, n2_ref, wr_ref,
                     x1_ref, h2_ref, aff_ref, wo_ref):
    @pl.when(pl.program_id(0) == 0)
    def _():
        _cast_chunks(wo32_ref, wo_ref)

    a_out = a_ref[...]
    o = of_ref[...].astype(F32) + ob_ref[...].astype(F32)
    hg = hg_ref[...]
    hw = hw_ref[...]
    b_parts = []
    for sl in range(HGRN_HEADS):
        cols = slice(sl * HGRN_DIM, (sl + 1) * HGRN_DIM)
        os_ = o[:, cols]
        y = os_ * lax.rsqrt(jnp.mean(os_ * os_, axis=-1, keepdims=True) + NORM_EPS) * hw
        g = hg[:, cols]
        b_parts.append((y * (g * jax.nn.sigmoid(g))).astype(BF16))
    mixed = jnp.concatenate([a_out] + b_parts, axis=-1)

    x1 = x_ref[...] + _dot(mixed, wo_ref[...])
    x1_ref[...] = x1
    h2 = x1 * lax.rsqrt(jnp.mean(x1 * x1, axis=-1, keepdims=True) + NORM_EPS) * n2_ref[...]
    d = x1.shape[1]
    h_hi, h_lo = _split2(h2)
    h2_ref[:, 0:d] = h_hi

    both = _dot(h_hi, wr_ref[...])
    logits = both[:, 0:LANES] + both[:, LANES:2 * LANES] + _dot(h_lo, wr_ref[:, 0:LANES])
    valid = lax.broadcasted_iota(jnp.int32, (1, LANES), 1) < N_EXPERTS
    logits = jnp.where(valid, logits, NEG_BIG)
    ex = jnp.exp(logits - jnp.max(logits, axis=-1, keepdims=True))
    aff = ex / jnp.sum(ex, axis=-1, keepdims=True)
    aff_ref[...] = aff
    a_hi, a_lo = _split2(aff)
    h2_ref[:, d:d + LANES] = a_hi
    h2_ref[:, d + LANES:d + 2 * LANES] = a_lo


def out_proj(x2d, a_out, o_f, o_b, proj, hgrn_norm_w, w_out, norm2_w, w_router, *, hg_col, tm=256):
    m, d = x2d.shape
    wr = jnp.pad(w_router, ((0, 0), (0, LANES - N_EXPERTS)))
    wr_pair = jnp.concatenate(_split2(wr), axis=1)

    def rows(width, colblock=0):
        return pl.BlockSpec((tm, width), lambda i: (i, colblock))

    def const(shape):
        return pl.BlockSpec(shape, lambda i: (0, 0))

    return pl.pallas_call(
        _out_proj_kernel,
        grid=(m // tm,),
        in_specs=[rows(d), rows(ATTN_WIDTH),
                  rows(HGRN_WIDTH), rows(HGRN_WIDTH), rows(HGRN_WIDTH, hg_col // HGRN_WIDTH),
                  const((1, HGRN_DIM)), const((ATTN_WIDTH + HGRN_WIDTH, d)),
                  const((1, d)), const((d, 2 * LANES))],
        out_specs=[rows(d), rows(d + 2 * LANES), rows(LANES)],
        out_shape=[jax.ShapeDtypeStruct((m, d), F32), jax.ShapeDtypeStruct((m, d + 2 * LANES), BF16),
                   jax.ShapeDtypeStruct((m, LANES), F32)],
        scratch_shapes=[pltpu.VMEM((ATTN_WIDTH + HGRN_WIDTH, d), BF16)],
        compiler_params=pltpu.CompilerParams(
            dimension_semantics=("arbitrary",), vmem_limit_bytes=V7X_VMEM_LIMIT_BYTES),
    )(x2d, a_out, o_f, o_b, proj, hgrn_norm_w.reshape(1, -1), w_out,
      norm2_w.reshape(1, -1), wr_pair)


ROUTE_BLOCK = 256
COUNT_ROWS = 512


def _routing_kernel(aff_ref, slot_ref, start_ref, *, cap):
    seq = aff_ref.shape[0]

    def count(pred):
        def body(c, acc):
            blk = aff_ref[pl.ds(pl.multiple_of(c * COUNT_ROWS, COUNT_ROWS), COUNT_ROWS), :]
            hits = jnp.where(pred(blk), 1, 0).reshape(COUNT_ROWS // 8, 8, LANES)
            return acc + jnp.sum(hits, axis=0)
        acc = lax.fori_loop(0, seq // COUNT_ROWS, body, jnp.zeros((8, LANES), jnp.int32), unroll=4)
        return jnp.sum(acc, axis=0, keepdims=True)

    def bit_step(t, thr_bits):
        cand = thr_bits | jnp.left_shift(jnp.int32(1), 30 - t)
        cand_f = pltpu.bitcast(cand, F32)
        return jnp.where(count(lambda blk: blk >= cand_f) >= cap, cand, thr_bits)

    thr = pltpu.bitcast(lax.fori_loop(0, 31, bit_step, jnp.zeros((1, LANES), jnp.int32)), F32)
    need = (cap - count(lambda blk: blk > thr)).astype(F32)

    row = lax.broadcasted_iota(jnp.int32, (ROUTE_BLOCK, ROUTE_BLOCK), 0)
    col = lax.broadcasted_iota(jnp.int32, (ROUTE_BLOCK, ROUTE_BLOCK), 1)
    before = jnp.where(col < row, 1.0, 0.0).astype(BF16)

    group = 2

    def assign(jg, carry):
        c_eq, c_sel = carry
        starts = [pl.multiple_of((jg * group + u) * ROUTE_BLOCK, ROUTE_BLOCK) for u in range(group)]
        blks = [aff_ref[pl.ds(r0, ROUTE_BLOCK), :] for r0 in starts]
        gts = [blk > thr for blk in blks]
        eqs = [blk == thr for blk in blks]
        eq_fs = [jnp.where(eq, 1.0, 0.0) for eq in eqs]
        eq_pre = [_dot(before, eq_f.astype(BF16)) for eq_f in eq_fs]
        sels = []
        for u in range(group):
            sels.append(gts[u] | (eqs[u] & (eq_pre[u] + c_eq < need)))
            c_eq = c_eq + jnp.sum(eq_fs[u], axis=0, keepdims=True)
        sel_fs = [jnp.where(sel, 1.0, 0.0) for sel in sels]
        sel_pre = [_dot(before, sel_f.astype(BF16)) for sel_f in sel_fs]
        for u in range(group):
            slot_ref[pl.ds(starts[u], ROUTE_BLOCK), :] = jnp.where(sels[u], sel_pre[u] + c_sel, -1.0).astype(jnp.int32)
            start_ref[pl.ds(jg * group + u, 1), :] = c_sel.astype(jnp.int32)
            c_sel = c_sel + jnp.sum(sel_fs[u], axis=0, keepdims=True)
        return c_eq, c_sel

    zero = jnp.zeros((1, LANES), F32)
    lax.fori_loop(0, seq // (ROUTE_BLOCK * group), assign, (zero, zero))


def routing(aff, *, batch, seq, cap):
    nblk = seq // ROUTE_BLOCK
    return pl.pallas_call(
        functools.partial(_routing_kernel, cap=cap),
        grid=(batch,),
        in_specs=[pl.BlockSpec((seq, LANES), lambda b: (b, 0))],
        out_specs=[pl.BlockSpec((seq, LANES), lambda b: (b, 0)), pl.BlockSpec((nblk, LANES), lambda b: (b, 0))],
        out_shape=[jax.ShapeDtypeStruct((batch * seq, LANES), jnp.int32),
                   jax.ShapeDtypeStruct((batch * nblk, LANES), jnp.int32)],
        compiler_params=pltpu.CompilerParams(
            dimension_semantics=("arbitrary",), vmem_limit_bytes=V7X_VMEM_LIMIT_BYTES),
    )(aff)


SLOT_ALIGN = 16
SLOT_WIN = ROUTE_BLOCK + SLOT_ALIGN
SLOT_WIN_SMALL = 64
FFN_ROWS = 256


def _one_hots(slot_ref, tok, bases, win):
    r = lax.broadcasted_iota(jnp.int32, (win, ROUTE_BLOCK), 0)
    return jnp.concatenate(
        [jnp.where((slot_ref[0, e, :, tok] - base) == r, 1.0, 0.0).astype(BF16) for e, base in enumerate(bases)],
        axis=0)


BLOCKS_PER_STEP = 4


def _block_windows(base_ref, fits_ref, j, n_blocks):
    b = pl.program_id(0)
    n_exp = N_EXPERTS
    bases = [pl.multiple_of(base_ref[(b * n_exp + e) * n_blocks + j], SLOT_ALIGN) for e in range(n_exp)]
    return bases, fits_ref[b * n_blocks + j] == 1


def _for_block_groups(base_ref, fits_ref, first, n_groups, n_blocks, run):
    def group(g, carry):
        j0 = first + g * BLOCKS_PER_STEP
        js = [j0 + u for u in range(BLOCKS_PER_STEP)]
        windows = [_block_windows(base_ref, fits_ref, j, n_blocks) for j in js]
        blocks = [(j, bases) for j, (bases, _) in zip(js, windows)]
        fits = functools.reduce(jnp.logical_and, [f for _, f in windows])
        pl.when(fits)(functools.partial(run, SLOT_WIN_SMALL, blocks))

        @pl.when(jnp.logical_not(fits))
        def _():
            def one(u, carry2):
                bases, _ = _block_windows(base_ref, fits_ref, j0 + u, n_blocks)
                run(SLOT_WIN, [(j0 + u, bases)])
                return carry2
            lax.fori_loop(0, BLOCKS_PER_STEP, one, 0)

        return carry

    lax.fori_loop(0, n_groups, group, 0)


def _token_rows(j):
    return pl.ds(pl.multiple_of(j * ROUTE_BLOCK, ROUTE_BLOCK), ROUTE_BLOCK)


def _gather_kernel(base_ref, fits_ref, slot_ref, h_ref, xin_ref, *, n_blocks):
    xin_ref[...] = jnp.zeros_like(xin_ref)

    def run(win, blocks):
        rows = [_dot(_one_hots(slot_ref, _token_rows(j), bases, win), h_ref[_token_rows(j), :]).astype(BF16)
                for j, bases in blocks]
        for (j, bases), rows_j in zip(blocks, rows):
            for e, base in enumerate(bases):
                xin_ref[0, e, pl.ds(base, win), :] += rows_j[e * win:(e + 1) * win]

    _for_block_groups(base_ref, fits_ref, 0, n_blocks // BLOCKS_PER_STEP, n_blocks, run)


def gather(base, fits, slot_t, h_ext, *, batch, seq, cap, tn=256):
    n_exp = slot_t.shape[1]
    width = h_ext.shape[1]
    rows = cap + SLOT_WIN
    grid_spec = pltpu.PrefetchScalarGridSpec(
        num_scalar_prefetch=2,
        grid=(batch, width // tn),
        in_specs=[pl.BlockSpec((1, n_exp, 1, seq), lambda b, n, base, fits: (b, 0, 0, 0)),
                  pl.BlockSpec((seq, tn), lambda b, n, base, fits: (b, n))],
        out_specs=pl.BlockSpec((1, n_exp, rows, tn), lambda b, n, base, fits: (b, 0, 0, n)),
    )
    return pl.pallas_call(
        functools.partial(_gather_kernel, n_blocks=seq // ROUTE_BLOCK),
        grid_spec=grid_spec,
        out_shape=jax.ShapeDtypeStruct((batch, n_exp, rows, width), BF16),
        compiler_params=pltpu.CompilerParams(
            dimension_semantics=("arbitrary", "arbitrary"), vmem_limit_bytes=V7X_VMEM_LIMIT_BYTES),
    )(base, fits, slot_t, h_ext)


def _ffn_kernel(xin_ref, g_ref, wg32_ref, wu32_ref, wd32_ref, y_ref, wg_ref, wu_ref, wd_ref, *, cap):
    e = pl.program_id(0)

    @pl.when(pl.program_id(1) == 0)
    def _():
        _cast_chunks(wg32_ref.at[0], wg_ref)
        _cast_chunks(wu32_ref.at[0], wu_ref)
        _cast_chunks(wd32_ref.at[0], wd_ref)

    lane = lax.broadcasted_iota(jnp.int32, (1, 2 * LANES), 1)
    mine = (lane == e) | (lane == LANES + e)
    for rb in range(cap // FFN_ROWS):
        rows = slice(rb * FFN_ROWS, (rb + 1) * FFN_ROWS)
        xb = xin_ref[0, 0, rows, :]
        gate = jnp.sum(jnp.where(mine, g_ref[0, 0, rows, :].astype(F32), 0.0), axis=-1, keepdims=True)
        gate_h = _dot(xb, wg_ref[...])
        hid = (gate_h * jax.nn.sigmoid(gate_h)) * _dot(xb, wu_ref[...])
        y_ref[0, 0, rows, :] = (_dot(hid.astype(BF16), wd_ref[...]) * gate).astype(BF16)
    y_ref[0, 0, cap:, :] = jnp.zeros((y_ref.shape[2] - cap, y_ref.shape[3]), BF16)


def expert_ffn(xin, wg, wu, wd, *, cap):
    batch, n_exp, rows, width = xin.shape
    _, d, f = wg.shape
    return pl.pallas_call(
        functools.partial(_ffn_kernel, cap=cap),
        grid=(n_exp, batch),
        in_specs=[pl.BlockSpec((1, 1, cap, d), lambda e, b: (b, e, 0, 0)),
                  pl.BlockSpec((1, 1, cap, 2 * LANES), lambda e, b: (b, e, 0, d // (2 * LANES))),
                  pl.BlockSpec((1, d, f), lambda e, b: (e, 0, 0)),
                  pl.BlockSpec((1, d, f), lambda e, b: (e, 0, 0)),
                  pl.BlockSpec((1, f, d), lambda e, b: (e, 0, 0))],
        out_specs=pl.BlockSpec((1, 1, rows, d), lambda e, b: (b, e, 0, 0)),
        out_shape=jax.ShapeDtypeStruct((batch, n_exp, rows, d), BF16),
        scratch_shapes=[pltpu.VMEM((d, f), BF16), pltpu.VMEM((d, f), BF16), pltpu.VMEM((f, d), BF16)],
        compiler_params=pltpu.CompilerParams(
            dimension_semantics=("arbitrary", "arbitrary"), vmem_limit_bytes=V7X_VMEM_LIMIT_BYTES),
    )(xin, xin, wg, wu, wd)


def _combine_kernel(base_ref, fits_ref, slot_ref, y_ref, x1_ref, out_ref, ywin, *, n_blocks):
    blocks_here = out_ref.shape[0] // ROUTE_BLOCK
    first = pl.program_id(2) * blocks_here

    def run(win, blocks):
        stacked = N_EXPERTS * win
        separate = stacked * len(blocks) <= ywin.shape[0]
        for u, (j, bases) in enumerate(blocks):
            off = u * stacked if separate else 0
            rows = _token_rows(j - first)
            for e, base in enumerate(bases):
                ywin[off + e * win:off + (e + 1) * win, :] = y_ref[0, e, pl.ds(base, win), :]
            hits = _one_hots(slot_ref, _token_rows(j), bases, win)
            out_ref[rows, :] = x1_ref[rows, :] + _dot(hits, ywin[off:off + stacked, :], ((0,), (0,)))

    _for_block_groups(base_ref, fits_ref, first, blocks_here // BLOCKS_PER_STEP, n_blocks, run)


def combine(base, fits, slot_t, y, x1, *, batch, seq, tn=512, tt=1024):
    n_exp, rows, d = y.shape[1], y.shape[2], y.shape[3]
    grid_spec = pltpu.PrefetchScalarGridSpec(
        num_scalar_prefetch=2,
        grid=(batch, d // tn, seq // tt),
        in_specs=[pl.BlockSpec((1, n_exp, 1, seq), lambda b, n, t, base, fits: (b, 0, 0, 0)),
                  pl.BlockSpec((1, n_exp, rows, tn), lambda b, n, t, base, fits: (b, 0, 0, n)),
                  pl.BlockSpec((tt, tn), lambda b, n, t, base, fits: (b * (seq // tt) + t, n))],
        out_specs=pl.BlockSpec((tt, tn), lambda b, n, t, base, fits: (b * (seq // tt) + t, n)),
        scratch_shapes=[pltpu.VMEM((n_exp * SLOT_WIN, tn), BF16)],
    )
    return pl.pallas_call(
        functools.partial(_combine_kernel, n_blocks=seq // ROUTE_BLOCK),
        grid_spec=grid_spec,
        out_shape=jax.ShapeDtypeStruct((batch * seq, d), F32),
        compiler_params=pltpu.CompilerParams(
            dimension_semantics=("arbitrary", "arbitrary", "arbitrary"),
            vmem_limit_bytes=V7X_VMEM_LIMIT_BYTES),
    )(base, fits, slot_t, y, x1)


def kernel(x, norm1_w, w_in, attn_q_norm_w, attn_k_norm_w, hgrn_lb_fwd, hgrn_lb_bwd, hgrn_out_norm_w,
           w_out, norm2_w, w_router, w_expert_gate, w_expert_up, w_expert_down):
    batch, seq, d_model = x.shape
    depth = w_in.shape[0]
    cap = max(1, CAPACITY_FACTOR * seq // N_EXPERTS)
    hgrn_col = 3 * ATTN_WIDTH
    lb_f_all = jnp.cumsum(jax.nn.softmax(hgrn_lb_fwd.astype(F32), axis=0), axis=0)
    lb_b_all = jnp.cumsum(jax.nn.softmax(hgrn_lb_bwd.astype(F32), axis=0), axis=0)

    x2d = x.reshape(batch * seq, d_model)
    for l in range(depth):
        proj = in_proj(x2d, norm1_w[l], w_in[l], attn_q_norm_w[l], attn_k_norm_w[l])
        o_f, o_b = hgrn(proj, lb_f_all[l], lb_b_all[l], batch=batch, seq=seq, col0=hgrn_col)
        a_out = attention(proj, attn_q_norm_w[l], attn_k_norm_w[l], batch=batch, seq=seq)
        x1, h_ext, aff = out_proj(x2d, a_out, o_f, o_b, proj, hgrn_out_norm_w[l], w_out[l],
                                  norm2_w[l], w_router[l], hg_col=hgrn_col + 4 * HGRN_WIDTH)
        slot, start = routing(aff, batch=batch, seq=seq, cap=cap)

        def expert_major(t):
            return t.reshape(batch, -1, LANES)[:, :, :N_EXPERTS].transpose(0, 2, 1)

        slot_t = expert_major(slot).reshape(batch, N_EXPERTS, 1, seq)
        start_t = expert_major(start)
        base_t = start_t // SLOT_ALIGN * SLOT_ALIGN
        end_t = jnp.concatenate([start_t[:, :, 1:], jnp.full((batch, N_EXPERTS, 1), cap, jnp.int32)], axis=2)
        fits = (jnp.max(end_t - base_t, axis=1) <= SLOT_WIN_SMALL).astype(jnp.int32).reshape(-1)
        base = base_t.reshape(-1)
        xin = gather(base, fits, slot_t, h_ext, batch=batch, seq=seq, cap=cap)
        y = expert_ffn(xin, w_expert_gate[l], w_expert_up[l], w_expert_down[l], cap=cap)
        x2d = combine(base, fits, slot_t, y, x1, batch=batch, seq=seq)
    return x2d.reshape(batch, seq, d_model)
```

```python
import functools

import jax
import jax.numpy as jnp
from jax import lax
from jax.experimental import pallas as pl
from jax.experimental.pallas import tpu as pltpu

F32 = jnp.float32
BF16 = jnp.bfloat16

NORM_EPS = 1e-6
NEG_BIG = -1e30
ATTN_HEAD_DIM = 64
ATTN_HEADS = 8
ATTN_WIDTH = ATTN_HEADS * ATTN_HEAD_DIM
DILATED_PATTERNS = ((128, 1), (512, 4), (2048, 16))
ALIBI_MAX_BIAS = 8.0
HGRN_DIM = 128
HGRN_HEADS = 4
HGRN_WIDTH = HGRN_HEADS * HGRN_DIM
N_EXPERTS = 16
CAPACITY_FACTOR = 2
V7X_VMEM_LIMIT_BYTES = 56 * 1024 * 1024


def _dot(a, b, dims=((1,), (0,))):
    return lax.dot_general(a, b, (dims, ((), ())), preferred_element_type=F32)


def _dot_nt(a, b):
    return _dot(a, b, ((1,), (1,)))


def _cast_chunks(src_ref, dst_ref, rows=256):
    for r0 in range(0, src_ref.shape[0], rows):
        dst_ref[r0:r0 + rows, :] = src_ref[r0:r0 + rows, :].astype(dst_ref.dtype)


def _in_proj_kernel(x_ref, nw_ref, w32_ref, qw_ref, kw_ref, o_ref, w_ref, *, n_chunk):
    @pl.when(pl.program_id(0) == 0)
    def _():
        _cast_chunks(w32_ref, w_ref)

    x = x_ref[...]
    h = x * lax.rsqrt(jnp.mean(x * x, axis=-1, keepdims=True) + NORM_EPS) * nw_ref[...]
    h = h.astype(BF16)
    n_total = o_ref.shape[1]
    for c in range(n_total // n_chunk):
        cols = slice(c * n_chunk, (c + 1) * n_chunk)
        o_ref[:, cols] = _dot(h, w_ref[:, cols])
    for sl in range(ATTN_WIDTH // LANES):
        cols = slice(sl * LANES, (sl + 1) * LANES)
        o_ref[:, cols] = _head_rms(o_ref[:, cols], qw_ref[:, cols]) * (ATTN_HEAD_DIM ** -0.5)
        kcols = slice(ATTN_WIDTH + sl * LANES, ATTN_WIDTH + (sl + 1) * LANES)
        o_ref[:, kcols] = _head_rms(o_ref[:, kcols], kw_ref[:, cols])


def in_proj(x2d, norm_w, w, qw, kw, *, tm=256, n_chunk=512):
    m, d = x2d.shape
    n = w.shape[1]
    wspec = pl.BlockSpec((1, ATTN_WIDTH), lambda i: (0, 0))
    return pl.pallas_call(
        functools.partial(_in_proj_kernel, n_chunk=n_chunk),
        grid=(m // tm,),
        in_specs=[
            pl.BlockSpec((tm, d), lambda i: (i, 0)),
            pl.BlockSpec((1, d), lambda i: (0, 0)),
            pl.BlockSpec((d, n), lambda i: (0, 0), pipeline_mode=pl.Buffered(1)),
            wspec, wspec,
        ],
        out_specs=pl.BlockSpec((tm, n), lambda i: (i, 0)),
        out_shape=jax.ShapeDtypeStruct((m, n), F32),
        scratch_shapes=[pltpu.VMEM((d, n), BF16)],
        compiler_params=pltpu.CompilerParams(
            dimension_semantics=("arbitrary",), vmem_limit_bytes=V7X_VMEM_LIMIT_BYTES),
    )(x2d, norm_w.reshape(1, d), w,
      jnp.tile(qw, ATTN_HEADS).reshape(1, -1), jnp.tile(kw, ATTN_HEADS).reshape(1, -1))


HGRN_CHUNK = 64
HGRN_SUB = 16
HGRN_MAX_LOG_DECAY = 80.0


def _hgrn_chunk(load, store, lb, st_ref, *, reverse, bounded_decay):
    q_raw, z, v = load()
    c, dk = q_raw.shape
    sub = HGRN_SUB
    q = q_raw * jax.nn.sigmoid(q_raw)
    ez = jnp.exp(-jnp.abs(z))
    inv = 1.0 / (1.0 + ez)
    pos = z >= 0.0
    f = lb + (1.0 - lb) * jnp.where(pos, inv, ez * inv)
    k = (1.0 - lb) * jnp.where(pos, ez * inv, inv)
    g = jnp.log(f)

    row = lax.broadcasted_iota(jnp.int32, (c, c), 0)
    col = lax.broadcasted_iota(jnp.int32, (c, c), 1)
    tri = (col >= row) if reverse else (col <= row)
    tri_bf = jnp.where(tri, 1.0, 0.0).astype(BF16)
    g_hi, g_lo = _split2(g)
    yield
    cum = _dot(tri_bf, g_hi) + _dot(tri_bf, g_lo)
    yield

    def edge(r):
        return cum[r:r + 1, :]

    assert c == 4 * sub
    half = 2 * sub
    if reverse:
        ref_half, ref_q_lo, ref_q_hi, ref_end = edge(half), edge(sub), edge(half + sub), edge(0)
    else:
        ref_half, ref_q_lo, ref_q_hi, ref_end = edge(half - 1), edge(sub - 1), edge(half + sub - 1), edge(c - 1)

    r1 = lax.broadcasted_iota(jnp.int32, (c, 1), 0)
    ref_quarter = jnp.where(r1 < half, ref_q_lo, ref_q_hi)

    v_bf = v.astype(BF16)
    tb, sb = row // sub, col // sub
    if reverse:
        m1 = (tb < 2) & (sb >= 2)
        m2 = ((tb == 0) & (sb == 1)) | ((tb == 2) & (sb == 3))
    else:
        m1 = (tb >= 2) & (sb < 2)
        m2 = ((tb == 1) & (sb == 0)) | ((tb == 3) & (sb == 2))
    if bounded_decay:
        first = (sub - 1) if reverse else 0
        blocks = [slice(b * sub, (b + 1) * sub) for b in range(c // sub)]
        refs_d = [edge(b * sub + first) for b in range(c // sub)]
        q3 = jnp.concatenate([q[rows] * jnp.exp(cum[rows] - d) for rows, d in zip(blocks, refs_d)], axis=0)
        k3 = jnp.concatenate([k[rows] * jnp.exp(d - cum[rows]) for rows, d in zip(blocks, refs_d)], axis=0)

        def scaled(x, factors):
            return jnp.concatenate([x[rows] * f for rows, f in zip(blocks, factors)], axis=0).astype(BF16)

        refs_q = [ref_q_lo, ref_q_lo, ref_q_hi, ref_q_hi]
        qe = scaled(q3, [jnp.exp(d) for d in refs_d])
        ke = scaled(k3, [jnp.exp(ref_end - d) for d in refs_d])
        q1 = scaled(q3, [jnp.exp(jnp.minimum(d - ref_half, 0.0)) for d in refs_d])
        k1 = scaled(k3, [jnp.exp(jnp.minimum(ref_half - d, 0.0)) for d in refs_d])
        q2 = scaled(q3, [jnp.exp(jnp.minimum(d - rq, 0.0)) for d, rq in zip(refs_d, refs_q)])
        k2 = scaled(k3, [jnp.exp(jnp.minimum(rq - d, 0.0)) for d, rq in zip(refs_d, refs_q)])
        q3, k3 = q3.astype(BF16), k3.astype(BF16)
        m3 = (tb == sb) & ((col >= row) if reverse else (col <= row))
    else:
        qe = (q * jnp.exp(cum)).astype(BF16)
        ke = (k * jnp.exp(ref_end - cum)).astype(BF16)
        q1 = (q * jnp.exp(jnp.minimum(cum - ref_half, 0.0))).astype(BF16)
        k1 = (k * jnp.exp(jnp.minimum(ref_half - cum, 0.0))).astype(BF16)
        q2 = (q * jnp.exp(jnp.minimum(cum - ref_quarter, 0.0))).astype(BF16)
        k2 = (k * jnp.exp(jnp.minimum(ref_quarter - cum, 0.0))).astype(BF16)
    yield

    st = st_ref[...]
    o = _dot_nt(qe, st.astype(BF16))
    st_ref[...] = st * jnp.exp(ref_end) + _dot(v_bf, ke, ((0,), (0,)))
    a = jnp.where(m1, _dot_nt(q1, k1), 0.0) + jnp.where(m2, _dot_nt(q2, k2), 0.0)
    if bounded_decay:
        a = a + jnp.where(m3, _dot_nt(q3, k3), 0.0)
        a_bf = a.astype(BF16)
        yield
        store(o + _dot(a_bf, v_bf))
        return
    yield

    t_loc = lax.broadcasted_iota(jnp.int32, (sub, 1), 0)
    lane = lax.broadcasted_iota(jnp.int32, (sub, c), 1)
    blocks = []
    for blk in range(c // sub):
        rows = slice(blk * sub, (blk + 1) * sub)
        cum_b, q_b = cum[rows], q[rows]
        a_b = jnp.zeros((sub, c), F32)
        for s_loc in range(sub):
            s = blk * sub + s_loc
            keep = (t_loc <= s_loc) if reverse else (t_loc >= s_loc)
            e = jnp.exp(jnp.where(keep, cum_b - cum[s:s + 1, :], NEG_BIG))
            p = (q_b * k[s:s + 1, :]) * e
            a_b = jnp.where(lane == s, jnp.sum(p, axis=-1, keepdims=True), a_b)
        blocks.append(a_b)
    a = a + jnp.concatenate(blocks, axis=0)
    store(o + _dot(a.astype(BF16), v_bf))


def _hgrn_kernel(bounded_ref, qf_ref, zf_ref, vf_ref, qb_ref, zb_ref, vb_ref, lbf_ref, lbb_ref,
                 of_ref, ob_ref, sf_ref, sb_ref):
    @pl.when(pl.program_id(1) == 0)
    def _():
        sf_ref[...] = jnp.zeros_like(sf_ref)
        sb_ref[...] = jnp.zeros_like(sb_ref)

    n_chunks = qf_ref.shape[0] // HGRN_CHUNK

    def run(bounded_decay):
        per_step = 4 if bounded_decay and n_chunks % 4 == 0 else 1

        def body(step, carry):
            chains = []
            for u in range(per_step):
                ci = step * per_step + u
                rf = pl.ds(pl.multiple_of(ci * HGRN_CHUNK, HGRN_CHUNK), HGRN_CHUNK)
                rb = pl.ds(pl.multiple_of((n_chunks - 1 - ci) * HGRN_CHUNK, HGRN_CHUNK), HGRN_CHUNK)
                for h in range(HGRN_HEADS):
                    cols = slice(h * HGRN_DIM, (h + 1) * HGRN_DIM)
                    for rows, q_ref, z_ref, v_ref, lb_ref, o_ref, st_ref, reverse in (
                            (rf, qf_ref, zf_ref, vf_ref, lbf_ref, of_ref, sf_ref, False),
                            (rb, qb_ref, zb_ref, vb_ref, lbb_ref, ob_ref, sb_ref, True)):
                        def load(rows=rows, cols=cols, q_ref=q_ref, z_ref=z_ref, v_ref=v_ref):
                            return q_ref[rows, cols], z_ref[rows, cols], v_ref[rows, cols]

                        def store(o, rows=rows, cols=cols, o_ref=o_ref):
                            o_ref[rows, cols] = o.astype(o_ref.dtype)

                        chains.append(_hgrn_chunk(load, store, lb_ref[:, cols], st_ref.at[h],
                                                  reverse=reverse, bounded_decay=bounded_decay))
            while chains:
                chains = [ch for ch in chains if next(ch, True) is None]
            return carry

        lax.fori_loop(0, n_chunks // per_step, body, 0)

    pl.when(bounded_ref[0] == 1)(functools.partial(run, True))
    pl.when(bounded_ref[0] != 1)(functools.partial(run, False))


def hgrn(proj, lb_f, lb_b, *, batch, seq, col0, tile=512):
    nt = seq // tile
    cb = col0 // HGRN_WIDTH
    worst = -(HGRN_SUB - 1) * jnp.log(jnp.minimum(jnp.min(lb_f), jnp.min(lb_b)))
    bounded = (worst <= HGRN_MAX_LOG_DECAY).astype(jnp.int32).reshape(1)

    def fwd(colblock):
        return pl.BlockSpec((tile, HGRN_WIDTH), lambda b, i, flag: (b * nt + i, cb + colblock))

    def bwd(colblock):
        return pl.BlockSpec((tile, HGRN_WIDTH), lambda b, i, flag: (b * nt + nt - 1 - i, cb + colblock))

    lb_spec = pl.BlockSpec((1, HGRN_WIDTH), lambda b, i, flag: (0, 0))
    out_shape = jax.ShapeDtypeStruct((batch * seq, HGRN_WIDTH), BF16)
    state = pltpu.VMEM((HGRN_HEADS, HGRN_DIM, HGRN_DIM), F32)
    grid_spec = pltpu.PrefetchScalarGridSpec(
        num_scalar_prefetch=1,
        grid=(batch, nt),
        in_specs=[fwd(0), fwd(1), fwd(3), bwd(0), bwd(2), bwd(3), lb_spec, lb_spec],
        out_specs=[
            pl.BlockSpec((tile, HGRN_WIDTH), lambda b, i, flag: (b * nt + i, 0)),
            pl.BlockSpec((tile, HGRN_WIDTH), lambda b, i, flag: (b * nt + nt - 1 - i, 0)),
        ],
        scratch_shapes=[state, state],
    )
    return pl.pallas_call(
        _hgrn_kernel,
        grid_spec=grid_spec,
        out_shape=[out_shape, out_shape],
        compiler_params=pltpu.CompilerParams(
            dimension_semantics=("arbitrary", "arbitrary"),
            vmem_limit_bytes=V7X_VMEM_LIMIT_BYTES),
    )(bounded, proj, proj, proj, proj, proj, proj, lb_f.reshape(1, -1), lb_b.reshape(1, -1))


ATTN_HALF = 64
ATTN_QT = 128
LANES = 128
ATTN_SLABS = ATTN_WIDTH // LANES


def _head_rms(xs, w):
    lo = lax.broadcasted_iota(jnp.int32, (1, LANES), 1) < ATTN_HEAD_DIM
    sq = xs * xs
    s_lo = jnp.sum(jnp.where(lo, sq, 0.0), axis=-1, keepdims=True)
    s_hi = jnp.sum(jnp.where(lo, 0.0, sq), axis=-1, keepdims=True)
    ms = jnp.where(lo, s_lo, s_hi) * (1.0 / ATTN_HEAD_DIM)
    return xs * lax.rsqrt(ms + NORM_EPS) * w


def _attn_kernel(shifted_ref, *refs, seq):
    pl.when(shifted_ref[0] == 1)(lambda: _attn_tile(*refs, seq=seq, shifted=True))
    pl.when(shifted_ref[0] != 1)(lambda: _attn_tile(*refs, seq=seq, shifted=False))


def _attn_tile(*refs, seq, shifted):
    n_pat = len(DILATED_PATTERNS)
    ns = ATTN_SLABS
    q_refs, k_refs, kp_refs, kn_refs, v_refs, vp_refs, vn_refs = [refs[g * ns:(g + 1) * ns] for g in range(7)]
    bias_refs = refs[7 * ns:7 * ns + n_pat]
    o_ref, kwin, vwin, qn, s_scr, o_scr, l_scr = refs[7 * ns + n_pat:]
    tile = o_ref.shape[0]
    i = pl.program_id(1)
    lo = lax.broadcasted_iota(jnp.int32, (1, LANES), 1) < ATTN_HEAD_DIM

    for pi, ((_, dil), bias_ref) in enumerate(zip(DILATED_PATTERNS, bias_refs)):
        sub_rows = tile // dil
        qt = min(ATTN_QT, sub_rows)
        kt = qt + 2 * ATTN_HALF
        halo = ATTN_HALF * dil
        kcol = lax.broadcasted_iota(jnp.int32, (1, kt), 1)
        n_sub = sub_rows // qt

        def fill(r, ws, dil=dil, sub_rows=sub_rows, halo=halo):
            def rows_of(ref, start, n):
                return ref[pl.ds(start + r, n, stride=dil), :]

            main = slice(ATTN_HALF, ATTN_HALF + sub_rows)
            after = slice(ATTN_HALF + sub_rows, 2 * ATTN_HALF + sub_rows)
            for sl in range(ns):
                w = ws * ns + sl
                kwin[w, 0:ATTN_HALF, :] = rows_of(kp_refs[sl], tile - halo, ATTN_HALF).astype(BF16)
                kwin[w, main, :] = rows_of(k_refs[sl], 0, sub_rows).astype(BF16)
                kwin[w, after, :] = rows_of(kn_refs[sl], 0, ATTN_HALF).astype(BF16)
                vwin[w, 0:ATTN_HALF, :] = rows_of(vp_refs[sl], tile - halo, ATTN_HALF).astype(BF16)
                vwin[w, main, :] = rows_of(v_refs[sl], 0, sub_rows).astype(BF16)
                vwin[w, after, :] = rows_of(vn_refs[sl], 0, ATTN_HALF).astype(BF16)
                qn[w, 0:sub_rows, :] = rows_of(q_refs[sl], 0, sub_rows).astype(BF16)

        def sub(j, r, ws, ss, pi=pi, dil=dil, bias_ref=bias_ref, qt=qt, kt=kt, kcol=kcol):
            r0 = j * qt if isinstance(j, int) else pl.multiple_of(j * qt, qt)
            lk = (i * tile) // dil + j * qt - ATTN_HALF + kcol
            edge = jnp.where((lk >= 0) & (lk < seq // dil), 0.0, NEG_BIG)
            out_rows = pl.ds(r + j * (qt * dil), qt, stride=dil)
            for sl in range(ns):
                qs = qn[ws * ns + sl, pl.ds(r0, qt), :]
                ks = kwin[ws * ns + sl, pl.ds(r0, kt), :]
                for hh in range(2):
                    qh = jnp.where(lo if hh == 0 else jnp.logical_not(lo), qs, jnp.zeros_like(qs))
                    s_scr[ss * ATTN_HEADS + 2 * sl + hh, 0:qt, 0:kt] = (
                        _dot_nt(qh, ks) + bias_ref[2 * sl + hh] + edge)
            for sl in range(ns):
                vs = vwin[ws * ns + sl, pl.ds(r0, kt), :]
                o_slab = l_slab = None
                for hh in range(2):
                    s = s_scr[ss * ATTN_HEADS + 2 * sl + hh, 0:qt, 0:kt]
                    if shifted:
                        p = jnp.exp(s)
                    else:
                        m = jnp.max(s, axis=-1, keepdims=True)
                        p = jnp.exp(s - m)
                    den = jnp.sum(p, axis=-1, keepdims=True)
                    oh = _dot(p.astype(BF16), vs) * (1.0 / den)
                    lse = jnp.log(den) if shifted else m + jnp.log(den)
                    o_slab = oh if hh == 0 else jnp.where(lo, o_slab, oh)
                    l_slab = lse if hh == 0 else jnp.where(lo, l_slab, lse)
                o_scr[pi * ns + sl, out_rows, :] = o_slab
                l_scr[pi * ns + sl, out_rows, :] = l_slab

        if n_sub >= 2:
            assert n_sub % 2 == 0

            def subsequence(r, carry, fill=fill, sub=sub, n_sub=n_sub):
                fill(r, 0)
                if n_sub == 2:
                    sub(0, r, 0, 0)
                    sub(1, r, 0, 1)
                else:
                    def pair(jj, carry2):
                        sub(2 * jj, r, 0, 0)
                        sub(2 * jj + 1, r, 0, 1)
                        return carry2
                    lax.fori_loop(0, n_sub // 2, pair, 0)
                return carry

            lax.fori_loop(0, dil, subsequence, 0)
        else:
            assert dil % 2 == 0

            def subsequence_pair(rp, carry, fill=fill, sub=sub):
                fill(2 * rp, 0)
                fill(2 * rp + 1, 1)
                sub(0, 2 * rp, 0, 0)
                sub(0, 2 * rp + 1, 1, 1)
                return carry

            lax.fori_loop(0, dil // 2, subsequence_pair, 0)

    def merge(c, carry):
        rows = pl.ds(pl.multiple_of(c * ATTN_QT, ATTN_QT), ATTN_QT)
        for sl in range(ns):
            ls = [l_scr[p * ns + sl, rows, :] for p in range(n_pat)]
            mx = functools.reduce(jnp.maximum, ls)
            ws = [jnp.exp(l - mx) for l in ls]
            num = sum(w * o_scr[p * ns + sl, rows, :] for p, w in enumerate(ws))
            o_ref[rows, sl * LANES:(sl + 1) * LANES] = (num / sum(ws)).astype(o_ref.dtype)
        return carry

    lax.fori_loop(0, tile // ATTN_QT, merge, 0)


def _attn_bias(dilation, qt, shift):
    slopes = jnp.exp2(-ALIBI_MAX_BIAS * jnp.arange(1, ATTN_HEADS + 1, dtype=F32) / ATTN_HEADS)
    t = jnp.arange(qt)[:, None]
    j = jnp.arange(qt + 2 * ATTN_HALF)[None, :]
    dist = jnp.abs(j - ATTN_HALF - t)
    alibi = -slopes[:, None, None] * (dilation * dist).astype(F32)[None] - shift
    return jnp.where((dist <= ATTN_HALF)[None], alibi, NEG_BIG)


ATTN_MAX_SHIFT = 30.0


def attention(proj, qw, kw, *, batch, seq, tile=1024):
    nt = seq // tile
    n_pat = len(DILATED_PATTERNS)
    assert all(tile % (ATTN_HALF * dil) == 0 for _, dil in DILATED_PATTERNS)
    bound = (1.0 + 2.0 ** -6) * (ATTN_HEAD_DIM ** 0.5) * jnp.max(jnp.abs(qw)) * jnp.max(jnp.abs(kw))
    use_shift = bound <= ATTN_MAX_SHIFT
    shift = jnp.where(use_shift, bound, 0.0).astype(F32)

    def slabs(c, shift_tiles):
        def spec(sl):
            def index(b, i, flag):
                return (b * nt + jnp.clip(i + shift_tiles, 0, nt - 1), c * ATTN_SLABS + sl)
            return pl.BlockSpec((tile, LANES), index)
        return [spec(sl) for sl in range(ATTN_SLABS)]

    groups = [(0, 0), (1, 0), (1, -1), (1, 1), (2, 0), (2, -1), (2, 1)]
    biases = [_attn_bias(dil, min(ATTN_QT, tile // dil), shift) for _, dil in DILATED_PATTERNS]
    bias_specs = [pl.BlockSpec(bias.shape, lambda b, i, flag: (0, 0, 0)) for bias in biases]
    grid_spec = pltpu.PrefetchScalarGridSpec(
        num_scalar_prefetch=1,
        grid=(batch, nt),
        in_specs=[s for c, shift_tiles in groups for s in slabs(c, shift_tiles)] + bias_specs,
        out_specs=pl.BlockSpec((tile, ATTN_WIDTH), lambda b, i, flag: (b * nt + i, 0)),
        scratch_shapes=[pltpu.VMEM((2 * ATTN_SLABS, tile + 2 * ATTN_HALF, LANES), BF16),
                        pltpu.VMEM((2 * ATTN_SLABS, tile + 2 * ATTN_HALF, LANES), BF16),
                        pltpu.VMEM((2 * ATTN_SLABS, tile, LANES), BF16),
                        pltpu.VMEM((2 * ATTN_HEADS, ATTN_QT, ATTN_QT + 2 * ATTN_HALF), F32),
                        pltpu.VMEM((n_pat * ATTN_SLABS, tile, LANES), F32),
                        pltpu.VMEM((n_pat * ATTN_SLABS, tile, LANES), F32)],
    )
    return pl.pallas_call(
        functools.partial(_attn_kernel, seq=seq),
        grid_spec=grid_spec,
        out_shape=jax.ShapeDtypeStruct((batch * seq, ATTN_WIDTH), BF16),
        compiler_params=pltpu.CompilerParams(
            dimension_semantics=("arbitrary", "arbitrary"), vmem_limit_bytes=V7X_VMEM_LIMIT_BYTES),
    )(use_shift.astype(jnp.int32).reshape(1), *([proj] * (len(groups) * ATTN_SLABS)), *biases)


def _split2(x):
    hi = x.astype(BF16)
    return hi, (x - hi.astype(F32)).astype(BF16)


def _out_proj_kernel(x_ref, a_ref, of_ref, ob_ref, hg_ref, hw_ref, wo32_ref, n2_ref, wr_ref,
                     x1_ref, h2_ref, aff_ref, wo_ref):
    @pl.when(pl.program_id(0) == 0)
    def _():
        _cast_chunks(wo32_ref, wo_ref)

    a_out = a_ref[...]
    o = of_ref[...].astype(F32) + ob_ref[...].astype(F32)
    hg = hg_ref[...]
    hw = hw_ref[...]
    b_parts = []
    for sl in range(HGRN_HEADS):
        cols = slice(sl * HGRN_DIM, (sl + 1) * HGRN_DIM)
        os_ = o[:, cols]
        y = os_ * lax.rsqrt(jnp.mean(os_ * os_, axis=-1, keepdims=True) + NORM_EPS) * hw
        g = hg[:, cols]
        b_parts.append((y * (g * jax.nn.sigmoid(g))).astype(BF16))
    mixed = jnp.concatenate([a_out] + b_parts, axis=-1)

    x1 = x_ref[...] + _dot(mixed, wo_ref[...])
    x1_ref[...] = x1
    h2 = x1 * lax.rsqrt(jnp.mean(x1 * x1, axis=-1, keepdims=True) + NORM_EPS) * n2_ref[...]
    d = x1.shape[1]
    h_hi, h_lo = _split2(h2)
    h2_ref[:, 0:d] = h_hi

    both = _dot(h_hi, wr_ref[...])
    logits = both[:, 0:LANES] + both[:, LANES:2 * LANES] + _dot(h_lo, wr_ref[:, 0:LANES])
    valid = lax.broadcasted_iota(jnp.int32, (1, LANES), 1) < N_EXPERTS
    logits = jnp.where(valid, logits, NEG_BIG)
    ex = jnp.exp(logits - jnp.max(logits, axis=-1, keepdims=True))
    aff = ex / jnp.sum(ex, axis=-1, keepdims=True)
    aff_ref[...] = aff
    a_hi, a_lo = _split2(aff)
    h2_ref[:, d:d + LANES] = a_hi
    h2_ref[:, d + LANES:d + 2 * LANES] = a_lo


def out_proj(x2d, a_out, o_f, o_b, proj, hgrn_norm_w, w_out, norm2_w, w_router, *, hg_col, tm=256):
    m, d = x2d.shape
    wr = jnp.pad(w_router, ((0, 0), (0, LANES - N_EXPERTS)))
    wr_pair = jnp.concatenate(_split2(wr), axis=1)

    def rows(width, colblock=0):
        return pl.BlockSpec((tm, width), lambda i: (i, colblock))

    def const(shape):
        return pl.BlockSpec(shape, lambda i: (0, 0))

    return pl.pallas_call(
        _out_proj_kernel,
        grid=(m // tm,),
        in_specs=[rows(d), rows(ATTN_WIDTH),
                  rows(HGRN_WIDTH), rows(HGRN_WIDTH), rows(HGRN_WIDTH, hg_col // HGRN_WIDTH),
                  const((1, HGRN_DIM)), const((ATTN_WIDTH + HGRN_WIDTH, d)),
                  const((1, d)), const((d, 2 * LANES))],
        out_specs=[rows(d), rows(d + 2 * LANES), rows(LANES)],
        out_shape=[jax.ShapeDtypeStruct((m, d), F32), jax.ShapeDtypeStruct((m, d + 2 * LANES), BF16),
                   jax.ShapeDtypeStruct((m, LANES), F32)],
        scratch_shapes=[pltpu.VMEM((ATTN_WIDTH + HGRN_WIDTH, d), BF16)],
        compiler_params=pltpu.CompilerParams(
            dimension_semantics=("arbitrary",), vmem_limit_bytes=V7X_VMEM_LIMIT_BYTES),
    )(x2d, a_out, o_f, o_b, proj, hgrn_norm_w.reshape(1, -1), w_out,
      norm2_w.reshape(1, -1), wr_pair)


ROUTE_BLOCK = 256
COUNT_ROWS = 512


def _routing_kernel(aff_ref, slot_ref, start_ref, *, cap):
    seq = aff_ref.shape[0]

    def count(pred):
        def body(c, acc):
            blk = aff_ref[pl.ds(pl.multiple_of(c * COUNT_ROWS, COUNT_ROWS), COUNT_ROWS), :]
            hits = jnp.where(pred(blk), 1, 0).reshape(COUNT_ROWS // 8, 8, LANES)
            return acc + jnp.sum(hits, axis=0)
        acc = lax.fori_loop(0, seq // COUNT_ROWS, body, jnp.zeros((8, LANES), jnp.int32), unroll=4)
        return jnp.sum(acc, axis=0, keepdims=True)

    def bit_step(t, thr_bits):
        cand = thr_bits | jnp.left_shift(jnp.int32(1), 30 - t)
        cand_f = pltpu.bitcast(cand, F32)
        return jnp.where(count(lambda blk: blk >= cand_f) >= cap, cand, thr_bits)

    thr = pltpu.bitcast(lax.fori_loop(0, 31, bit_step, jnp.zeros((1, LANES), jnp.int32)), F32)
    need = (cap - count(lambda blk: blk > thr)).astype(F32)

    row = lax.broadcasted_iota(jnp.int32, (ROUTE_BLOCK, ROUTE_BLOCK), 0)
    col = lax.broadcasted_iota(jnp.int32, (ROUTE_BLOCK, ROUTE_BLOCK), 1)
    before = jnp.where(col < row, 1.0, 0.0).astype(BF16)

    group = 2

    def assign(jg, carry):
        c_eq, c_sel = carry
        starts = [pl.multiple_of((jg * group + u) * ROUTE_BLOCK, ROUTE_BLOCK) for u in range(group)]
        blks = [aff_ref[pl.ds(r0, ROUTE_BLOCK), :] for r0 in starts]
        gts = [blk > thr for blk in blks]
        eqs = [blk == thr for blk in blks]
        eq_fs = [jnp.where(eq, 1.0, 0.0) for eq in eqs]
        eq_pre = [_dot(before, eq_f.astype(BF16)) for eq_f in eq_fs]
        sels = []
        for u in range(group):
            sels.append(gts[u] | (eqs[u] & (eq_pre[u] + c_eq < need)))
            c_eq = c_eq + jnp.sum(eq_fs[u], axis=0, keepdims=True)
        sel_fs = [jnp.where(sel, 1.0, 0.0) for sel in sels]
        sel_pre = [_dot(before, sel_f.astype(BF16)) for sel_f in sel_fs]
        for u in range(group):
            slot_ref[pl.ds(starts[u], ROUTE_BLOCK), :] = jnp.where(sels[u], sel_pre[u] + c_sel, -1.0).astype(jnp.int32)
            start_ref[pl.ds(jg * group + u, 1), :] = c_sel.astype(jnp.int32)
            c_sel = c_sel + jnp.sum(sel_fs[u], axis=0, keepdims=True)
        return c_eq, c_sel

    zero = jnp.zeros((1, LANES), F32)
    lax.fori_loop(0, seq // (ROUTE_BLOCK * group), assign, (zero, zero))


def routing(aff, *, batch, seq, cap):
    nblk = seq // ROUTE_BLOCK
    return pl.pallas_call(
        functools.partial(_routing_kernel, cap=cap),
        grid=(batch,),
        in_specs=[pl.BlockSpec((seq, LANES), lambda b: (b, 0))],
        out_specs=[pl.BlockSpec((seq, LANES), lambda b: (b, 0)), pl.BlockSpec((nblk, LANES), lambda b: (b, 0))],
        out_shape=[jax.ShapeDtypeStruct((batch * seq, LANES), jnp.int32),
                   jax.ShapeDtypeStruct((batch * nblk, LANES), jnp.int32)],
        compiler_params=pltpu.CompilerParams(
            dimension_semantics=("arbitrary",), vmem_limit_bytes=V7X_VMEM_LIMIT_BYTES),
    )(aff)


SLOT_ALIGN = 16
SLOT_WIN = ROUTE_BLOCK + SLOT_ALIGN
SLOT_WIN_SMALL = 64
FFN_ROWS = 256


def _one_hots(slot_ref, tok, bases, win):
    r = lax.broadcasted_iota(jnp.int32, (win, ROUTE_BLOCK), 0)
    return jnp.concatenate(
        [jnp.where((slot_ref[0, e, :, tok] - base) == r, 1.0, 0.0).astype(BF16) for e, base in enumerate(bases)],
        axis=0)


BLOCKS_PER_STEP = 4


def _block_windows(base_ref, fits_ref, j, n_blocks):
    b = pl.program_id(0)
    n_exp = N_EXPERTS
    bases = [pl.multiple_of(base_ref[(b * n_exp + e) * n_blocks + j], SLOT_ALIGN) for e in range(n_exp)]
    return bases, fits_ref[b * n_blocks + j] == 1


def _for_block_groups(base_ref, fits_ref, first, n_groups, n_blocks, run):
    def group(g, carry):
        j0 = first + g * BLOCKS_PER_STEP
        js = [j0 + u for u in range(BLOCKS_PER_STEP)]
        windows = [_block_windows(base_ref, fits_ref, j, n_blocks) for j in js]
        blocks = [(j, bases) for j, (bases, _) in zip(js, windows)]
        fits = functools.reduce(jnp.logical_and, [f for _, f in windows])
        pl.when(fits)(functools.partial(run, SLOT_WIN_SMALL, blocks))

        @pl.when(jnp.logical_not(fits))
        def _():
            def one(u, carry2):
                bases, _ = _block_windows(base_ref, fits_ref, j0 + u, n_blocks)
                run(SLOT_WIN, [(j0 + u, bases)])
                return carry2
            lax.fori_loop(0, BLOCKS_PER_STEP, one, 0)

        return carry

    lax.fori_loop(0, n_groups, group, 0)


def _token_rows(j):
    return pl.ds(pl.multiple_of(j * ROUTE_BLOCK, ROUTE_BLOCK), ROUTE_BLOCK)


def _gather_kernel(base_ref, fits_ref, slot_ref, h_ref, xin_ref, *, n_blocks):
    xin_ref[...] = jnp.zeros_like(xin_ref)

    def run(win, blocks):
        rows = [_dot(_one_hots(slot_ref, _token_rows(j), bases, win), h_ref[_token_rows(j), :]).astype(BF16)
                for j, bases in blocks]
        for (j, bases), rows_j in zip(blocks, rows):
            for e, base in enumerate(bases):
                xin_ref[0, e, pl.ds(base, win), :] += rows_j[e * win:(e + 1) * win]

    _for_block_groups(base_ref, fits_ref, 0, n_blocks // BLOCKS_PER_STEP, n_blocks, run)


def gather(base, fits, slot_t, h_ext, *, batch, seq, cap, tn=256):
    n_exp = slot_t.shape[1]
    width = h_ext.shape[1]
    rows = cap + SLOT_WIN
    grid_spec = pltpu.PrefetchScalarGridSpec(
        num_scalar_prefetch=2,
        grid=(batch, width // tn),
        in_specs=[pl.BlockSpec((1, n_exp, 1, seq), lambda b, n, base, fits: (b, 0, 0, 0)),
                  pl.BlockSpec((seq, tn), lambda b, n, base, fits: (b, n))],
        out_specs=pl.BlockSpec((1, n_exp, rows, tn), lambda b, n, base, fits: (b, 0, 0, n)),
    )
    return pl.pallas_call(
        functools.partial(_gather_kernel, n_blocks=seq // ROUTE_BLOCK),
        grid_spec=grid_spec,
        out_shape=jax.ShapeDtypeStruct((batch, n_exp, rows, width), BF16),
        compiler_params=pltpu.CompilerParams(
            dimension_semantics=("arbitrary", "arbitrary"), vmem_limit_bytes=V7X_VMEM_LIMIT_BYTES),
    )(base, fits, slot_t, h_ext)


def _ffn_kernel(xin_ref, g_ref, wg32_ref, wu32_ref, wd32_ref, y_ref, wg_ref, wu_ref, wd_ref, *, cap):
    e = pl.program_id(0)

    @pl.when(pl.program_id(1) == 0)
    def _():
        _cast_chunks(wg32_ref.at[0], wg_ref)
        _cast_chunks(wu32_ref.at[0], wu_ref)
        _cast_chunks(wd32_ref.at[0], wd_ref)

    lane = lax.broadcasted_iota(jnp.int32, (1, 2 * LANES), 1)
    mine = (lane == e) | (lane == LANES + e)
    for rb in range(cap // FFN_ROWS):
        rows = slice(rb * FFN_ROWS, (rb + 1) * FFN_ROWS)
        xb = xin_ref[0, 0, rows, :]
        gate = jnp.sum(jnp.where(mine, g_ref[0, 0, rows, :].astype(F32), 0.0), axis=-1, keepdims=True)
        gate_h = _dot(xb, wg_ref[...])
        hid = (gate_h * jax.nn.sigmoid(gate_h)) * _dot(xb, wu_ref[...])
        y_ref[0, 0, rows, :] = (_dot(hid.astype(BF16), wd_ref[...]) * gate).astype(BF16)
    y_ref[0, 0, cap:, :] = jnp.zeros((y_ref.shape[2] - cap, y_ref.shape[3]), BF16)


def expert_ffn(xin, wg, wu, wd, *, cap):
    batch, n_exp, rows, width = xin.shape
    _, d, f = wg.shape
    return pl.pallas_call(
        functools.partial(_ffn_kernel, cap=cap),
        grid=(n_exp, batch),
        in_specs=[pl.BlockSpec((1, 1, cap, d), lambda e, b: (b, e, 0, 0)),
                  pl.BlockSpec((1, 1, cap, 2 * LANES), lambda e, b: (b, e, 0, d // (2 * LANES))),
                  pl.BlockSpec((1, d, f), lambda e, b: (e, 0, 0)),
                  pl.BlockSpec((1, d, f), lambda e, b: (e, 0, 0)),
                  pl.BlockSpec((1, f, d), lambda e, b: (e, 0, 0))],
        out_specs=pl.BlockSpec((1, 1, rows, d), lambda e, b: (b, e, 0, 0)),
        out_shape=jax.ShapeDtypeStruct((batch, n_exp, rows, d), BF16),
        scratch_shapes=[pltpu.VMEM((d, f), BF16), pltpu.VMEM((d, f), BF16), pltpu.VMEM((f, d), BF16)],
        compiler_params=pltpu.CompilerParams(
            dimension_semantics=("arbitrary", "arbitrary"), vmem_limit_bytes=V7X_VMEM_LIMIT_BYTES),
    )(xin, xin, wg, wu, wd)


def _combine_kernel(base_ref, fits_ref, slot_ref, y_ref, x1_ref, out_ref, ywin, *, n_blocks):
    blocks_here = out_ref.shape[0] // ROUTE_BLOCK
    first = pl.program_id(2) * blocks_here

    def run(win, blocks):
        stacked = N_EXPERTS * win
        separate = stacked * len(blocks) <= ywin.shape[0]
        for u, (j, bases) in enumerate(blocks):
            off = u * stacked if separate else 0
            rows = _token_rows(j - first)
            for e, base in enumerate(bases):
                ywin[off + e * win:off + (e + 1) * win, :] = y_ref[0, e, pl.ds(base, win), :]
            hits = _one_hots(slot_ref, _token_rows(j), bases, win)
            out_ref[rows, :] = x1_ref[rows, :] + _dot(hits, ywin[off:off + stacked, :], ((0,), (0,)))

    _for_block_groups(base_ref, fits_ref, first, blocks_here // BLOCKS_PER_STEP, n_blocks, run)


def combine(base, fits, slot_t, y, x1, *, batch, seq, tn=512, tt=1024):
    n_exp, rows, d = y.shape[1], y.shape[2], y.shape[3]
    grid_spec = pltpu.PrefetchScalarGridSpec(
        num_scalar_prefetch=2,
        grid=(batch, d // tn, seq // tt),
        in_specs=[pl.BlockSpec((1, n_exp, 1, seq), lambda b, n, t, base, fits: (b, 0, 0, 0)),
                  pl.BlockSpec((1, n_exp, rows, tn), lambda b, n, t, base, fits: (b, 0, 0, n)),
                  pl.BlockSpec((tt, tn), lambda b, n, t, base, fits: (b * (seq // tt) + t, n))],
        out_specs=pl.BlockSpec((tt, tn), lambda b, n, t, base, fits: (b * (seq // tt) + t, n)),
        scratch_shapes=[pltpu.VMEM((n_exp * SLOT_WIN, tn), BF16)],
    )
    return pl.pallas_call(
        functools.partial(_combine_kernel, n_blocks=seq // ROUTE_BLOCK),
        grid_spec=grid_spec,
        out_shape=jax.ShapeDtypeStruct((batch * seq, d), F32),
        compiler_params=pltpu.CompilerParams(
            dimension_semantics=("arbitrary", "arbitrary", "arbitrary"),
            vmem_limit_bytes=V7X_VMEM_LIMIT_BYTES),
    )(base, fits, slot_t, y, x1)


def kernel(x, norm1_w, w_in, attn_q_norm_w, attn_k_norm_w, hgrn_lb_fwd, hgrn_lb_bwd, hgrn_out_norm_w,
           w_out, norm2_w, w_router, w_expert_gate, w_expert_up, w_expert_down):
    batch, seq, d_model = x.shape
    depth = w_in.shape[0]
    cap = max(1, CAPACITY_FACTOR * seq // N_EXPERTS)
    hgrn_col = 3 * ATTN_WIDTH
    lb_f_all = jnp.cumsum(jax.nn.softmax(hgrn_lb_fwd.astype(F32), axis=0), axis=0)
    lb_b_all = jnp.cumsum(jax.nn.softmax(hgrn_lb_bwd.astype(F32), axis=0), axis=0)

    x2d = x.reshape(batch * seq, d_model)
    for l in range(depth):
        proj = in_proj(x2d, norm1_w[l], w_in[l], attn_q_norm_w[l], attn_k_norm_w[l])
        o_f, o_b = hgrn(proj, lb_f_all[l], lb_b_all[l], batch=batch, seq=seq, col0=hgrn_col)
        a_out = attention(proj, attn_q_norm_w[l], attn_k_norm_w[l], batch=batch, seq=seq)
        x1, h_ext, aff = out_proj(x2d, a_out, o_f, o_b, proj, hgrn_out_norm_w[l], w_out[l],
                                  norm2_w[l], w_router[l], hg_col=hgrn_col + 4 * HGRN_WIDTH)
        slot, start = routing(aff, batch=batch, seq=seq, cap=cap)

        def expert_major(t):
            return t.reshape(batch, -1, LANES)[:, :, :N_EXPERTS].transpose(0, 2, 1)

        slot_t = expert_major(slot).reshape(batch, N_EXPERTS, 1, seq)
        start_t = expert_major(start)
        base_t = start_t // SLOT_ALIGN * SLOT_ALIGN
        end_t = jnp.concatenate([start_t[:, :, 1:], jnp.full((batch, N_EXPERTS, 1), cap, jnp.int32)], axis=2)
        fits = (jnp.max(end_t - base_t, axis=1) <= SLOT_WIN_SMALL).astype(jnp.int32).reshape(-1)
        base = base_t.reshape(-1)
        xin = gather(base, fits, slot_t, h_ext, batch=batch, seq=seq, cap=cap)
        y = expert_ffn(xin, w_expert_gate[l], w_expert_up[l], w_expert_down[l], cap=cap)
        x2d = combine(base, fits, slot_t, y, x1, batch=batch, seq=seq)
    return x2d.reshape(batch, seq, d_model)
```

```python
import functools

import jax
import jax.numpy as jnp
from jax import lax
from jax.experimental import pallas as pl
from jax.experimental.pallas import tpu as pltpu

F32 = jnp.float32
BF16 = jnp.bfloat16

NORM_EPS = 1e-6
NEG_BIG = -1e30
ATTN_HEAD_DIM = 64
ATTN_HEADS = 8
ATTN_WIDTH = ATTN_HEADS * ATTN_HEAD_DIM
DILATED_PATTERNS = ((128, 1), (512, 4), (2048, 16))
ALIBI_MAX_BIAS = 8.0
HGRN_DIM = 128
HGRN_HEADS = 4
HGRN_WIDTH = HGRN_HEADS * HGRN_DIM
N_EXPERTS = 16
CAPACITY_FACTOR = 2
V7X_VMEM_LIMIT_BYTES = 56 * 1024 * 1024


def _dot(a, b, dims=((1,), (0,))):
    return lax.dot_general(a, b, (dims, ((), ())), preferred_element_type=F32)


def _dot_nt(a, b):
    return _dot(a, b, ((1,), (1,)))


def _cast_chunks(src_ref, dst_ref, rows=256):
    for r0 in range(0, src_ref.shape[0], rows):
        dst_ref[r0:r0 + rows, :] = src_ref[r0:r0 + rows, :].astype(dst_ref.dtype)


def _in_proj_kernel(x_ref, nw_ref, w32_ref, qw_ref, kw_ref, o_ref, w_ref, *, n_chunk):
    @pl.when(pl.program_id(0) == 0)
    def _():
        _cast_chunks(w32_ref, w_ref)

    x = x_ref[...]
    h = x * lax.rsqrt(jnp.mean(x * x, axis=-1, keepdims=True) + NORM_EPS) * nw_ref[...]
    h = h.astype(BF16)
    n_total = o_ref.shape[1]
    for c in range(n_total // n_chunk):
        cols = slice(c * n_chunk, (c + 1) * n_chunk)
        o_ref[:, cols] = _dot(h, w_ref[:, cols])
    for sl in range(ATTN_WIDTH // LANES):
        cols = slice(sl * LANES, (sl + 1) * LANES)
        o_ref[:, cols] = _head_rms(o_ref[:, cols], qw_ref[:, cols]) * (ATTN_HEAD_DIM ** -0.5)
        kcols = slice(ATTN_WIDTH + sl * LANES, ATTN_WIDTH + (sl + 1) * LANES)
        o_ref[:, kcols] = _head_rms(o_ref[:, kcols], kw_ref[:, cols])


def in_proj(x2d, norm_w, w, qw, kw, *, tm=512, n_chunk=512):
    m, d = x2d.shape
    n = w.shape[1]
    wspec = pl.BlockSpec((1, ATTN_WIDTH), lambda i: (0, 0))
    return pl.pallas_call(
        functools.partial(_in_proj_kernel, n_chunk=n_chunk),
        grid=(m // tm,),
        in_specs=[
            pl.BlockSpec((tm, d), lambda i: (i, 0)),
            pl.BlockSpec((1, d), lambda i: (0, 0)),
            pl.BlockSpec((d, n), lambda i: (0, 0), pipeline_mode=pl.Buffered(1)),
            wspec, wspec,
        ],
        out_specs=pl.BlockSpec((tm, n), lambda i: (i, 0)),
        out_shape=jax.ShapeDtypeStruct((m, n), F32),
        scratch_shapes=[pltpu.VMEM((d, n), BF16)],
        compiler_params=pltpu.CompilerParams(
            dimension_semantics=("arbitrary",), vmem_limit_bytes=V7X_VMEM_LIMIT_BYTES),
    )(x2d, norm_w.reshape(1, d), w,
      jnp.tile(qw, ATTN_HEADS).reshape(1, -1), jnp.tile(kw, ATTN_HEADS).reshape(1, -1))


HGRN_CHUNK = 64
HGRN_SUB = 16
HGRN_MAX_LOG_DECAY = 80.0


def _hgrn_chunk(load, store, lb, st_ref, *, reverse, bounded_decay):
    q_raw, z, v = load()
    c, dk = q_raw.shape
    sub = HGRN_SUB
    q = q_raw * jax.nn.sigmoid(q_raw)
    ez = jnp.exp(-jnp.abs(z))
    inv = 1.0 / (1.0 + ez)
    pos = z >= 0.0
    f = lb + (1.0 - lb) * jnp.where(pos, inv, ez * inv)
    k = (1.0 - lb) * jnp.where(pos, ez * inv, inv)
    g = jnp.log(f)

    row = lax.broadcasted_iota(jnp.int32, (c, c), 0)
    col = lax.broadcasted_iota(jnp.int32, (c, c), 1)
    tri = (col >= row) if reverse else (col <= row)
    tri_bf = jnp.where(tri, 1.0, 0.0).astype(BF16)
    g_hi, g_lo = _split2(g)
    yield
    cum = _dot(tri_bf, g_hi) + _dot(tri_bf, g_lo)
    yield

    def edge(r):
        return cum[r:r + 1, :]

    assert c == 4 * sub
    half = 2 * sub
    if reverse:
        ref_half, ref_q_lo, ref_q_hi, ref_end = edge(half), edge(sub), edge(half + sub), edge(0)
    else:
        ref_half, ref_q_lo, ref_q_hi, ref_end = edge(half - 1), edge(sub - 1), edge(half + sub - 1), edge(c - 1)

    r1 = lax.broadcasted_iota(jnp.int32, (c, 1), 0)
    ref_quarter = jnp.where(r1 < half, ref_q_lo, ref_q_hi)

    v_bf = v.astype(BF16)
    tb, sb = row // sub, col // sub
    if reverse:
        m1 = (tb < 2) & (sb >= 2)
        m2 = ((tb == 0) & (sb == 1)) | ((tb == 2) & (sb == 3))
    else:
        m1 = (tb >= 2) & (sb < 2)
        m2 = ((tb == 1) & (sb == 0)) | ((tb == 3) & (sb == 2))
    if bounded_decay:
        first = (sub - 1) if reverse else 0
        blocks = [slice(b * sub, (b + 1) * sub) for b in range(c // sub)]
        refs_d = [edge(b * sub + first) for b in range(c // sub)]
        q3 = jnp.concatenate([q[rows] * jnp.exp(cum[rows] - d) for rows, d in zip(blocks, refs_d)], axis=0)
        k3 = jnp.concatenate([k[rows] * jnp.exp(d - cum[rows]) for rows, d in zip(blocks, refs_d)], axis=0)

        def scaled(x, factors):
            return jnp.concatenate([x[rows] * f for rows, f in zip(blocks, factors)], axis=0).astype(BF16)

        refs_q = [ref_q_lo, ref_q_lo, ref_q_hi, ref_q_hi]
        qe = scaled(q3, [jnp.exp(d) for d in refs_d])
        ke = scaled(k3, [jnp.exp(ref_end - d) for d in refs_d])
        q1 = scaled(q3, [jnp.exp(jnp.minimum(d - ref_half, 0.0)) for d in refs_d])
        k1 = scaled(k3, [jnp.exp(jnp.minimum(ref_half - d, 0.0)) for d in refs_d])
        q2 = scaled(q3, [jnp.exp(jnp.minimum(d - rq, 0.0)) for d, rq in zip(refs_d, refs_q)])
        k2 = scaled(k3, [jnp.exp(jnp.minimum(rq - d, 0.0)) for d, rq in zip(refs_d, refs_q)])
        q3, k3 = q3.astype(BF16), k3.astype(BF16)
        m3 = (tb == sb) & ((col >= row) if reverse else (col <= row))
    else:
        qe = (q * jnp.exp(cum)).astype(BF16)
        ke = (k * jnp.exp(ref_end - cum)).astype(BF16)
        q1 = (q * jnp.exp(jnp.minimum(cum - ref_half, 0.0))).astype(BF16)
        k1 = (k * jnp.exp(jnp.minimum(ref_half - cum, 0.0))).astype(BF16)
        q2 = (q * jnp.exp(jnp.minimum(cum - ref_quarter, 0.0))).astype(BF16)
        k2 = (k * jnp.exp(jnp.minimum(ref_quarter - cum, 0.0))).astype(BF16)
    yield

    st = st_ref[...]
    o = _dot_nt(qe, st.astype(BF16))
    st_ref[...] = st * jnp.exp(ref_end) + _dot(v_bf, ke, ((0,), (0,)))
    a = jnp.where(m1, _dot_nt(q1, k1), 0.0) + jnp.where(m2, _dot_nt(q2, k2), 0.0)
    if bounded_decay:
        a = a + jnp.where(m3, _dot_nt(q3, k3), 0.0)
        a_bf = a.astype(BF16)
        yield
        store(o + _dot(a_bf, v_bf))
        return
    yield

    t_loc = lax.broadcasted_iota(jnp.int32, (sub, 1), 0)
    lane = lax.broadcasted_iota(jnp.int32, (sub, c), 1)
    blocks = []
    for blk in range(c // sub):
        rows = slice(blk * sub, (blk + 1) * sub)
        cum_b, q_b = cum[rows], q[rows]
        a_b = jnp.zeros((sub, c), F32)
        for s_loc in range(sub):
            s = blk * sub + s_loc
            keep = (t_loc <= s_loc) if reverse else (t_loc >= s_loc)
            e = jnp.exp(jnp.where(keep, cum_b - cum[s:s + 1, :], NEG_BIG))
            p = (q_b * k[s:s + 1, :]) * e
            a_b = jnp.where(lane == s, jnp.sum(p, axis=-1, keepdims=True), a_b)
        blocks.append(a_b)
    a = a + jnp.concatenate(blocks, axis=0)
    store(o + _dot(a.astype(BF16), v_bf))


def _hgrn_kernel(bounded_ref, qf_ref, zf_ref, vf_ref, qb_ref, zb_ref, vb_ref, lbf_ref, lbb_ref,
                 of_ref, ob_ref, sf_ref, sb_ref):
    @pl.when(pl.program_id(1) == 0)
    def _():
        sf_ref[...] = jnp.zeros_like(sf_ref)
        sb_ref[...] = jnp.zeros_like(sb_ref)

    n_chunks = qf_ref.shape[0] // HGRN_CHUNK

    def run(bounded_decay):
        per_step = 4 if bounded_decay and n_chunks % 4 == 0 else 1

        def body(step, carry):
            chains = []
            for u in range(per_step):
                ci = step * per_step + u
                rf = pl.ds(pl.multiple_of(ci * HGRN_CHUNK, HGRN_CHUNK), HGRN_CHUNK)
                rb = pl.ds(pl.multiple_of((n_chunks - 1 - ci) * HGRN_CHUNK, HGRN_CHUNK), HGRN_CHUNK)
                for h in range(HGRN_HEADS):
                    cols = slice(h * HGRN_DIM, (h + 1) * HGRN_DIM)
                    for rows, q_ref, z_ref, v_ref, lb_ref, o_ref, st_ref, reverse in (
                            (rf, qf_ref, zf_ref, vf_ref, lbf_ref, of_ref, sf_ref, False),
                            (rb, qb_ref, zb_ref, vb_ref, lbb_ref, ob_ref, sb_ref, True)):
                        def load(rows=rows, cols=cols, q_ref=q_ref, z_ref=z_ref, v_ref=v_ref):
                            return q_ref[rows, cols], z_ref[rows, cols], v_ref[rows, cols]

                        def store(o, rows=rows, cols=cols, o_ref=o_ref):
                            o_ref[rows, cols] = o.astype(o_ref.dtype)

                        chains.append(_hgrn_chunk(load, store, lb_ref[:, cols], st_ref.at[h],
                                                  reverse=reverse, bounded_decay=bounded_decay))
            while chains:
                chains = [ch for ch in chains if next(ch, True) is None]
            return carry

        lax.fori_loop(0, n_chunks // per_step, body, 0)

    pl.when(bounded_ref[0] == 1)(functools.partial(run, True))
    pl.when(bounded_ref[0] != 1)(functools.partial(run, False))


def hgrn(proj, lb_f, lb_b, *, batch, seq, col0, tile=512):
    nt = seq // tile
    cb = col0 // HGRN_WIDTH
    worst = -(HGRN_SUB - 1) * jnp.log(jnp.minimum(jnp.min(lb_f), jnp.min(lb_b)))
    bounded = (worst <= HGRN_MAX_LOG_DECAY).astype(jnp.int32).reshape(1)

    def fwd(colblock):
        return pl.BlockSpec((tile, HGRN_WIDTH), lambda b, i, flag: (b * nt + i, cb + colblock))

    def bwd(colblock):
        return pl.BlockSpec((tile, HGRN_WIDTH), lambda b, i, flag: (b * nt + nt - 1 - i, cb + colblock))

    lb_spec = pl.BlockSpec((1, HGRN_WIDTH), lambda b, i, flag: (0, 0))
    out_shape = jax.ShapeDtypeStruct((batch * seq, HGRN_WIDTH), BF16)
    state = pltpu.VMEM((HGRN_HEADS, HGRN_DIM, HGRN_DIM), F32)
    grid_spec = pltpu.PrefetchScalarGridSpec(
        num_scalar_prefetch=1,
        grid=(batch, nt),
        in_specs=[fwd(0), fwd(1), fwd(3), bwd(0), bwd(2), bwd(3), lb_spec, lb_spec],
        out_specs=[
            pl.BlockSpec((tile, HGRN_WIDTH), lambda b, i, flag: (b * nt + i, 0)),
            pl.BlockSpec((tile, HGRN_WIDTH), lambda b, i, flag: (b * nt + nt - 1 - i, 0)),
        ],
        scratch_shapes=[state, state],
    )
    return pl.pallas_call(
        _hgrn_kernel,
        grid_spec=grid_spec,
        out_shape=[out_shape, out_shape],
        compiler_params=pltpu.CompilerParams(
            dimension_semantics=("arbitrary", "arbitrary"),
            vmem_limit_bytes=V7X_VMEM_LIMIT_BYTES),
    )(bounded, proj, proj, proj, proj, proj, proj, lb_f.reshape(1, -1), lb_b.reshape(1, -1))


ATTN_HALF = 64
ATTN_QT = 128
LANES = 128
ATTN_SLABS = ATTN_WIDTH // LANES


def _head_rms(xs, w):
    lo = lax.broadcasted_iota(jnp.int32, (1, LANES), 1) < ATTN_HEAD_DIM
    sq = xs * xs
    s_lo = jnp.sum(jnp.where(lo, sq, 0.0), axis=-1, keepdims=True)
    s_hi = jnp.sum(jnp.where(lo, 0.0, sq), axis=-1, keepdims=True)
    ms = jnp.where(lo, s_lo, s_hi) * (1.0 / ATTN_HEAD_DIM)
    return xs * lax.rsqrt(ms + NORM_EPS) * w


def _attn_kernel(shifted_ref, *refs, seq):
    pl.when(shifted_ref[0] == 1)(lambda: _attn_tile(*refs, seq=seq, shifted=True))
    pl.when(shifted_ref[0] != 1)(lambda: _attn_tile(*refs, seq=seq, shifted=False))


def _attn_tile(*refs, seq, shifted):
    n_pat = len(DILATED_PATTERNS)
    ns = ATTN_SLABS
    q_refs, k_refs, kp_refs, kn_refs, v_refs, vp_refs, vn_refs = [refs[g * ns:(g + 1) * ns] for g in range(7)]
    bias_refs = refs[7 * ns:7 * ns + n_pat]
    o_ref, kwin, vwin, qn, s_scr, o_scr, l_scr = refs[7 * ns + n_pat:]
    tile = o_ref.shape[0]
    i = pl.program_id(1)
    lo = lax.broadcasted_iota(jnp.int32, (1, LANES), 1) < ATTN_HEAD_DIM

    for pi, ((_, dil), bias_ref) in enumerate(zip(DILATED_PATTERNS, bias_refs)):
        sub_rows = tile // dil
        qt = min(ATTN_QT, sub_rows)
        kt = qt + 2 * ATTN_HALF
        halo = ATTN_HALF * dil
        kcol = lax.broadcasted_iota(jnp.int32, (1, kt), 1)
        n_sub = sub_rows // qt

        def fill(r, ws, dil=dil, sub_rows=sub_rows, halo=halo):
            def rows_of(ref, start, n):
                return ref[pl.ds(start + r, n, stride=dil), :]

            main = slice(ATTN_HALF, ATTN_HALF + sub_rows)
            after = slice(ATTN_HALF + sub_rows, 2 * ATTN_HALF + sub_rows)
            for sl in range(ns):
                w = ws * ns + sl
                kwin[w, 0:ATTN_HALF, :] = rows_of(kp_refs[sl], tile - halo, ATTN_HALF).astype(BF16)
                kwin[w, main, :] = rows_of(k_refs[sl], 0, sub_rows).astype(BF16)
                kwin[w, after, :] = rows_of(kn_refs[sl], 0, ATTN_HALF).astype(BF16)
                vwin[w, 0:ATTN_HALF, :] = rows_of(vp_refs[sl], tile - halo, ATTN_HALF).astype(BF16)
                vwin[w, main, :] = rows_of(v_refs[sl], 0, sub_rows).astype(BF16)
                vwin[w, after, :] = rows_of(vn_refs[sl], 0, ATTN_HALF).astype(BF16)
                qn[w, 0:sub_rows, :] = rows_of(q_refs[sl], 0, sub_rows).astype(BF16)

        def sub(j, r, ws, ss, pi=pi, dil=dil, bias_ref=bias_ref, qt=qt, kt=kt, kcol=kcol):
            r0 = j * qt if isinstance(j, int) else pl.multiple_of(j * qt, qt)
            lk = (i * tile) // dil + j * qt - ATTN_HALF + kcol
            edge = jnp.where((lk >= 0) & (lk < seq // dil), 0.0, NEG_BIG)
            out_rows = pl.ds(r + j * (qt * dil), qt, stride=dil)
            for sl in range(ns):
                qs = qn[ws * ns + sl, pl.ds(r0, qt), :]
                ks = kwin[ws * ns + sl, pl.ds(r0, kt), :]
                for hh in range(2):
                    qh = jnp.where(lo if hh == 0 else jnp.logical_not(lo), qs, jnp.zeros_like(qs))
                    s_scr[ss * ATTN_HEADS + 2 * sl + hh, 0:qt, 0:kt] = (
                        _dot_nt(qh, ks) + bias_ref[2 * sl + hh] + edge)
            for sl in range(ns):
                vs = vwin[ws * ns + sl, pl.ds(r0, kt), :]
                o_slab = l_slab = None
                for hh in range(2):
                    s = s_scr[ss * ATTN_HEADS + 2 * sl + hh, 0:qt, 0:kt]
                    if shifted:
                        p = jnp.exp(s)
                    else:
                        m = jnp.max(s, axis=-1, keepdims=True)
                        p = jnp.exp(s - m)
                    den = jnp.sum(p, axis=-1, keepdims=True)
                    oh = _dot(p.astype(BF16), vs) * (1.0 / den)
                    lse = jnp.log(den) if shifted else m + jnp.log(den)
                    o_slab = oh if hh == 0 else jnp.where(lo, o_slab, oh)
                    l_slab = lse if hh == 0 else jnp.where(lo, l_slab, lse)
                o_scr[pi * ns + sl, out_rows, :] = o_slab
                l_scr[pi * ns + sl, out_rows, :] = l_slab

        if n_sub >= 2:
            assert n_sub % 2 == 0

            def subsequence(r, carry, fill=fill, sub=sub, n_sub=n_sub):
                fill(r, 0)
                if n_sub == 2:
                    sub(0, r, 0, 0)
                    sub(1, r, 0, 1)
                else:
                    def pair(jj, carry2):
                        sub(2 * jj, r, 0, 0)
                        sub(2 * jj + 1, r, 0, 1)
                        return carry2
                    lax.fori_loop(0, n_sub // 2, pair, 0)
                return carry

            lax.fori_loop(0, dil, subsequence, 0)
        else:
            assert dil % 2 == 0

            def subsequence_pair(rp, carry, fill=fill, sub=sub):
                fill(2 * rp, 0)
                fill(2 * rp + 1, 1)
                sub(0, 2 * rp, 0, 0)
                sub(0, 2 * rp + 1, 1, 1)
                return carry

            lax.fori_loop(0, dil // 2, subsequence_pair, 0)

    def merge(c, carry):
        rows = pl.ds(pl.multiple_of(c * ATTN_QT, ATTN_QT), ATTN_QT)
        for sl in range(ns):
            ls = [l_scr[p * ns + sl, rows, :] for p in range(n_pat)]
            mx = functools.reduce(jnp.maximum, ls)
            ws = [jnp.exp(l - mx) for l in ls]
            num = sum(w * o_scr[p * ns + sl, rows, :] for p, w in enumerate(ws))
            o_ref[rows, sl * LANES:(sl + 1) * LANES] = (num / sum(ws)).astype(o_ref.dtype)
        return carry

    lax.fori_loop(0, tile // ATTN_QT, merge, 0)


def _attn_bias(dilation, qt, shift):
    slopes = jnp.exp2(-ALIBI_MAX_BIAS * jnp.arange(1, ATTN_HEADS + 1, dtype=F32) / ATTN_HEADS)
    t = jnp.arange(qt)[:, None]
    j = jnp.arange(qt + 2 * ATTN_HALF)[None, :]
    dist = jnp.abs(j - ATTN_HALF - t)
    alibi = -slopes[:, None, None] * (dilation * dist).astype(F32)[None] - shift
    return jnp.where((dist <= ATTN_HALF)[None], alibi, NEG_BIG)


ATTN_MAX_SHIFT = 30.0


def attention(proj, qw, kw, *, batch, seq, tile=1024):
    nt = seq // tile
    n_pat = len(DILATED_PATTERNS)
    assert all(tile % (ATTN_HALF * dil) == 0 for _, dil in DILATED_PATTERNS)
    bound = (1.0 + 2.0 ** -6) * (ATTN_HEAD_DIM ** 0.5) * jnp.max(jnp.abs(qw)) * jnp.max(jnp.abs(kw))
    use_shift = bound <= ATTN_MAX_SHIFT
    shift = jnp.where(use_shift, bound, 0.0).astype(F32)

    def slabs(c, shift_tiles):
        def spec(sl):
            def index(b, i, flag):
                return (b * nt + jnp.clip(i + shift_tiles, 0, nt - 1), c * ATTN_SLABS + sl)
            return pl.BlockSpec((tile, LANES), index)
        return [spec(sl) for sl in range(ATTN_SLABS)]

    groups = [(0, 0), (1, 0), (1, -1), (1, 1), (2, 0), (2, -1), (2, 1)]
    biases = [_attn_bias(dil, min(ATTN_QT, tile // dil), shift) for _, dil in DILATED_PATTERNS]
    bias_specs = [pl.BlockSpec(bias.shape, lambda b, i, flag: (0, 0, 0)) for bias in biases]
    grid_spec = pltpu.PrefetchScalarGridSpec(
        num_scalar_prefetch=1,
        grid=(batch, nt),
        in_specs=[s for c, shift_tiles in groups for s in slabs(c, shift_tiles)] + bias_specs,
        out_specs=pl.BlockSpec((tile, ATTN_WIDTH), lambda b, i, flag: (b * nt + i, 0)),
        scratch_shapes=[pltpu.VMEM((2 * ATTN_SLABS, tile + 2 * ATTN_HALF, LANES), BF16),
                        pltpu.VMEM((2 * ATTN_SLABS, tile + 2 * ATTN_HALF, LANES), BF16),
                        pltpu.VMEM((2 * ATTN_SLABS, tile, LANES), BF16),
                        pltpu.VMEM((2 * ATTN_HEADS, ATTN_QT, ATTN_QT + 2 * ATTN_HALF), F32),
                        pltpu.VMEM((n_pat * ATTN_SLABS, tile, LANES), F32),
                        pltpu.VMEM((n_pat * ATTN_SLABS, tile, LANES), F32)],
    )
    return pl.pallas_call(
        functools.partial(_attn_kernel, seq=seq),
        grid_spec=grid_spec,
        out_shape=jax.ShapeDtypeStruct((batch * seq, ATTN_WIDTH), BF16),
        compiler_params=pltpu.CompilerParams(
            dimension_semantics=("arbitrary", "arbitrary"), vmem_limit_bytes=V7X_VMEM_LIMIT_BYTES),
    )(use_shift.astype(jnp.int32).reshape(1), *([proj] * (len(groups) * ATTN_SLABS)), *biases)


def _split2(x):
    hi = x.astype(BF16)
    return hi, (x - hi.astype(F32)).astype(BF16)


def _out_proj_kernel(x_ref, a_ref, of_ref, ob_ref, hg_ref, hw_ref, wo32_ref, n2_ref, wr_ref,
                     x1_ref, h2_ref, aff_ref, wo_ref):
    @pl.when(pl.program_id(0) == 0)
    def _():
        _cast_chunks(wo32_ref, wo_ref)

    a_out = a_ref[...]
    o = of_ref[...].astype(F32) + ob_ref[...].astype(F32)
    hg = hg_ref[...]
    hw = hw_ref[...]
    b_parts = []
    for sl in range(HGRN_HEADS):
        cols = slice(sl * HGRN_DIM, (sl + 1) * HGRN_DIM)
        os_ = o[:, cols]
        y = os_ * lax.rsqrt(jnp.mean(os_ * os_, axis=-1, keepdims=True) + NORM_EPS) * hw
        g = hg[:, cols]
        b_parts.append((y * (g * jax.nn.sigmoid(g))).astype(BF16))
    mixed = jnp.concatenate([a_out] + b_parts, axis=-1)

    x1 = x_ref[...] + _dot(mixed, wo_ref[...])
    x1_ref[...] = x1
    h2 = x1 * lax.rsqrt(jnp.mean(x1 * x1, axis=-1, keepdims=True) + NORM_EPS) * n2_ref[...]
    d = x1.shape[1]
    h_hi, h_lo = _split2(h2)
    h2_ref[:, 0:d] = h_hi

    both = _dot(h_hi, wr_ref[...])
    logits = both[:, 0:LANES] + both[:, LANES:2 * LANES] + _dot(h_lo, wr_ref[:, 0:LANES])
    valid = lax.broadcasted_iota(jnp.int32, (1, LANES), 1) < N_EXPERTS
    logits = jnp.where(valid, logits, NEG_BIG)
    ex = jnp.exp(logits - jnp.max(logits, axis=-1, keepdims=True))
    aff = ex / jnp.sum(ex, axis=-1, keepdims=True)
    aff_ref[...] = aff
    a_hi, a_lo = _split2(aff)
    h2_ref[:, d:d + LANES] = a_hi
    h2_ref[:, d + LANES:d + 2 * LANES] = a_lo


def out_proj(x2d, a_out, o_f, o_b, proj, hgrn_norm_w, w_out, norm2_w, w_router, *, hg_col, tm=512):
    m, d = x2d.shape
    wr = jnp.pad(w_router, ((0, 0), (0, LANES - N_EXPERTS)))
    wr_pair = jnp.concatenate(_split2(wr), axis=1)

    def rows(width, colblock=0):
        return pl.BlockSpec((tm, width), lambda i: (i, colblock))

    def const(shape):
        return pl.BlockSpec(shape, lambda i: (0, 0))

    return pl.pallas_call(
        _out_proj_kernel,
        grid=(m // tm,),
        in_specs=[rows(d), rows(ATTN_WIDTH),
                  rows(HGRN_WIDTH), rows(HGRN_WIDTH), rows(HGRN_WIDTH, hg_col // HGRN_WIDTH),
                  const((1, HGRN_DIM)), const((ATTN_WIDTH + HGRN_WIDTH, d)),
                  const((1, d)), const((d, 2 * LANES))],
        out_specs=[rows(d), rows(d + 2 * LANES), rows(LANES)],
        out_shape=[jax.ShapeDtypeStruct((m, d), F32), jax.ShapeDtypeStruct((m, d + 2 * LANES), BF16),
                   jax.ShapeDtypeStruct((m, LANES), F32)],
        scratch_shapes=[pltpu.VMEM((ATTN_WIDTH + HGRN_WIDTH, d), BF16)],
        compiler_params=pltpu.CompilerParams(
            dimension_semantics=("arbitrary",), vmem_limit_bytes=V7X_VMEM_LIMIT_BYTES),
    )(x2d, a_out, o_f, o_b, proj, hgrn_norm_w.reshape(1, -1), w_out,
      norm2_w.reshape(1, -1), wr_pair)


ROUTE_BLOCK = 256
COUNT_ROWS = 512


def _routing_kernel(aff_ref, slot_ref, start_ref, *, cap):
    seq = aff_ref.shape[0]

    def count(pred):
        def body(c, acc):
            blk = aff_ref[pl.ds(pl.multiple_of(c * COUNT_ROWS, COUNT_ROWS), COUNT_ROWS), :]
            hits = jnp.where(pred(blk), 1, 0).reshape(COUNT_ROWS // 8, 8, LANES)
            return acc + jnp.sum(hits, axis=0)
        acc = lax.fori_loop(0, seq // COUNT_ROWS, body, jnp.zeros((8, LANES), jnp.int32), unroll=4)
        return jnp.sum(acc, axis=0, keepdims=True)

    def bit_step(t, thr_bits):
        cand = thr_bits | jnp.left_shift(jnp.int32(1), 30 - t)
        cand_f = pltpu.bitcast(cand, F32)
        return jnp.where(count(lambda blk: blk >= cand_f) >= cap, cand, thr_bits)

    thr = pltpu.bitcast(lax.fori_loop(0, 31, bit_step, jnp.zeros((1, LANES), jnp.int32)), F32)
    need = (cap - count(lambda blk: blk > thr)).astype(F32)

    row = lax.broadcasted_iota(jnp.int32, (ROUTE_BLOCK, ROUTE_BLOCK), 0)
    col = lax.broadcasted_iota(jnp.int32, (ROUTE_BLOCK, ROUTE_BLOCK), 1)
    before = jnp.where(col < row, 1.0, 0.0).astype(BF16)

    group = 2

    def assign(jg, carry):
        c_eq, c_sel = carry
        starts = [pl.multiple_of((jg * group + u) * ROUTE_BLOCK, ROUTE_BLOCK) for u in range(group)]
        blks = [aff_ref[pl.ds(r0, ROUTE_BLOCK), :] for r0 in starts]
        gts = [blk > thr for blk in blks]
        eqs = [blk == thr for blk in blks]
        eq_fs = [jnp.where(eq, 1.0, 0.0) for eq in eqs]
        eq_pre = [_dot(before, eq_f.astype(BF16)) for eq_f in eq_fs]
        sels = []
        for u in range(group):
            sels.append(gts[u] | (eqs[u] & (eq_pre[u] + c_eq < need)))
            c_eq = c_eq + jnp.sum(eq_fs[u], axis=0, keepdims=True)
        sel_fs = [jnp.where(sel, 1.0, 0.0) for sel in sels]
        sel_pre = [_dot(before, sel_f.astype(BF16)) for sel_f in sel_fs]
        for u in range(group):
            slot_ref[pl.ds(starts[u], ROUTE_BLOCK), :] = jnp.where(sels[u], sel_pre[u] + c_sel, -1.0).astype(jnp.int32)
            start_ref[pl.ds(jg * group + u, 1), :] = c_sel.astype(jnp.int32)
            c_sel = c_sel + jnp.sum(sel_fs[u], axis=0, keepdims=True)
        return c_eq, c_sel

    zero = jnp.zeros((1, LANES), F32)
    lax.fori_loop(0, seq // (ROUTE_BLOCK * group), assign, (zero, zero))


def routing(aff, *, batch, seq, cap):
    nblk = seq // ROUTE_BLOCK
    return pl.pallas_call(
        functools.partial(_routing_kernel, cap=cap),
        grid=(batch,),
        in_specs=[pl.BlockSpec((seq, LANES), lambda b: (b, 0))],
        out_specs=[pl.BlockSpec((seq, LANES), lambda b: (b, 0)), pl.BlockSpec((nblk, LANES), lambda b: (b, 0))],
        out_shape=[jax.ShapeDtypeStruct((batch * seq, LANES), jnp.int32),
                   jax.ShapeDtypeStruct((batch * nblk, LANES), jnp.int32)],
        compiler_params=pltpu.CompilerParams(
            dimension_semantics=("arbitrary",), vmem_limit_bytes=V7X_VMEM_LIMIT_BYTES),
    )(aff)


SLOT_ALIGN = 16
SLOT_WIN = ROUTE_BLOCK + SLOT_ALIGN
SLOT_WIN_SMALL = 64
FFN_ROWS = 256


def _one_hots(slot_ref, tok, bases, win):
    r = lax.broadcasted_iota(jnp.int32, (win, ROUTE_BLOCK), 0)
    return jnp.concatenate(
        [jnp.where((slot_ref[0, e, :, tok] - base) == r, 1.0, 0.0).astype(BF16) for e, base in enumerate(bases)],
        axis=0)


BLOCKS_PER_STEP = 4


def _block_windows(base_ref, fits_ref, j, n_blocks):
    b = pl.program_id(0)
    n_exp = N_EXPERTS
    bases = [pl.multiple_of(base_ref[(b * n_exp + e) * n_blocks + j], SLOT_ALIGN) for e in range(n_exp)]
    return bases, fits_ref[b * n_blocks + j] == 1


def _for_block_groups(base_ref, fits_ref, first, n_groups, n_blocks, run):
    def group(g, carry):
        j0 = first + g * BLOCKS_PER_STEP
        js = [j0 + u for u in range(BLOCKS_PER_STEP)]
        windows = [_block_windows(base_ref, fits_ref, j, n_blocks) for j in js]
        blocks = [(j, bases) for j, (bases, _) in zip(js, windows)]
        fits = functools.reduce(jnp.logical_and, [f for _, f in windows])
        pl.when(fits)(functools.partial(run, SLOT_WIN_SMALL, blocks))

        @pl.when(jnp.logical_not(fits))
        def _():
            def one(u, carry2):
                bases, _ = _block_windows(base_ref, fits_ref, j0 + u, n_blocks)
                run(SLOT_WIN, [(j0 + u, bases)])
                return carry2
            lax.fori_loop(0, BLOCKS_PER_STEP, one, 0)

        return carry

    lax.fori_loop(0, n_groups, group, 0)


def _token_rows(j):
    return pl.ds(pl.multiple_of(j * ROUTE_BLOCK, ROUTE_BLOCK), ROUTE_BLOCK)


def _gather_kernel(base_ref, fits_ref, slot_ref, h_ref, xin_ref, *, n_blocks):
    xin_ref[...] = jnp.zeros_like(xin_ref)

    def run(win, blocks):
        rows = [_dot(_one_hots(slot_ref, _token_rows(j), bases, win), h_ref[_token_rows(j), :]).astype(BF16)
                for j, bases in blocks]
        for (j, bases), rows_j in zip(blocks, rows):
            for e, base in enumerate(bases):
                xin_ref[0, e, pl.ds(base, win), :] += rows_j[e * win:(e + 1) * win]

    _for_block_groups(base_ref, fits_ref, 0, n_blocks // BLOCKS_PER_STEP, n_blocks, run)


def gather(base, fits, slot_t, h_ext, *, batch, seq, cap, tn=256):
    n_exp = slot_t.shape[1]
    width = h_ext.shape[1]
    rows = cap + SLOT_WIN
    grid_spec = pltpu.PrefetchScalarGridSpec(
        num_scalar_prefetch=2,
        grid=(batch, width // tn),
        in_specs=[pl.BlockSpec((1, n_exp, 1, seq), lambda b, n, base, fits: (b, 0, 0, 0)),
                  pl.BlockSpec((seq, tn), lambda b, n, base, fits: (b, n))],
        out_specs=pl.BlockSpec((1, n_exp, rows, tn), lambda b, n, base, fits: (b, 0, 0, n)),
    )
    return pl.pallas_call(
        functools.partial(_gather_kernel, n_blocks=seq // ROUTE_BLOCK),
        grid_spec=grid_spec,
        out_shape=jax.ShapeDtypeStruct((batch, n_exp, rows, width), BF16),
        compiler_params=pltpu.CompilerParams(
            dimension_semantics=("arbitrary", "arbitrary"), vmem_limit_bytes=V7X_VMEM_LIMIT_BYTES),
    )(base, fits, slot_t, h_ext)


def _ffn_kernel(xin_ref, g_ref, wg32_ref, wu32_ref, wd32_ref, y_ref, wg_ref, wu_ref, wd_ref, *, cap):
    e = pl.program_id(0)

    @pl.when(pl.program_id(1) == 0)
    def _():
        _cast_chunks(wg32_ref.at[0], wg_ref)
        _cast_chunks(wu32_ref.at[0], wu_ref)
        _cast_chunks(wd32_ref.at[0], wd_ref)

    lane = lax.broadcasted_iota(jnp.int32, (1, 2 * LANES), 1)
    mine = (lane == e) | (lane == LANES + e)
    for rb in range(cap // FFN_ROWS):
        rows = slice(rb * FFN_ROWS, (rb + 1) * FFN_ROWS)
        xb = xin_ref[0, 0, rows, :]
        gate = jnp.sum(jnp.where(mine, g_ref[0, 0, rows, :].astype(F32), 0.0), axis=-1, keepdims=True)
        gate_h = _dot(xb, wg_ref[...])
        hid = (gate_h * jax.nn.sigmoid(gate_h)) * _dot(xb, wu_ref[...])
        y_ref[0, 0, rows, :] = (_dot(hid.astype(BF16), wd_ref[...]) * gate).astype(BF16)
    y_ref[0, 0, cap:, :] = jnp.zeros((y_ref.shape[2] - cap, y_ref.shape[3]), BF16)


def expert_ffn(xin, wg, wu, wd, *, cap):
    batch, n_exp, rows, width = xin.shape
    _, d, f = wg.shape
    return pl.pallas_call(
        functools.partial(_ffn_kernel, cap=cap),
        grid=(n_exp, batch),
        in_specs=[pl.BlockSpec((1, 1, cap, d), lambda e, b: (b, e, 0, 0)),
                  pl.BlockSpec((1, 1, cap, 2 * LANES), lambda e, b: (b, e, 0, d // (2 * LANES))),
                  pl.BlockSpec((1, d, f), lambda e, b: (e, 0, 0)),
                  pl.BlockSpec((1, d, f), lambda e, b: (e, 0, 0)),
                  pl.BlockSpec((1, f, d), lambda e, b: (e, 0, 0))],
        out_specs=pl.BlockSpec((1, 1, rows, d), lambda e, b: (b, e, 0, 0)),
        out_shape=jax.ShapeDtypeStruct((batch, n_exp, rows, d), BF16),
        scratch_shapes=[pltpu.VMEM((d, f), BF16), pltpu.VMEM((d, f), BF16), pltpu.VMEM((f, d), BF16)],
        compiler_params=pltpu.CompilerParams(
            dimension_semantics=("arbitrary", "arbitrary"), vmem_limit_bytes=V7X_VMEM_LIMIT_BYTES),
    )(xin, xin, wg, wu, wd)


def _combine_kernel(base_ref, fits_ref, slot_ref, y_ref, x1_ref, out_ref, ywin, *, n_blocks):
    blocks_here = out_ref.shape[0] // ROUTE_BLOCK
    first = pl.program_id(2) * blocks_here

    def run(win, blocks):
        stacked = N_EXPERTS * win
        separate = stacked * len(blocks) <= ywin.shape[0]
        for u, (j, bases) in enumerate(blocks):
            off = u * stacked if separate else 0
            rows = _token_rows(j - first)
            for e, base in enumerate(bases):
                ywin[off + e * win:off + (e + 1) * win, :] = y_ref[0, e, pl.ds(base, win), :]
            hits = _one_hots(slot_ref, _token_rows(j), bases, win)
            out_ref[rows, :] = x1_ref[rows, :] + _dot(hits, ywin[off:off + stacked, :], ((0,), (0,)))

    _for_block_groups(base_ref, fits_ref, first, blocks_here // BLOCKS_PER_STEP, n_blocks, run)


def combine(base, fits, slot_t, y, x1, *, batch, seq, tn=512, tt=1024):
    n_exp, rows, d = y.shape[1], y.shape[2], y.shape[3]
    grid_spec = pltpu.PrefetchScalarGridSpec(
        num_scalar_prefetch=2,
        grid=(batch, d // tn, seq // tt),
        in_specs=[pl.BlockSpec((1, n_exp, 1, seq), lambda b, n, t, base, fits: (b, 0, 0, 0)),
                  pl.BlockSpec((1, n_exp, rows, tn), lambda b, n, t, base, fits: (b, 0, 0, n)),
                  pl.BlockSpec((tt, tn), lambda b, n, t, base, fits: (b * (seq // tt) + t, n))],
        out_specs=pl.BlockSpec((tt, tn), lambda b, n, t, base, fits: (b * (seq // tt) + t, n)),
        scratch_shapes=[pltpu.VMEM((n_exp * SLOT_WIN, tn), BF16)],
    )
    return pl.pallas_call(
        functools.partial(_combine_kernel, n_blocks=seq // ROUTE_BLOCK),
        grid_spec=grid_spec,
        out_shape=jax.ShapeDtypeStruct((batch * seq, d), F32),
        compiler_params=pltpu.CompilerParams(
            dimension_semantics=("arbitrary", "arbitrary", "arbitrary"),
            vmem_limit_bytes=V7X_VMEM_LIMIT_BYTES),
    )(base, fits, slot_t, y, x1)


def kernel(x, norm1_w, w_in, attn_q_norm_w, attn_k_norm_w, hgrn_lb_fwd, hgrn_lb_bwd, hgrn_out_norm_w,
           w_out, norm2_w, w_router, w_expert_gate, w_expert_up, w_expert_down):
    batch, seq, d_model = x.shape
    depth = w_in.shape[0]
    cap = max(1, CAPACITY_FACTOR * seq // N_EXPERTS)
    hgrn_col = 3 * ATTN_WIDTH
    lb_f_all = jnp.cumsum(jax.nn.softmax(hgrn_lb_fwd.astype(F32), axis=0), axis=0)
    lb_b_all = jnp.cumsum(jax.nn.softmax(hgrn_lb_bwd.astype(F32), axis=0), axis=0)

    x2d = x.reshape(batch * seq, d_model)
    for l in range(depth):
        proj = in_proj(x2d, norm1_w[l], w_in[l], attn_q_norm_w[l], attn_k_norm_w[l])
        o_f, o_b = hgrn(proj, lb_f_all[l], lb_b_all[l], batch=batch, seq=seq, col0=hgrn_col)
        a_out = attention(proj, attn_q_norm_w[l], attn_k_norm_w[l], batch=batch, seq=seq)
        x1, h_ext, aff = out_proj(x2d, a_out, o_f, o_b, proj, hgrn_out_norm_w[l], w_out[l],
                                  norm2_w[l], w_router[l], hg_col=hgrn_col + 4 * HGRN_WIDTH)
        slot, start = routing(aff, batch=batch, seq=seq, cap=cap)

        def expert_major(t):
            return t.reshape(batch, -1, LANES)[:, :, :N_EXPERTS].transpose(0, 2, 1)

        slot_t = expert_major(slot).reshape(batch, N_EXPERTS, 1, seq)
        start_t = expert_major(start)
        base_t = start_t // SLOT_ALIGN * SLOT_ALIGN
        end_t = jnp.concatenate([start_t[:, :, 1:], jnp.full((batch, N_EXPERTS, 1), cap, jnp.int32)], axis=2)
        fits = (jnp.max(end_t - base_t, axis=1) <= SLOT_WIN_SMALL).astype(jnp.int32).reshape(-1)
        base = base_t.reshape(-1)
        xin = gather(base, fits, slot_t, h_ext, batch=batch, seq=seq, cap=cap)
        y = expert_ffn(xin, w_expert_gate[l], w_expert_up[l], w_expert_down[l], cap=cap)
        x2d = combine(base, fits, slot_t, y, x1, batch=batch, seq=seq)
    return x2d.reshape(batch, seq, d_model)
```

```python
import functools

import jax
import jax.numpy as jnp
from jax import lax
from jax.experimental import pallas as pl
from jax.experimental.pallas import tpu as pltpu

F32 = jnp.float32
BF16 = jnp.bfloat16

NORM_EPS = 1e-6
NEG_BIG = -1e30
LOG2_E = 1.4426950408889634
LN_2 = 0.6931471805599453
ATTN_HEAD_DIM = 64
ATTN_HEADS = 8
ATTN_WIDTH = ATTN_HEADS * ATTN_HEAD_DIM
DILATED_PATTERNS = ((128, 1), (512, 4), (2048, 16))
ALIBI_MAX_BIAS = 8.0
HGRN_DIM = 128
HGRN_HEADS = 4
HGRN_WIDTH = HGRN_HEADS * HGRN_DIM
N_EXPERTS = 16
CAPACITY_FACTOR = 2
V7X_VMEM_LIMIT_BYTES = 56 * 1024 * 1024


def _dot(a, b, dims=((1,), (0,))):
    return lax.dot_general(a, b, (dims, ((), ())), preferred_element_type=F32)


def _dot_nt(a, b):
    return _dot(a, b, ((1,), (1,)))


def _cast_chunks(src_ref, dst_ref, rows=256):
    for r0 in range(0, src_ref.shape[0], rows):
        dst_ref[r0:r0 + rows, :] = src_ref[r0:r0 + rows, :].astype(dst_ref.dtype)


def _in_proj_kernel(x_ref, nw_ref, w32_ref, qw_ref, kw_ref, o_ref, w_ref, *, n_chunk):
    @pl.when(pl.program_id(0) == 0)
    def _():
        _cast_chunks(w32_ref, w_ref)

    x = x_ref[...]
    h = x * lax.rsqrt(jnp.mean(x * x, axis=-1, keepdims=True) + NORM_EPS) * nw_ref[...]
    h = h.astype(BF16)
    n_total = o_ref.shape[1]
    for c in range(n_total // n_chunk):
        cols = slice(c * n_chunk, (c + 1) * n_chunk)
        o_ref[:, cols] = _dot(h, w_ref[:, cols])
    for sl in range(ATTN_WIDTH // LANES):
        cols = slice(sl * LANES, (sl + 1) * LANES)
        o_ref[:, cols] = _head_rms(o_ref[:, cols], qw_ref[:, cols]) * (ATTN_HEAD_DIM ** -0.5 * LOG2_E)
        kcols = slice(ATTN_WIDTH + sl * LANES, ATTN_WIDTH + (sl + 1) * LANES)
        o_ref[:, kcols] = _head_rms(o_ref[:, kcols], kw_ref[:, cols])


def in_proj(x2d, norm_w, w, qw, kw, *, tm=512, n_chunk=512):
    m, d = x2d.shape
    n = w.shape[1]
    wspec = pl.BlockSpec((1, ATTN_WIDTH), lambda i: (0, 0))
    return pl.pallas_call(
        functools.partial(_in_proj_kernel, n_chunk=n_chunk),
        grid=(m // tm,),
        in_specs=[
            pl.BlockSpec((tm, d), lambda i: (i, 0)),
            pl.BlockSpec((1, d), lambda i: (0, 0)),
            pl.BlockSpec((d, n), lambda i: (0, 0), pipeline_mode=pl.Buffered(1)),
            wspec, wspec,
        ],
        out_specs=pl.BlockSpec((tm, n), lambda i: (i, 0)),
        out_shape=jax.ShapeDtypeStruct((m, n), F32),
        scratch_shapes=[pltpu.VMEM((d, n), BF16)],
        compiler_params=pltpu.CompilerParams(
            dimension_semantics=("arbitrary",), vmem_limit_bytes=V7X_VMEM_LIMIT_BYTES),
    )(x2d, norm_w.reshape(1, d), w,
      jnp.tile(qw, ATTN_HEADS).reshape(1, -1), jnp.tile(kw, ATTN_HEADS).reshape(1, -1))


HGRN_CHUNK = 64
HGRN_SUB = 16
HGRN_MAX_LOG_DECAY = 80.0


def _hgrn_chunk(load, store, lb, st_ref, *, reverse, bounded_decay):
    q_raw, z, v = load()
    c, dk = q_raw.shape
    sub = HGRN_SUB
    q = q_raw * jax.nn.sigmoid(q_raw)
    ez = jnp.exp(-jnp.abs(z))
    inv = 1.0 / (1.0 + ez)
    pos = z >= 0.0
    f = lb + (1.0 - lb) * jnp.where(pos, inv, ez * inv)
    k = (1.0 - lb) * jnp.where(pos, ez * inv, inv)
    g = jnp.log(f)

    row = lax.broadcasted_iota(jnp.int32, (c, c), 0)
    col = lax.broadcasted_iota(jnp.int32, (c, c), 1)
    tri = (col >= row) if reverse else (col <= row)
    tri_bf = jnp.where(tri, 1.0, 0.0).astype(BF16)
    g_hi, g_lo = _split2(g)
    yield
    cum = _dot(tri_bf, g_hi) + _dot(tri_bf, g_lo)
    yield

    def edge(r):
        return cum[r:r + 1, :]

    assert c == 4 * sub
    half = 2 * sub
    if reverse:
        ref_half, ref_q_lo, ref_q_hi, ref_end = edge(half), edge(sub), edge(half + sub), edge(0)
    else:
        ref_half, ref_q_lo, ref_q_hi, ref_end = edge(half - 1), edge(sub - 1), edge(half + sub - 1), edge(c - 1)

    r1 = lax.broadcasted_iota(jnp.int32, (c, 1), 0)
    ref_quarter = jnp.where(r1 < half, ref_q_lo, ref_q_hi)

    v_bf = v.astype(BF16)
    tb, sb = row // sub, col // sub
    if reverse:
        m1 = (tb < 2) & (sb >= 2)
        m2 = ((tb == 0) & (sb == 1)) | ((tb == 2) & (sb == 3))
    else:
        m1 = (tb >= 2) & (sb < 2)
        m2 = ((tb == 1) & (sb == 0)) | ((tb == 3) & (sb == 2))
    if bounded_decay:
        first = (sub - 1) if reverse else 0
        blocks = [slice(b * sub, (b + 1) * sub) for b in range(c // sub)]
        refs_d = [edge(b * sub + first) for b in range(c // sub)]
        q3 = jnp.concatenate([q[rows] * jnp.exp(cum[rows] - d) for rows, d in zip(blocks, refs_d)], axis=0)
        k3 = jnp.concatenate([k[rows] * jnp.exp(d - cum[rows]) for rows, d in zip(blocks, refs_d)], axis=0)

        def scaled(x, factors):
            return jnp.concatenate([x[rows] * f for rows, f in zip(blocks, factors)], axis=0).astype(BF16)

        refs_q = [ref_q_lo, ref_q_lo, ref_q_hi, ref_q_hi]
        qe = scaled(q3, [jnp.exp(d) for d in refs_d])
        ke = scaled(k3, [jnp.exp(ref_end - d) for d in refs_d])
        q1 = scaled(q3, [jnp.exp(jnp.minimum(d - ref_half, 0.0)) for d in refs_d])
        k1 = scaled(k3, [jnp.exp(jnp.minimum(ref_half - d, 0.0)) for d in refs_d])
        q2 = scaled(q3, [jnp.exp(jnp.minimum(d - rq, 0.0)) for d, rq in zip(refs_d, refs_q)])
        k2 = scaled(k3, [jnp.exp(jnp.minimum(rq - d, 0.0)) for d, rq in zip(refs_d, refs_q)])
        q3, k3 = q3.astype(BF16), k3.astype(BF16)
        m3 = (tb == sb) & ((col >= row) if reverse else (col <= row))
    else:
        qe = (q * jnp.exp(cum)).astype(BF16)
        ke = (k * jnp.exp(ref_end - cum)).astype(BF16)
        q1 = (q * jnp.exp(jnp.minimum(cum - ref_half, 0.0))).astype(BF16)
        k1 = (k * jnp.exp(jnp.minimum(ref_half - cum, 0.0))).astype(BF16)
        q2 = (q * jnp.exp(jnp.minimum(cum - ref_quarter, 0.0))).astype(BF16)
        k2 = (k * jnp.exp(jnp.minimum(ref_quarter - cum, 0.0))).astype(BF16)
    yield

    st = st_ref[...]
    o = _dot_nt(qe, st.astype(BF16))
    st_ref[...] = st * jnp.exp(ref_end) + _dot(v_bf, ke, ((0,), (0,)))
    a = jnp.where(m1, _dot_nt(q1, k1), 0.0) + jnp.where(m2, _dot_nt(q2, k2), 0.0)
    if bounded_decay:
        a = a + jnp.where(m3, _dot_nt(q3, k3), 0.0)
        a_bf = a.astype(BF16)
        yield
        store(o + _dot(a_bf, v_bf))
        return
    yield

    t_loc = lax.broadcasted_iota(jnp.int32, (sub, 1), 0)
    lane = lax.broadcasted_iota(jnp.int32, (sub, c), 1)
    blocks = []
    for blk in range(c // sub):
        rows = slice(blk * sub, (blk + 1) * sub)
        cum_b, q_b = cum[rows], q[rows]
        a_b = jnp.zeros((sub, c), F32)
        for s_loc in range(sub):
            s = blk * sub + s_loc
            keep = (t_loc <= s_loc) if reverse else (t_loc >= s_loc)
            e = jnp.exp(jnp.where(keep, cum_b - cum[s:s + 1, :], NEG_BIG))
            p = (q_b * k[s:s + 1, :]) * e
            a_b = jnp.where(lane == s, jnp.sum(p, axis=-1, keepdims=True), a_b)
        blocks.append(a_b)
    a = a + jnp.concatenate(blocks, axis=0)
    store(o + _dot(a.astype(BF16), v_bf))


def _hgrn_kernel(bounded_ref, qf_ref, zf_ref, vf_ref, qb_ref, zb_ref, vb_ref, lbf_ref, lbb_ref,
                 of_ref, ob_ref, sf_ref, sb_ref):
    @pl.when(pl.program_id(1) == 0)
    def _():
        sf_ref[...] = jnp.zeros_like(sf_ref)
        sb_ref[...] = jnp.zeros_like(sb_ref)

    n_chunks = qf_ref.shape[0] // HGRN_CHUNK

    def run(bounded_decay):
        per_step = 4 if bounded_decay and n_chunks % 4 == 0 else 1

        def body(step, carry):
            chains = []
            for u in range(per_step):
                ci = step * per_step + u
                rf = pl.ds(pl.multiple_of(ci * HGRN_CHUNK, HGRN_CHUNK), HGRN_CHUNK)
                rb = pl.ds(pl.multiple_of((n_chunks - 1 - ci) * HGRN_CHUNK, HGRN_CHUNK), HGRN_CHUNK)
                for h in range(HGRN_HEADS):
                    cols = slice(h * HGRN_DIM, (h + 1) * HGRN_DIM)
                    for rows, q_ref, z_ref, v_ref, lb_ref, o_ref, st_ref, reverse in (
                            (rf, qf_ref, zf_ref, vf_ref, lbf_ref, of_ref, sf_ref, False),
                            (rb, qb_ref, zb_ref, vb_ref, lbb_ref, ob_ref, sb_ref, True)):
                        def load(rows=rows, cols=cols, q_ref=q_ref, z_ref=z_ref, v_ref=v_ref):
                            return q_ref[rows, cols], z_ref[rows, cols], v_ref[rows, cols]

                        def store(o, rows=rows, cols=cols, o_ref=o_ref):
                            o_ref[rows, cols] = o.astype(o_ref.dtype)

                        chains.append(_hgrn_chunk(load, store, lb_ref[:, cols], st_ref.at[h],
                                                  reverse=reverse, bounded_decay=bounded_decay))
            while chains:
                chains = [ch for ch in chains if next(ch, True) is None]
            return carry

        lax.fori_loop(0, n_chunks // per_step, body, 0)

    pl.when(bounded_ref[0] == 1)(functools.partial(run, True))
    pl.when(bounded_ref[0] != 1)(functools.partial(run, False))


def hgrn(proj, lb_f, lb_b, *, batch, seq, col0, tile=512):
    nt = seq // tile
    cb = col0 // HGRN_WIDTH
    worst = -(HGRN_SUB - 1) * jnp.log(jnp.minimum(jnp.min(lb_f), jnp.min(lb_b)))
    bounded = (worst <= HGRN_MAX_LOG_DECAY).astype(jnp.int32).reshape(1)

    def fwd(colblock):
        return pl.BlockSpec((tile, HGRN_WIDTH), lambda b, i, flag: (b * nt + i, cb + colblock))

    def bwd(colblock):
        return pl.BlockSpec((tile, HGRN_WIDTH), lambda b, i, flag: (b * nt + nt - 1 - i, cb + colblock))

    lb_spec = pl.BlockSpec((1, HGRN_WIDTH), lambda b, i, flag: (0, 0))
    out_shape = jax.ShapeDtypeStruct((batch * seq, HGRN_WIDTH), BF16)
    state = pltpu.VMEM((HGRN_HEADS, HGRN_DIM, HGRN_DIM), F32)
    grid_spec = pltpu.PrefetchScalarGridSpec(
        num_scalar_prefetch=1,
        grid=(batch, nt),
        in_specs=[fwd(0), fwd(1), fwd(3), bwd(0), bwd(2), bwd(3), lb_spec, lb_spec],
        out_specs=[
            pl.BlockSpec((tile, HGRN_WIDTH), lambda b, i, flag: (b * nt + i, 0)),
            pl.BlockSpec((tile, HGRN_WIDTH), lambda b, i, flag: (b * nt + nt - 1 - i, 0)),
        ],
        scratch_shapes=[state, state],
    )
    return pl.pallas_call(
        _hgrn_kernel,
        grid_spec=grid_spec,
        out_shape=[out_shape, out_shape],
        compiler_params=pltpu.CompilerParams(
            dimension_semantics=("arbitrary", "arbitrary"),
            vmem_limit_bytes=V7X_VMEM_LIMIT_BYTES),
    )(bounded, proj, proj, proj, proj, proj, proj, lb_f.reshape(1, -1), lb_b.reshape(1, -1))


ATTN_HALF = 64
ATTN_QT = 128
LANES = 128
ATTN_SLABS = ATTN_WIDTH // LANES


def _head_rms(xs, w):
    lo = lax.broadcasted_iota(jnp.int32, (1, LANES), 1) < ATTN_HEAD_DIM
    sq = xs * xs
    s_lo = jnp.sum(jnp.where(lo, sq, 0.0), axis=-1, keepdims=True)
    s_hi = jnp.sum(jnp.where(lo, 0.0, sq), axis=-1, keepdims=True)
    ms = jnp.where(lo, s_lo, s_hi) * (1.0 / ATTN_HEAD_DIM)
    return xs * lax.rsqrt(ms + NORM_EPS) * w


def _attn_kernel(shifted_ref, *refs, seq):
    pl.when(shifted_ref[0] == 1)(lambda: _attn_tile(*refs, seq=seq, shifted=True))
    pl.when(shifted_ref[0] != 1)(lambda: _attn_tile(*refs, seq=seq, shifted=False))


def _attn_tile(*refs, seq, shifted):
    n_pat = len(DILATED_PATTERNS)
    ns = ATTN_SLABS
    q_refs, k_refs, kp_refs, kn_refs, v_refs, vp_refs, vn_refs = [refs[g * ns:(g + 1) * ns] for g in range(7)]
    bias_refs = refs[7 * ns:7 * ns + n_pat]
    o_ref, kwin, vwin, qn, s_scr, o_scr, l_scr = refs[7 * ns + n_pat:]
    tile = o_ref.shape[0]
    i = pl.program_id(1)
    lo = lax.broadcasted_iota(jnp.int32, (1, LANES), 1) < ATTN_HEAD_DIM

    for pi, ((_, dil), bias_ref) in enumerate(zip(DILATED_PATTERNS, bias_refs)):
        sub_rows = tile // dil
        qt = min(ATTN_QT, sub_rows)
        kt = qt + 2 * ATTN_HALF
        halo = ATTN_HALF * dil
        kcol = lax.broadcasted_iota(jnp.int32, (1, kt), 1)
        n_sub = sub_rows // qt

        def fill(r, ws, dil=dil, sub_rows=sub_rows, halo=halo):
            def rows_of(ref, start, n):
                return ref[pl.ds(start + r, n, stride=dil), :]

            main = slice(ATTN_HALF, ATTN_HALF + sub_rows)
            after = slice(ATTN_HALF + sub_rows, 2 * ATTN_HALF + sub_rows)
            for sl in range(ns):
                w = ws * ns + sl
                kwin[w, 0:ATTN_HALF, :] = rows_of(kp_refs[sl], tile - halo, ATTN_HALF).astype(BF16)
                kwin[w, main, :] = rows_of(k_refs[sl], 0, sub_rows).astype(BF16)
                kwin[w, after, :] = rows_of(kn_refs[sl], 0, ATTN_HALF).astype(BF16)
                vwin[w, 0:ATTN_HALF, :] = rows_of(vp_refs[sl], tile - halo, ATTN_HALF).astype(BF16)
                vwin[w, main, :] = rows_of(v_refs[sl], 0, sub_rows).astype(BF16)
                vwin[w, after, :] = rows_of(vn_refs[sl], 0, ATTN_HALF).astype(BF16)
                qn[w, 0:sub_rows, :] = rows_of(q_refs[sl], 0, sub_rows).astype(BF16)

        def sub(j, r, ws, ss, pi=pi, dil=dil, bias_ref=bias_ref, qt=qt, kt=kt, kcol=kcol):
            r0 = j * qt if isinstance(j, int) else pl.multiple_of(j * qt, qt)
            lk = (i * tile) // dil + j * qt - ATTN_HALF + kcol
            edge = jnp.where((lk >= 0) & (lk < seq // dil), 0.0, NEG_BIG)
            out_rows = pl.ds(r + j * (qt * dil), qt, stride=dil)
            for sl in range(ns):
                qs = qn[ws * ns + sl, pl.ds(r0, qt), :]
                ks = kwin[ws * ns + sl, pl.ds(r0, kt), :]
                for hh in range(2):
                    qh = jnp.where(lo if hh == 0 else jnp.logical_not(lo), qs, jnp.zeros_like(qs))
                    s_scr[ss * ATTN_HEADS + 2 * sl + hh, 0:qt, 0:kt] = (
                        _dot_nt(qh, ks) + bias_ref[2 * sl + hh] + edge)
            for sl in range(ns):
                vs = vwin[ws * ns + sl, pl.ds(r0, kt), :]
                o_slab = l_slab = None
                for hh in range(2):
                    s = s_scr[ss * ATTN_HEADS + 2 * sl + hh, 0:qt, 0:kt]
                    if shifted:
                        p = jnp.exp2(s)
                    else:
                        m = jnp.max(s, axis=-1, keepdims=True)
                        p = jnp.exp2(s - m)
                    den = jnp.sum(p, axis=-1, keepdims=True)
                    oh = _dot(p.astype(BF16), vs) * (1.0 / den)
                    lse = jnp.log(den) if shifted else m * LN_2 + jnp.log(den)
                    o_slab = oh if hh == 0 else jnp.where(lo, o_slab, oh)
                    l_slab = lse if hh == 0 else jnp.where(lo, l_slab, lse)
                o_scr[pi * ns + sl, out_rows, :] = o_slab
                l_scr[pi * ns + sl, out_rows, :] = l_slab

        if n_sub >= 2:
            assert n_sub % 2 == 0

            def subsequence(r, carry, fill=fill, sub=sub, n_sub=n_sub):
                fill(r, 0)
                if n_sub == 2:
                    sub(0, r, 0, 0)
                    sub(1, r, 0, 1)
                else:
                    def pair(jj, carry2):
                        sub(2 * jj, r, 0, 0)
                        sub(2 * jj + 1, r, 0, 1)
                        return carry2
                    lax.fori_loop(0, n_sub // 2, pair, 0)
                return carry

            lax.fori_loop(0, dil, subsequence, 0)
        else:
            assert dil % 2 == 0

            def subsequence_pair(rp, carry, fill=fill, sub=sub):
                fill(2 * rp, 0)
                fill(2 * rp + 1, 1)
                sub(0, 2 * rp, 0, 0)
                sub(0, 2 * rp + 1, 1, 1)
                return carry

            lax.fori_loop(0, dil // 2, subsequence_pair, 0)

    def merge(c, carry):
        rows = pl.ds(pl.multiple_of(c * ATTN_QT, ATTN_QT), ATTN_QT)
        for sl in range(ns):
            ls = [l_scr[p * ns + sl, rows, :] for p in range(n_pat)]
            mx = functools.reduce(jnp.maximum, ls)
            ws = [jnp.exp(l - mx) for l in ls]
            num = sum(w * o_scr[p * ns + sl, rows, :] for p, w in enumerate(ws))
            o_ref[rows, sl * LANES:(sl + 1) * LANES] = (num / sum(ws)).astype(o_ref.dtype)
        return carry

    lax.fori_loop(0, tile // ATTN_QT, merge, 0)


def _attn_bias(dilation, qt, shift):
    slopes = jnp.exp2(-ALIBI_MAX_BIAS * jnp.arange(1, ATTN_HEADS + 1, dtype=F32) / ATTN_HEADS)
    t = jnp.arange(qt)[:, None]
    j = jnp.arange(qt + 2 * ATTN_HALF)[None, :]
    dist = jnp.abs(j - ATTN_HALF - t)
    alibi = -slopes[:, None, None] * (dilation * dist).astype(F32)[None] - shift
    return jnp.where((dist <= ATTN_HALF)[None], alibi * LOG2_E, NEG_BIG)


ATTN_MAX_SHIFT = 30.0


def attention(proj, qw, kw, *, batch, seq, tile=1024):
    nt = seq // tile
    n_pat = len(DILATED_PATTERNS)
    assert all(tile % (ATTN_HALF * dil) == 0 for _, dil in DILATED_PATTERNS)
    bound = (1.0 + 2.0 ** -6) * (ATTN_HEAD_DIM ** 0.5) * jnp.max(jnp.abs(qw)) * jnp.max(jnp.abs(kw))
    use_shift = bound <= ATTN_MAX_SHIFT
    shift = jnp.where(use_shift, bound, 0.0).astype(F32)

    def slabs(c, shift_tiles):
        def spec(sl):
            def index(b, i, flag):
                return (b * nt + jnp.clip(i + shift_tiles, 0, nt - 1), c * ATTN_SLABS + sl)
            return pl.BlockSpec((tile, LANES), index)
        return [spec(sl) for sl in range(ATTN_SLABS)]

    groups = [(0, 0), (1, 0), (1, -1), (1, 1), (2, 0), (2, -1), (2, 1)]
    biases = [_attn_bias(dil, min(ATTN_QT, tile // dil), shift) for _, dil in DILATED_PATTERNS]
    bias_specs = [pl.BlockSpec(bias.shape, lambda b, i, flag: (0, 0, 0)) for bias in biases]
    grid_spec = pltpu.PrefetchScalarGridSpec(
        num_scalar_prefetch=1,
        grid=(batch, nt),
        in_specs=[s for c, shift_tiles in groups for s in slabs(c, shift_tiles)] + bias_specs,
        out_specs=pl.BlockSpec((tile, ATTN_WIDTH), lambda b, i, flag: (b * nt + i, 0)),
        scratch_shapes=[pltpu.VMEM((2 * ATTN_SLABS, tile + 2 * ATTN_HALF, LANES), BF16),
                        pltpu.VMEM((2 * ATTN_SLABS, tile + 2 * ATTN_HALF, LANES), BF16),
                        pltpu.VMEM((2 * ATTN_SLABS, tile, LANES), BF16),
                        pltpu.VMEM((2 * ATTN_HEADS, ATTN_QT, ATTN_QT + 2 * ATTN_HALF), F32),
                        pltpu.VMEM((n_pat * ATTN_SLABS, tile, LANES), F32),
                        pltpu.VMEM((n_pat * ATTN_SLABS, tile, LANES), F32)],
    )
    return pl.pallas_call(
        functools.partial(_attn_kernel, seq=seq),
        grid_spec=grid_spec,
        out_shape=jax.ShapeDtypeStruct((batch * seq, ATTN_WIDTH), BF16),
        compiler_params=pltpu.CompilerParams(
            dimension_semantics=("arbitrary", "arbitrary"), vmem_limit_bytes=V7X_VMEM_LIMIT_BYTES),
    )(use_shift.astype(jnp.int32).reshape(1), *([proj] * (len(groups) * ATTN_SLABS)), *biases)


def _split2(x):
    hi = x.astype(BF16)
    return hi, (x - hi.astype(F32)).astype(BF16)


def _out_proj_kernel(x_ref, a_ref, of_ref, ob_ref, hg_ref, hw_ref, wo32_ref, n2_ref, wr_ref,
                     x1_ref, h2_ref, aff_ref, wo_ref):
    @pl.when(pl.program_id(0) == 0)
    def _():
        _cast_chunks(wo32_ref, wo_ref)

    a_out = a_ref[...]
    o = of_ref[...].astype(F32) + ob_ref[...].astype(F32)
    hg = hg_ref[...]
    hw = hw_ref[...]
    b_parts = []
    for sl in range(HGRN_HEADS):
        cols = slice(sl * HGRN_DIM, (sl + 1) * HGRN_DIM)
        os_ = o[:, cols]
        y = os_ * lax.rsqrt(jnp.mean(os_ * os_, axis=-1, keepdims=True) + NORM_EPS) * hw
        g = hg[:, cols]
        b_parts.append((y * (g * jax.nn.sigmoid(g))).astype(BF16))
    mixed = jnp.concatenate([a_out] + b_parts, axis=-1)

    x1 = x_ref[...] + _dot(mixed, wo_ref[...])
    x1_ref[...] = x1
    h2 = x1 * lax.rsqrt(jnp.mean(x1 * x1, axis=-1, keepdims=True) + NORM_EPS) * n2_ref[...]
    d = x1.shape[1]
    h_hi, h_lo = _split2(h2)
    h2_ref[:, 0:d] = h_hi

    both = _dot(h_hi, wr_ref[...])
    logits = both[:, 0:LANES] + both[:, LANES:2 * LANES] + _dot(h_lo, wr_ref[:, 0:LANES])
    valid = lax.broadcasted_iota(jnp.int32, (1, LANES), 1) < N_EXPERTS
    logits = jnp.where(valid, logits, NEG_BIG)
    ex = jnp.exp(logits - jnp.max(logits, axis=-1, keepdims=True))
    aff = ex / jnp.sum(ex, axis=-1, keepdims=True)
    aff_ref[...] = aff
    a_hi, a_lo = _split2(aff)
    h2_ref[:, d:d + LANES] = a_hi
    h2_ref[:, d + LANES:d + 2 * LANES] = a_lo


def out_proj(x2d, a_out, o_f, o_b, proj, hgrn_norm_w, w_out, norm2_w, w_router, *, hg_col, tm=512):
    m, d = x2d.shape
    wr = jnp.pad(w_router, ((0, 0), (0, LANES - N_EXPERTS)))
    wr_pair = jnp.concatenate(_split2(wr), axis=1)

    def rows(width, colblock=0):
        return pl.BlockSpec((tm, width), lambda i: (i, colblock))

    def const(shape):
        return pl.BlockSpec(shape, lambda i: (0, 0))

    return pl.pallas_call(
        _out_proj_kernel,
        grid=(m // tm,),
        in_specs=[rows(d), rows(ATTN_WIDTH),
                  rows(HGRN_WIDTH), rows(HGRN_WIDTH), rows(HGRN_WIDTH, hg_col // HGRN_WIDTH),
                  const((1, HGRN_DIM)), const((ATTN_WIDTH + HGRN_WIDTH, d)),
                  const((1, d)), const((d, 2 * LANES))],
        out_specs=[rows(d), rows(d + 2 * LANES), rows(LANES)],
        out_shape=[jax.ShapeDtypeStruct((m, d), F32), jax.ShapeDtypeStruct((m, d + 2 * LANES), BF16),
                   jax.ShapeDtypeStruct((m, LANES), F32)],
        scratch_shapes=[pltpu.VMEM((ATTN_WIDTH + HGRN_WIDTH, d), BF16)],
        compiler_params=pltpu.CompilerParams(
            dimension_semantics=("arbitrary",), vmem_limit_bytes=V7X_VMEM_LIMIT_BYTES),
    )(x2d, a_out, o_f, o_b, proj, hgrn_norm_w.reshape(1, -1), w_out,
      norm2_w.reshape(1, -1), wr_pair)


ROUTE_BLOCK = 256
COUNT_ROWS = 512


def _routing_kernel(aff_ref, slot_ref, start_ref, *, cap):
    seq = aff_ref.shape[0]

    def count(pred):
        def body(c, acc):
            blk = aff_ref[pl.ds(pl.multiple_of(c * COUNT_ROWS, COUNT_ROWS), COUNT_ROWS), :]
            hits = jnp.where(pred(blk), 1, 0).reshape(COUNT_ROWS // 8, 8, LANES)
            return acc + jnp.sum(hits, axis=0)
        acc = lax.fori_loop(0, seq // COUNT_ROWS, body, jnp.zeros((8, LANES), jnp.int32), unroll=4)
        return jnp.sum(acc, axis=0, keepdims=True)

    def bit_step(t, thr_bits):
        cand = thr_bits | jnp.left_shift(jnp.int32(1), 30 - t)
        cand_f = pltpu.bitcast(cand, F32)
        return jnp.where(count(lambda blk: blk >= cand_f) >= cap, cand, thr_bits)

    thr = pltpu.bitcast(lax.fori_loop(0, 31, bit_step, jnp.zeros((1, LANES), jnp.int32)), F32)
    need = (cap - count(lambda blk: blk > thr)).astype(F32)

    row = lax.broadcasted_iota(jnp.int32, (ROUTE_BLOCK, ROUTE_BLOCK), 0)
    col = lax.broadcasted_iota(jnp.int32, (ROUTE_BLOCK, ROUTE_BLOCK), 1)
    before = jnp.where(col < row, 1.0, 0.0).astype(BF16)

    group = 2

    def assign(jg, carry):
        c_eq, c_sel = carry
        starts = [pl.multiple_of((jg * group + u) * ROUTE_BLOCK, ROUTE_BLOCK) for u in range(group)]
        blks = [aff_ref[pl.ds(r0, ROUTE_BLOCK), :] for r0 in starts]
        gts = [blk > thr for blk in blks]
        eqs = [blk == thr for blk in blks]
        eq_fs = [jnp.where(eq, 1.0, 0.0) for eq in eqs]
        eq_pre = [_dot(before, eq_f.astype(BF16)) for eq_f in eq_fs]
        sels = []
        for u in range(group):
            sels.append(gts[u] | (eqs[u] & (eq_pre[u] + c_eq < need)))
            c_eq = c_eq + jnp.sum(eq_fs[u], axis=0, keepdims=True)
        sel_fs = [jnp.where(sel, 1.0, 0.0) for sel in sels]
        sel_pre = [_dot(before, sel_f.astype(BF16)) for sel_f in sel_fs]
        for u in range(group):
            slot_ref[pl.ds(starts[u], ROUTE_BLOCK), :] = jnp.where(sels[u], sel_pre[u] + c_sel, -1.0).astype(jnp.int32)
            start_ref[pl.ds(jg * group + u, 1), :] = c_sel.astype(jnp.int32)
            c_sel = c_sel + jnp.sum(sel_fs[u], axis=0, keepdims=True)
        return c_eq, c_sel

    zero = jnp.zeros((1, LANES), F32)
    lax.fori_loop(0, seq // (ROUTE_BLOCK * group), assign, (zero, zero))


def routing(aff, *, batch, seq, cap):
    nblk = seq // ROUTE_BLOCK
    return pl.pallas_call(
        functools.partial(_routing_kernel, cap=cap),
        grid=(batch,),
        in_specs=[pl.BlockSpec((seq, LANES), lambda b: (b, 0))],
        out_specs=[pl.BlockSpec((seq, LANES), lambda b: (b, 0)), pl.BlockSpec((nblk, LANES), lambda b: (b, 0))],
        out_shape=[jax.ShapeDtypeStruct((batch * seq, LANES), jnp.int32),
                   jax.ShapeDtypeStruct((batch * nblk, LANES), jnp.int32)],
        compiler_params=pltpu.CompilerParams(
            dimension_semantics=("arbitrary",), vmem_limit_bytes=V7X_VMEM_LIMIT_BYTES),
    )(aff)


SLOT_ALIGN = 16
SLOT_WIN = ROUTE_BLOCK + SLOT_ALIGN
SLOT_WIN_SMALL = 64
FFN_ROWS = 256


def _one_hots(slot_ref, tok, bases, win):
    r = lax.broadcasted_iota(jnp.int32, (win, ROUTE_BLOCK), 0)
    return jnp.concatenate(
        [jnp.where((slot_ref[0, e, :, tok] - base) == r, 1.0, 0.0).astype(BF16) for e, base in enumerate(bases)],
        axis=0)


BLOCKS_PER_STEP = 4


def _block_windows(base_ref, fits_ref, j, n_blocks):
    b = pl.program_id(0)
    n_exp = N_EXPERTS
    bases = [pl.multiple_of(base_ref[(b * n_exp + e) * n_blocks + j], SLOT_ALIGN) for e in range(n_exp)]
    return bases, fits_ref[b * n_blocks + j] == 1


def _for_block_groups(base_ref, fits_ref, first, n_groups, n_blocks, run):
    def group(g, carry):
        j0 = first + g * BLOCKS_PER_STEP
        js = [j0 + u for u in range(BLOCKS_PER_STEP)]
        windows = [_block_windows(base_ref, fits_ref, j, n_blocks) for j in js]
        blocks = [(j, bases) for j, (bases, _) in zip(js, windows)]
        fits = functools.reduce(jnp.logical_and, [f for _, f in windows])
        pl.when(fits)(functools.partial(run, SLOT_WIN_SMALL, blocks))

        @pl.when(jnp.logical_not(fits))
        def _():
            def one(u, carry2):
                bases, _ = _block_windows(base_ref, fits_ref, j0 + u, n_blocks)
                run(SLOT_WIN, [(j0 + u, bases)])
                return carry2
            lax.fori_loop(0, BLOCKS_PER_STEP, one, 0)

        return carry

    lax.fori_loop(0, n_groups, group, 0)


def _token_rows(j):
    return pl.ds(pl.multiple_of(j * ROUTE_BLOCK, ROUTE_BLOCK), ROUTE_BLOCK)


def _gather_kernel(base_ref, fits_ref, slot_ref, h_ref, xin_ref, *, n_blocks):
    xin_ref[...] = jnp.zeros_like(xin_ref)

    def run(win, blocks):
        rows = [_dot(_one_hots(slot_ref, _token_rows(j), bases, win), h_ref[_token_rows(j), :]).astype(BF16)
                for j, bases in blocks]
        for (j, bases), rows_j in zip(blocks, rows):
            for e, base in enumerate(bases):
                xin_ref[0, e, pl.ds(base, win), :] += rows_j[e * win:(e + 1) * win]

    _for_block_groups(base_ref, fits_ref, 0, n_blocks // BLOCKS_PER_STEP, n_blocks, run)


def gather(base, fits, slot_t, h_ext, *, batch, seq, cap, tn=256):
    n_exp = slot_t.shape[1]
    width = h_ext.shape[1]
    rows = cap + SLOT_WIN
    grid_spec = pltpu.PrefetchScalarGridSpec(
        num_scalar_prefetch=2,
        grid=(batch, width // tn),
        in_specs=[pl.BlockSpec((1, n_exp, 1, seq), lambda b, n, base, fits: (b, 0, 0, 0)),
                  pl.BlockSpec((seq, tn), lambda b, n, base, fits: (b, n))],
        out_specs=pl.BlockSpec((1, n_exp, rows, tn), lambda b, n, base, fits: (b, 0, 0, n)),
    )
    return pl.pallas_call(
        functools.partial(_gather_kernel, n_blocks=seq // ROUTE_BLOCK),
        grid_spec=grid_spec,
        out_shape=jax.ShapeDtypeStruct((batch, n_exp, rows, width), BF16),
        compiler_params=pltpu.CompilerParams(
            dimension_semantics=("arbitrary", "arbitrary"), vmem_limit_bytes=V7X_VMEM_LIMIT_BYTES),
    )(base, fits, slot_t, h_ext)


def _ffn_kernel(xin_ref, g_ref, wg32_ref, wu32_ref, wd32_ref, y_ref, wg_ref, wu_ref, wd_ref, *, cap):
    e = pl.program_id(0)

    @pl.when(pl.program_id(1) == 0)
    def _():
        _cast_chunks(wg32_ref.at[0], wg_ref)
        _cast_chunks(wu32_ref.at[0], wu_ref)
        _cast_chunks(wd32_ref.at[0], wd_ref)

    lane = lax.broadcasted_iota(jnp.int32, (1, 2 * LANES), 1)
    mine = (lane == e) | (lane == LANES + e)
    for rb in range(cap // FFN_ROWS):
        rows = slice(rb * FFN_ROWS, (rb + 1) * FFN_ROWS)
        xb = xin_ref[0, 0, rows, :]
        gate = jnp.sum(jnp.where(mine, g_ref[0, 0, rows, :].astype(F32), 0.0), axis=-1, keepdims=True)
        gate_h = _dot(xb, wg_ref[...])
        hid = (gate_h * jax.nn.sigmoid(gate_h)) * _dot(xb, wu_ref[...])
        y_ref[0, 0, rows, :] = (_dot(hid.astype(BF16), wd_ref[...]) * gate).astype(BF16)
    y_ref[0, 0, cap:, :] = jnp.zeros((y_ref.shape[2] - cap, y_ref.shape[3]), BF16)


def expert_ffn(xin, wg, wu, wd, *, cap):
    batch, n_exp, rows, width = xin.shape
    _, d, f = wg.shape
    return pl.pallas_call(
        functools.partial(_ffn_kernel, cap=cap),
        grid=(n_exp, batch),
        in_specs=[pl.BlockSpec((1, 1, cap, d), lambda e, b: (b, e, 0, 0)),
                  pl.BlockSpec((1, 1, cap, 2 * LANES), lambda e, b: (b, e, 0, d // (2 * LANES))),
                  pl.BlockSpec((1, d, f), lambda e, b: (e, 0, 0)),
                  pl.BlockSpec((1, d, f), lambda e, b: (e, 0, 0)),
                  pl.BlockSpec((1, f, d), lambda e, b: (e, 0, 0))],
        out_specs=pl.BlockSpec((1, 1, rows, d), lambda e, b: (b, e, 0, 0)),
        out_shape=jax.ShapeDtypeStruct((batch, n_exp, rows, d), BF16),
        scratch_shapes=[pltpu.VMEM((d, f), BF16), pltpu.VMEM((d, f), BF16), pltpu.VMEM((f, d), BF16)],
        compiler_params=pltpu.CompilerParams(
            dimension_semantics=("arbitrary", "arbitrary"), vmem_limit_bytes=V7X_VMEM_LIMIT_BYTES),
    )(xin, xin, wg, wu, wd)


def _combine_kernel(base_ref, fits_ref, slot_ref, y_ref, x1_ref, out_ref, ywin, *, n_blocks):
    blocks_here = out_ref.shape[0] // ROUTE_BLOCK
    first = pl.program_id(2) * blocks_here

    def run(win, blocks):
        stacked = N_EXPERTS * win
        separate = stacked * len(blocks) <= ywin.shape[0]
        for u, (j, bases) in enumerate(blocks):
            off = u * stacked if separate else 0
            rows = _token_rows(j - first)
            for e, base in enumerate(bases):
                ywin[off + e * win:off + (e + 1) * win, :] = y_ref[0, e, pl.ds(base, win), :]
            hits = _one_hots(slot_ref, _token_rows(j), bases, win)
            out_ref[rows, :] = x1_ref[rows, :] + _dot(hits, ywin[off:off + stacked, :], ((0,), (0,)))

    _for_block_groups(base_ref, fits_ref, first, blocks_here // BLOCKS_PER_STEP, n_blocks, run)


def combine(base, fits, slot_t, y, x1, *, batch, seq, tn=512, tt=1024):
    n_exp, rows, d = y.shape[1], y.shape[2], y.shape[3]
    grid_spec = pltpu.PrefetchScalarGridSpec(
        num_scalar_prefetch=2,
        grid=(batch, d // tn, seq // tt),
        in_specs=[pl.BlockSpec((1, n_exp, 1, seq), lambda b, n, t, base, fits: (b, 0, 0, 0)),
                  pl.BlockSpec((1, n_exp, rows, tn), lambda b, n, t, base, fits: (b, 0, 0, n)),
                  pl.BlockSpec((tt, tn), lambda b, n, t, base, fits: (b * (seq // tt) + t, n))],
        out_specs=pl.BlockSpec((tt, tn), lambda b, n, t, base, fits: (b * (seq // tt) + t, n)),
        scratch_shapes=[pltpu.VMEM((n_exp * SLOT_WIN, tn), BF16)],
    )
    return pl.pallas_call(
        functools.partial(_combine_kernel, n_blocks=seq // ROUTE_BLOCK),
        grid_spec=grid_spec,
        out_shape=jax.ShapeDtypeStruct((batch * seq, d), F32),
        compiler_params=pltpu.CompilerParams(
            dimension_semantics=("arbitrary", "arbitrary", "arbitrary"),
            vmem_limit_bytes=V7X_VMEM_LIMIT_BYTES),
    )(base, fits, slot_t, y, x1)


def kernel(x, norm1_w, w_in, attn_q_norm_w, attn_k_norm_w, hgrn_lb_fwd, hgrn_lb_bwd, hgrn_out_norm_w,
           w_out, norm2_w, w_router, w_expert_gate, w_expert_up, w_expert_down):
    batch, seq, d_model = x.shape
    depth = w_in.shape[0]
    cap = max(1, CAPACITY_FACTOR * seq // N_EXPERTS)
    hgrn_col = 3 * ATTN_WIDTH
    lb_f_all = jnp.cumsum(jax.nn.softmax(hgrn_lb_fwd.astype(F32), axis=0), axis=0)
    lb_b_all = jnp.cumsum(jax.nn.softmax(hgrn_lb_bwd.astype(F32), axis=0), axis=0)

    x2d = x.reshape(batch * seq, d_model)
    for l in range(depth):
        proj = in_proj(x2d, norm1_w[l], w_in[l], attn_q_norm_w[l], attn_k_norm_w[l])
        o_f, o_b = hgrn(proj, lb_f_all[l], lb_b_all[l], batch=batch, seq=seq, col0=hgrn_col)
        a_out = attention(proj, attn_q_norm_w[l], attn_k_norm_w[l], batch=batch, seq=seq)
        x1, h_ext, aff = out_proj(x2d, a_out, o_f, o_b, proj, hgrn_out_norm_w[l], w_out[l],
                                  norm2_w[l], w_router[l], hg_col=hgrn_col + 4 * HGRN_WIDTH)
        slot, start = routing(aff, batch=batch, seq=seq, cap=cap)

        def expert_major(t):
            return t.reshape(batch, -1, LANES)[:, :, :N_EXPERTS].transpose(0, 2, 1)

        slot_t = expert_major(slot).reshape(batch, N_EXPERTS, 1, seq)
        start_t = expert_major(start)
        base_t = start_t // SLOT_ALIGN * SLOT_ALIGN
        end_t = jnp.concatenate([start_t[:, :, 1:], jnp.full((batch, N_EXPERTS, 1), cap, jnp.int32)], axis=2)
        fits = (jnp.max(end_t - base_t, axis=1) <= SLOT_WIN_SMALL).astype(jnp.int32).reshape(-1)
        base = base_t.reshape(-1)
        xin = gather(base, fits, slot_t, h_ext, batch=batch, seq=seq, cap=cap)
        y = expert_ffn(xin, w_expert_gate[l], w_expert_up[l], w_expert_down[l], cap=cap)
        x2d = combine(base, fits, slot_t, y, x1, batch=batch, seq=seq)
    return x2d.reshape(batch, seq, d_model)
```
